```python
import jax, jax.numpy as jnp
from jax import lax
import numpy as np

D_MODEL = 2048
BATCH = 4
SEQ = 4096
DEPTH = 1

CHUNK = 64
EPS = 1e-6
PLE_DIM = 256
POOL_WINDOWS = (2, 4, 8, 16)
POOL_WIDTH = D_MODEL // 2
POOL_GROUP_DIM = POOL_WIDTH // len(POOL_WINDOWS)
SGU_BLOCK = 128
SGU_GROUPS = 8
SGU_WIDTH = D_MODEL // 2
SGU_GROUP_DIM = SGU_WIDTH // SGU_GROUPS
N_IN = POOL_WIDTH + 2 * SGU_WIDTH
N_EXPERT_GROUPS = 4
EXPERTS_PER_GROUP = 8
N_EXPERTS = N_EXPERT_GROUPS * EXPERTS_PER_GROUP
TOP_K_EXPERT = 2
D_EXPERT = D_MODEL // 4
MOE_BLOCK_ROWS = 256

kernel_name = "hybrid_pool_sgu_hmoe_block"


def _rmsnorm(x, g):
    xf = x.astype(jnp.float32)
    y = xf * lax.rsqrt(jnp.mean(xf * xf, axis=-1, keepdims=True) + EPS)
    return (y * g.astype(jnp.float32)).astype(x.dtype)


def _layernorm(x, g, b):
    xf = x.astype(jnp.float32)
    mu = jnp.mean(xf, axis=-1, keepdims=True)
    xc = xf - mu
    var = jnp.mean(xc * xc, axis=-1, keepdims=True)
    y = xc * lax.rsqrt(var + EPS) * g.astype(jnp.float32) + b.astype(jnp.float32)
    return y.astype(x.dtype)


def _pool_mixer(a, w_pool, pool_scale):
    bsz, s, _ = a.shape
    af = a.astype(jnp.float32)
    cs = jnp.cumsum(af, axis=1)
    t = jnp.arange(1, s + 1, dtype=jnp.float32)
    outs = []
    for gi, w in enumerate(POOL_WINDOWS):
        lo, hi = gi * POOL_GROUP_DIM, (gi + 1) * POOL_GROUP_DIM
        cs_g = cs[:, :, lo:hi]
        prev = jnp.pad(cs_g[:, : s - w], ((0, 0), (w, 0), (0, 0)))
        cnt = jnp.minimum(t, w)[None, :, None]
        outs.append((cs_g - prev) / cnt - af[:, :, lo:hi])
    z = jnp.stack(outs, axis=2).astype(a.dtype)
    y = jnp.einsum('bsgc,gcd->bsgd', z, w_pool).reshape(bsz, s, POOL_WIDTH)
    return y * pool_scale


def _spatial_gating(u, v, ln_g, ln_b, w_spatial, b_spatial):
    bsz, s, _ = u.shape
    u = jax.nn.gelu(u)
    v = _layernorm(jax.nn.gelu(v), ln_g, ln_b)
    nblk = s // SGU_BLOCK
    vb = v.reshape(bsz, nblk, SGU_BLOCK, SGU_GROUPS, SGU_GROUP_DIM)
    pos_chunk = jnp.arange(SGU_BLOCK) // CHUNK
    mask = pos_chunk[None, :] <= pos_chunk[:, None]
    ws = jnp.where(mask[None], w_spatial, 0.0)
    vm = jnp.einsum('gts,bnsgc->bntgc', ws, vb) + b_spatial.T[None, None, :, :, None]
    return u * vm.reshape(bsz, s, SGU_WIDTH)


def _hier_moe(h, w_rg, b_rg, w_re, b_re, w_g, w_u, w_d):
    bsz, s, d = h.shape
    n_tok = bsz * s
    hf = h.reshape(n_tok, d)
    grp_logits = (hf @ w_rg).astype(jnp.float32) + b_rg.astype(jnp.float32)
    grp_prob = jax.nn.softmax(grp_logits, axis=-1)
    grp_p, grp_idx = lax.top_k(grp_prob, 1)
    exp_logits = jnp.einsum('td,gde->tge', hf, w_re).astype(jnp.float32) + b_re.astype(jnp.float32)
    sel = jnp.take_along_axis(exp_logits, grp_idx[:, :, None], axis=1)[:, 0]
    top_logit, top_loc = lax.top_k(sel, TOP_K_EXPERT)
    weights = grp_p * jax.nn.softmax(top_logit, axis=-1)
    expert_id = grp_idx * EXPERTS_PER_GROUP + top_loc

    n_assign = n_tok * TOP_K_EXPERT
    n_blocks = -(-n_assign // MOE_BLOCK_ROWS) + N_EXPERTS
    n_rows = n_blocks * MOE_BLOCK_ROWS
    flat_e = expert_id.reshape(-1).astype(jnp.int32)
    flat_tok = jnp.repeat(jnp.arange(n_tok, dtype=jnp.int32), TOP_K_EXPERT)
    flat_w = weights.reshape(-1)
    order = jnp.argsort(flat_e)
    se, st, sw = flat_e[order], flat_tok[order], flat_w[order]
    counts = jnp.bincount(flat_e, length=N_EXPERTS)
    padded = (counts + MOE_BLOCK_ROWS - 1) // MOE_BLOCK_ROWS * MOE_BLOCK_ROWS
    pad_end = jnp.cumsum(padded)
    pad_start = pad_end - padded
    start = jnp.cumsum(counts) - counts
    dest = pad_start[se] + jnp.arange(n_assign, dtype=jnp.int32) - start[se]
    row_tok = jnp.full((n_rows,), n_tok, jnp.int32).at[dest].set(st)
    row_w = jnp.zeros((n_rows,), jnp.float32).at[dest].set(sw)
    block_start = jnp.arange(n_blocks, dtype=jnp.int32) * MOE_BLOCK_ROWS
    block_e = jnp.minimum(jnp.searchsorted(pad_end, block_start, side='right'), N_EXPERTS - 1)
    h_pad = jnp.concatenate([hf, jnp.zeros((1, d), hf.dtype)], axis=0)

    def expert_rows(args):
        e, tok, wt = args
        xb = h_pad[tok]
        hid = jax.nn.silu(xb @ w_g[e]) * (xb @ w_u[e])
        return (hid @ w_d[e]) * wt[:, None].astype(xb.dtype)

    ys = lax.map(expert_rows, (block_e, row_tok.reshape(n_blocks, MOE_BLOCK_ROWS),
                               row_w.reshape(n_blocks, MOE_BLOCK_ROWS)))
    out = jax.ops.segment_sum(ys.reshape(n_rows, d), row_tok, num_segments=n_tok + 1)[:n_tok]
    return out.reshape(bsz, s, d).astype(h.dtype)


def setup_inputs(seed: int = 0) -> dict:
    key = jax.random.key(seed)
    ks = jax.random.split(key, 32)
    L, D = DEPTH, D_MODEL

    def nrm(k, shape, fan_in):
        return jax.random.normal(k, shape, jnp.float32) * (fan_in ** -0.5)

    def gain(k, shape):
        return 1.0 + 0.02 * jax.random.normal(k, shape, jnp.float32)

    def bias(k, shape, scale=0.02):
        return scale * jax.random.normal(k, shape, jnp.float32)

    return {
        "x": jax.random.normal(ks[0], (BATCH, SEQ, D), jnp.float32),
        "p": jax.random.normal(ks[1], (DEPTH, BATCH, SEQ, PLE_DIM), jnp.float32),
        "g_mix": gain(ks[2], (L, D)),
        "w_in": nrm(ks[3], (L, D, N_IN), D),
        "w_pool": nrm(ks[4], (L, len(POOL_WINDOWS), POOL_GROUP_DIM, POOL_GROUP_DIM), POOL_GROUP_DIM),
        "pool_scale": gain(ks[5], (L, POOL_WIDTH)),
        "w_branch_a": nrm(ks[6], (L, POOL_WIDTH, D), POOL_WIDTH),
        "sgu_ln_g": gain(ks[7], (L, SGU_WIDTH)),
        "sgu_ln_b": bias(ks[8], (L, SGU_WIDTH)),
        "w_spatial": nrm(ks[9], (L, SGU_GROUPS, SGU_BLOCK, SGU_BLOCK), SGU_BLOCK),
        "b_spatial": gain(ks[10], (L, SGU_GROUPS, SGU_BLOCK)),
        "w_branch_b": nrm(ks[11], (L, SGU_WIDTH, D), SGU_WIDTH),
        "w_merge_gate": nrm(ks[12], (L, D, 2 * D), D),
        "b_merge_gate": bias(ks[13], (L, 2 * D)),
        "w_out": nrm(ks[14], (L, D, D), D),
        "g_ffn": gain(ks[15], (L, D)),
        "w_router_group": nrm(ks[16], (L, D, N_EXPERT_GROUPS), D),
        "b_router_group": bias(ks[17], (L, N_EXPERT_GROUPS), 0.01),
        "w_router_expert": nrm(ks[18], (L, N_EXPERT_GROUPS, D, EXPERTS_PER_GROUP), D),
        "b_router_expert": bias(ks[19], (L, N_EXPERT_GROUPS, EXPERTS_PER_GROUP), 0.01),
        "w_exp_gate": nrm(ks[20], (L, N_EXPERTS, D, D_EXPERT), D),
        "w_exp_up": nrm(ks[21], (L, N_EXPERTS, D, D_EXPERT), D),
        "w_exp_down": nrm(ks[22], (L, N_EXPERTS, D_EXPERT, D), D_EXPERT),
        "g_ple": gain(ks[23], (L, D)),
        "w_ple_gate": nrm(ks[24], (L, D, D), D),
        "b_ple_gate": bias(ks[25], (L, D)),
        "w_ple_up": nrm(ks[26], (L, PLE_DIM, D), PLE_DIM),
        "g_final": gain(ks[27], (D,)),
    }


def reference(x, p, g_mix, w_in, w_pool, pool_scale, w_branch_a, sgu_ln_g, sgu_ln_b, w_spatial,
              b_spatial, w_branch_b, w_merge_gate, b_merge_gate, w_out, g_ffn, w_router_group,
              b_router_group, w_router_expert, b_router_expert, w_exp_gate, w_exp_up, w_exp_down,
              g_ple, w_ple_gate, b_ple_gate, w_ple_up, g_final):
    for i in range(DEPTH):
        h = _rmsnorm(x, g_mix[i])
        z = h @ w_in[i]
        a_in = z[..., :POOL_WIDTH]
        u = z[..., POOL_WIDTH:POOL_WIDTH + SGU_WIDTH]
        v = z[..., POOL_WIDTH + SGU_WIDTH:]
        y_a = _pool_mixer(a_in, w_pool[i], pool_scale[i]) @ w_branch_a[i]
        y_b = _spatial_gating(u, v, sgu_ln_g[i], sgu_ln_b[i], w_spatial[i], b_spatial[i]) @ w_branch_b[i]
        gates = jax.nn.sigmoid(h @ w_merge_gate[i] + b_merge_gate[i])
        merged = gates[..., :D_MODEL] * y_a + gates[..., D_MODEL:] * y_b
        x = x + merged @ w_out[i]
        x = x + _hier_moe(_rmsnorm(x, g_ffn[i]), w_router_group[i], b_router_group[i],
                          w_router_expert[i], b_router_expert[i], w_exp_gate[i], w_exp_up[i], w_exp_down[i])
        gate_p = jax.nn.sigmoid(_rmsnorm(x, g_ple[i]) @ w_ple_gate[i] + b_ple_gate[i])
        x = x + gate_p * (p[i] @ w_ple_up[i])
    return _rmsnorm(x, g_final)
```

```python
import functools

import jax
import jax.numpy as jnp
from jax import lax
from jax.experimental import pallas as pl
from jax.experimental.pallas import tpu as pltpu

F32 = jnp.float32
BF16 = jnp.bfloat16

D_MODEL = 2048
SEQ = 4096
EPS = 1e-6
PLE_DIM = 256
POOL_WINDOWS = (2, 4, 8, 16)
POOL_WIDTH = D_MODEL // 2
POOL_GROUP_DIM = POOL_WIDTH // len(POOL_WINDOWS)
POOL_HISTORY = max(POOL_WINDOWS)
SGU_BLOCK = 128
SGU_CHUNK = 64
SGU_GROUPS = 8
SGU_WIDTH = D_MODEL // 2
SGU_GROUP_DIM = SGU_WIDTH // SGU_GROUPS
N_IN = POOL_WIDTH + 2 * SGU_WIDTH
N_EXPERT_GROUPS = 4
EXPERTS_PER_GROUP = 8
N_EXPERTS = N_EXPERT_GROUPS * EXPERTS_PER_GROUP
TOP_K = 2
D_EXPERT = D_MODEL // 4
MOE_BLOCK_ROWS = 256

LANES = 128
ROUTER_EXPERT_LANE0 = N_EXPERT_GROUPS
VMEM_LIMIT_BYTES = 56 * 1024 * 1024

TM_FRONT = 256
TM_BACK = 256
TM_TAIL = 256
MERGE_CHUNK = 512


def _rms_scale(x):
    return x * lax.rsqrt(jnp.mean(x * x, axis=-1, keepdims=True) + EPS)


def _resident(shape):
    zeros = (0,) * len(shape)
    return pl.BlockSpec(shape, lambda *_: zeros, pipeline_mode=pl.Buffered(1))


def _mixer_front_kernel(x_ref, gmix_ref, win_ref, wpool_ref, pscale_ref, lng_ref, lnb_ref, ws_ref,
                        bsp_ref, za_ref, sb_ref, hist_ref):
    tm = x_ref.shape[0]
    tiles_per_seq = SEQ // tm
    seq_tile = lax.rem(pl.program_id(0), tiles_per_seq)

    @pl.when(seq_tile == 0)
    def _():
        hist_ref[...] = jnp.zeros_like(hist_ref)

    h = (_rms_scale(x_ref[...]) * gmix_ref[...]).astype(BF16)
    z = jnp.dot(h, win_ref[...], preferred_element_type=F32)

    a = z[:, :POOL_WIDTH]
    ext = jnp.concatenate([hist_ref[...], a], axis=0)
    hist_ref[...] = a[tm - POOL_HISTORY:, :]
    frames = (seq_tile * tm + 1 + lax.broadcasted_iota(jnp.int32, (tm, 1), 0)).astype(F32)
    for gi, w in enumerate(POOL_WINDOWS):
        cols = slice(gi * POOL_GROUP_DIM, (gi + 1) * POOL_GROUP_DIM)
        s = ext[:, cols]
        k = 1
        while k < w:
            s = s + pltpu.roll(s, k, 0)
            k *= 2
        wsum = s[POOL_HISTORY:, :]
        zg = wsum / jnp.minimum(frames, float(w)) - a[:, cols]
        yg = jnp.dot(zg.astype(BF16), wpool_ref[gi], preferred_element_type=F32)
        za_ref[:, cols] = (yg * pscale_ref[:, cols]).astype(BF16)

    u = jax.nn.gelu(z[:, POOL_WIDTH:POOL_WIDTH + SGU_WIDTH])
    v = jax.nn.gelu(z[:, POOL_WIDTH + SGU_WIDTH:])
    vc = v - jnp.mean(v, axis=-1, keepdims=True)
    var = jnp.mean(vc * vc, axis=-1, keepdims=True)
    vn = (vc * lax.rsqrt(var + EPS) * lng_ref[...] + lnb_ref[...]).astype(BF16)
    t_chunk = lax.broadcasted_iota(jnp.int32, (SGU_BLOCK, SGU_BLOCK), 0) // SGU_CHUNK
    s_chunk = lax.broadcasted_iota(jnp.int32, (SGU_BLOCK, SGU_BLOCK), 1) // SGU_CHUNK
    causal = s_chunk <= t_chunk
    nblk = tm // SGU_BLOCK
    for g in range(SGU_GROUPS):
        cols = slice(g * SGU_GROUP_DIM, (g + 1) * SGU_GROUP_DIM)
        wsg = jnp.where(causal, ws_ref[g], 0.0).astype(BF16)
        vg = jnp.concatenate([vn[j * SGU_BLOCK:(j + 1) * SGU_BLOCK, cols] for j in range(nblk)], axis=1)
        vm = jnp.dot(wsg, vg, preferred_element_type=F32) + bsp_ref[g]
        for j in range(nblk):
            rows = slice(j * SGU_BLOCK, (j + 1) * SGU_BLOCK)
            sb_ref[rows, cols] = (u[rows, cols] * vm[:, j * SGU_GROUP_DIM:(j + 1) * SGU_GROUP_DIM]).astype(BF16)


def _mixer_front(x2d, g_mix, w_in, w_pool, pool_scale, ln_g, ln_b, w_spatial, b_spatial):
    n_tok = x2d.shape[0]
    tm = TM_FRONT
    row = lambda i: (i, 0)
    return pl.pallas_call(
        _mixer_front_kernel,
        grid=(n_tok // tm,),
        in_specs=[
            pl.BlockSpec((tm, D_MODEL), row),
            _resident((1, D_MODEL)),
            _resident((D_MODEL, N_IN)),
            _resident((len(POOL_WINDOWS), POOL_GROUP_DIM, POOL_GROUP_DIM)),
            _resident((1, POOL_WIDTH)),
            _resident((1, SGU_WIDTH)),
            _resident((1, SGU_WIDTH)),
            _resident((SGU_GROUPS, SGU_BLOCK, SGU_BLOCK)),
            _resident((SGU_GROUPS, SGU_BLOCK, 1)),
        ],
        out_specs=[pl.BlockSpec((tm, POOL_WIDTH), row), pl.BlockSpec((tm, SGU_WIDTH), row)],
        out_shape=[jax.ShapeDtypeStruct((n_tok, POOL_WIDTH), BF16),
                   jax.ShapeDtypeStruct((n_tok, SGU_WIDTH), BF16)],
        scratch_shapes=[pltpu.VMEM((POOL_HISTORY, POOL_WIDTH), F32)],
        compiler_params=pltpu.CompilerParams(dimension_semantics=("arbitrary",),
                                             vmem_limit_bytes=VMEM_LIMIT_BYTES),
        name="mixer_front",
    )(x2d, g_mix, w_in, w_pool, pool_scale, ln_g, ln_b, w_spatial, b_spatial)


def _route(logits):
    lane = lax.broadcasted_iota(jnp.int32, logits.shape, 1).astype(F32)
    neg = -jnp.inf
    far = float(LANES)

    def first_argmax(vals):
        top = jnp.max(vals, axis=-1, keepdims=True)
        return top, jnp.min(jnp.where(vals == top, lane, far), axis=-1, keepdims=True)

    is_grp = lane < float(N_EXPERT_GROUPS)
    g_top, g_idx = first_argmax(jnp.where(is_grp, logits, neg))
    g_den = jnp.sum(jnp.where(is_grp, jnp.exp(logits - g_top), 0.0), axis=-1, keepdims=True)
    grp_p = 1.0 / g_den
    lo = float(ROUTER_EXPERT_LANE0) + g_idx * float(EXPERTS_PER_GROUP)
    e_log = jnp.where(lane >= lo, jnp.where(lane < lo + float(EXPERTS_PER_GROUP), logits, neg), neg)
    t1, i1 = first_argmax(e_log)
    t2, i2 = first_argmax(jnp.where(lane == i1, neg, e_log))
    r = jnp.exp(t2 - t1)
    w1 = grp_p / (1.0 + r)
    w2 = grp_p * r / (1.0 + r)
    e1 = i1 - float(ROUTER_EXPERT_LANE0)
    e2 = i2 - float(ROUTER_EXPERT_LANE0)
    eid = jnp.where(lane == 0.0, e1, jnp.where(lane == 1.0, e2, 0.0)).astype(jnp.int32)
    wts = jnp.where(lane == 0.0, w1, jnp.where(lane == 1.0, w2, 0.0))
    return eid, wts


def _mixer_back_kernel(x_ref, za_ref, sb_ref, gmix_ref, wm_ref, bm_ref, wa_ref, wb_ref, wo_ref,
                       gffn_ref, wr_ref, br_ref, x1_ref, eid_ref, wts_ref):
    x = x_ref[...]
    h = (_rms_scale(x) * gmix_ref[...]).astype(BF16)
    za = za_ref[...]
    sb = sb_ref[...]
    acc = jnp.zeros(x.shape, F32)
    for c in range(D_MODEL // MERGE_CHUNK):
        ca = slice(c * MERGE_CHUNK, (c + 1) * MERGE_CHUNK)
        cb = slice(D_MODEL + c * MERGE_CHUNK, D_MODEL + (c + 1) * MERGE_CHUNK)
        ga = jax.nn.sigmoid(jnp.dot(h, wm_ref[:, ca], preferred_element_type=F32) + bm_ref[:, ca])
        gb = jax.nn.sigmoid(jnp.dot(h, wm_ref[:, cb], preferred_element_type=F32) + bm_ref[:, cb])
        ya = jnp.dot(za, wa_ref[:, ca], preferred_element_type=F32)
        yb = jnp.dot(sb, wb_ref[:, ca], preferred_element_type=F32)
        merged = (ga * ya + gb * yb).astype(BF16)
        acc = acc + jnp.dot(merged, wo_ref[ca, :], preferred_element_type=F32)
    x1 = x + acc
    x1_ref[...] = x1
    h2 = _rms_scale(x1) * gffn_ref[...]
    logits = jnp.dot(h2, wr_ref[...], preferred_element_type=F32,
                     precision=lax.Precision.HIGHEST) + br_ref[...]
    eid, wts = _route(logits)
    eid_ref[...] = eid
    wts_ref[...] = wts


def _mixer_back(x2d, za, sb, g_mix, w_merge, b_merge, w_a, w_b, w_out, g_ffn, w_router, b_router):
    n_tok = x2d.shape[0]
    tm = TM_BACK
    row = lambda i: (i, 0)
    return pl.pallas_call(
        _mixer_back_kernel,
        grid=(n_tok // tm,),
        in_specs=[
            pl.BlockSpec((tm, D_MODEL), row),
            pl.BlockSpec((tm, POOL_WIDTH), row),
            pl.BlockSpec((tm, SGU_WIDTH), row),
            _resident((1, D_MODEL)),
            _resident((D_MODEL, 2 * D_MODEL)),
            _resident((1, 2 * D_MODEL)),
            _resident((POOL_WIDTH, D_MODEL)),
            _resident((SGU_WIDTH, D_MODEL)),
            _resident((D_MODEL, D_MODEL)),
            _resident((1, D_MODEL)),
            _resident((D_MODEL, LANES)),
            _resident((1, LANES)),
        ],
        out_specs=[pl.BlockSpec((tm, D_MODEL), row), pl.BlockSpec((tm, LANES), row),
                   pl.BlockSpec((tm, LANES), row)],
        out_shape=[jax.ShapeDtypeStruct((n_tok, D_MODEL), F32),
                   jax.ShapeDtypeStruct((n_tok, LANES), jnp.int32),
                   jax.ShapeDtypeStruct((n_tok, LANES), F32)],
        compiler_params=pltpu.CompilerParams(dimension_semantics=("arbitrary",),
                                             vmem_limit_bytes=VMEM_LIMIT_BYTES),
        name="mixer_back",
    )(x2d, za, sb, g_mix, w_merge, b_merge, w_a, w_b, w_out, g_ffn, w_router, b_router)


def _row_gather_copy(src_hbm, src_row, dst_vmem, dst_row, sem):
    return pltpu.make_async_copy(src_hbm.at[pl.ds(src_row, 1), :], dst_vmem.at[pl.ds(dst_row, 1), :], sem)


def _expert_kernel(be_ref, nused_ref, rowtok_ref, x1_hbm, gffn_ref, wg_ref, wu_ref, wd_ref, roww_ref,
                   ys_ref, xbuf, sems, wg_bf, wu_bf, wd_bf):
    rows = MOE_BLOCK_ROWS
    b = pl.program_id(0)
    n_used = nused_ref[0]
    slot = lax.rem(b, 2)

    def start_gather(blk, slt):
        def body(r, carry):
            _row_gather_copy(x1_hbm, rowtok_ref[blk * rows + r], xbuf.at[slt], r, sems.at[slt]).start()
            return carry
        lax.fori_loop(0, rows, body, 0)

    @pl.when(b == 0)
    def _():
        start_gather(0, 0)

    @pl.when(b + 1 < n_used)
    def _():
        start_gather(b + 1, 1 - slot)

    @pl.when(b < n_used)
    def _():
        pltpu.make_async_copy(x1_hbm.at[pl.ds(0, rows), :], xbuf.at[slot], sems.at[slot]).wait()

        @pl.when(jnp.logical_or(b == 0, be_ref[b] != be_ref[jnp.maximum(b - 1, 0)]))
        def _():
            wg_bf[...] = wg_ref[...].astype(BF16)
            wu_bf[...] = wu_ref[...].astype(BF16)
            wd_bf[...] = wd_ref[...].astype(BF16)

        h2 = (_rms_scale(xbuf[slot]) * gffn_ref[...]).astype(BF16)
        gate = jnp.dot(h2, wg_bf[...], preferred_element_type=F32)
        up = jnp.dot(h2, wu_bf[...], preferred_element_type=F32)
        hid = (jax.nn.silu(gate) * up).astype(BF16)
        ys_ref[...] = jnp.dot(hid, wd_bf[...], preferred_element_type=F32) * roww_ref[...]

    @pl.when(b >= n_used)
    def _():
        ys_ref[...] = jnp.zeros_like(ys_ref)


def _expert_ffn(block_e, n_used, row_tok, x1, g_ffn, w_g, w_u, w_d, row_w):
    n_rows = row_tok.shape[0]
    n_blocks = n_rows // MOE_BLOCK_ROWS
    by_expert = lambda b, be, nu, rt: (be[b], 0, 0)
    grid_spec = pltpu.PrefetchScalarGridSpec(
        num_scalar_prefetch=3,
        grid=(n_blocks,),
        in_specs=[
            pl.BlockSpec(memory_space=pl.ANY),
            pl.BlockSpec((1, D_MODEL), lambda b, *_: (0, 0)),
            pl.BlockSpec((None, D_MODEL, D_EXPERT), by_expert),
            pl.BlockSpec((None, D_MODEL, D_EXPERT), by_expert),
            pl.BlockSpec((None, D_EXPERT, D_MODEL), by_expert),
            pl.BlockSpec((MOE_BLOCK_ROWS, 1), lambda b, *_: (b, 0)),
        ],
        out_specs=pl.BlockSpec((MOE_BLOCK_ROWS, D_MODEL), lambda b, *_: (b, 0)),
        scratch_shapes=[
            pltpu.VMEM((2, MOE_BLOCK_ROWS, D_MODEL), F32),
            pltpu.SemaphoreType.DMA((2,)),
            pltpu.VMEM((D_MODEL, D_EXPERT), BF16),
            pltpu.VMEM((D_MODEL, D_EXPERT), BF16),
            pltpu.VMEM((D_EXPERT, D_MODEL), BF16),
        ],
    )
    return pl.pallas_call(
        _expert_kernel,
        grid_spec=grid_spec,
        out_shape=jax.ShapeDtypeStruct((n_rows, D_MODEL), F32),
        compiler_params=pltpu.CompilerParams(dimension_semantics=("arbitrary",),
                                             vmem_limit_bytes=VMEM_LIMIT_BYTES),
        name="expert_ffn",
    )(block_e, n_used, row_tok, x1, g_ffn, w_g, w_u, w_d, row_w)


def _tail_kernel(dest_ref, x1_ref, ys_hbm, p_ref, gple_ref, wpg_ref, bpg_ref, wpu_ref, gfin_ref,
                 out_ref, ybuf, sems):
    tm = x1_ref.shape[0]
    i = pl.program_id(0)
    n_steps = pl.num_programs(0)
    slot = lax.rem(i, 2)

    def start_gather(step, slt):
        def body(r, carry):
            tok = step * tm + r
            for k in range(TOP_K):
                _row_gather_copy(ys_hbm, dest_ref[TOP_K * tok + k], ybuf.at[slt], k * tm + r,
                                 sems.at[slt]).start()
            return carry
        lax.fori_loop(0, tm, body, 0)

    @pl.when(i == 0)
    def _():
        start_gather(0, 0)

    @pl.when(i + 1 < n_steps)
    def _():
        start_gather(i + 1, 1 - slot)

    pltpu.make_async_copy(ys_hbm.at[pl.ds(0, TOP_K * tm), :], ybuf.at[slot], sems.at[slot]).wait()
    x2 = x1_ref[...] + ybuf[slot, :tm, :] + ybuf[slot, tm:, :]
    hn = (_rms_scale(x2) * gple_ref[...]).astype(BF16)
    gate = jax.nn.sigmoid(jnp.dot(hn, wpg_ref[...], preferred_element_type=F32) + bpg_ref[...])
    up = jnp.dot(p_ref[...].astype(BF16), wpu_ref[...], preferred_element_type=F32)
    x3 = x2 + gate * up
    out_ref[...] = _rms_scale(x3) * gfin_ref[...]


def _tail(dest, x1, ys, p2d, g_ple, w_pg, b_pg, w_pu, g_final):
    n_tok = x1.shape[0]
    tm = TM_TAIL
    row = lambda i, *_: (i, 0)
    const = lambda i, *_: (0, 0)
    grid_spec = pltpu.PrefetchScalarGridSpec(
        num_scalar_prefetch=1,
        grid=(n_tok // tm,),
        in_specs=[
            pl.BlockSpec((tm, D_MODEL), row),
            pl.BlockSpec(memory_space=pl.ANY),
            pl.BlockSpec((tm, PLE_DIM), row),
            pl.BlockSpec((1, D_MODEL), const),
            pl.BlockSpec((D_MODEL, D_MODEL), const, pipeline_mode=pl.Buffered(1)),
            pl.BlockSpec((1, D_MODEL), const),
            pl.BlockSpec((PLE_DIM, D_MODEL), const),
            pl.BlockSpec((1, D_MODEL), const),
        ],
        out_specs=pl.BlockSpec((tm, D_MODEL), row),
        scratch_shapes=[pltpu.VMEM((2, TOP_K * tm, D_MODEL), F32), pltpu.SemaphoreType.DMA((2,))],
    )
    return pl.pallas_call(
        _tail_kernel,
        grid_spec=grid_spec,
        out_shape=jax.ShapeDtypeStruct((n_tok, D_MODEL), F32),
        compiler_params=pltpu.CompilerParams(dimension_semantics=("arbitrary",),
                                             vmem_limit_bytes=VMEM_LIMIT_BYTES),
        name="tail",
    )(dest, x1, ys, p2d, g_ple, w_pg, b_pg, w_pu, g_final)


def _dispatch_plan(expert_id, weights):
    n_tok = expert_id.shape[0]
    n_assign = n_tok * TOP_K
    n_blocks = -(-n_assign // MOE_BLOCK_ROWS) + N_EXPERTS
    n_rows = n_blocks * MOE_BLOCK_ROWS
    flat_e = expert_id.reshape(-1)
    flat_w = weights.reshape(-1)
    order = jnp.argsort(flat_e, stable=True).astype(jnp.int32)
    se = flat_e[order]
    counts = jnp.sum((flat_e[:, None] == jnp.arange(N_EXPERTS, dtype=jnp.int32)[None, :]).astype(jnp.int32), axis=0)
    padded = (counts + MOE_BLOCK_ROWS - 1) // MOE_BLOCK_ROWS * MOE_BLOCK_ROWS
    pad_end = jnp.cumsum(padded)
    pad_start = pad_end - padded
    start = jnp.cumsum(counts) - counts
    row_of_sorted = (pad_start[se] + jnp.arange(n_assign, dtype=jnp.int32) - start[se]).astype(jnp.int32)
    row_tok = jnp.zeros((n_rows,), jnp.int32).at[row_of_sorted].set(order // TOP_K)
    row_w = jnp.zeros((n_rows,), F32).at[row_of_sorted].set(flat_w[order])
    dest = jnp.zeros((n_assign,), jnp.int32).at[order].set(row_of_sorted)
    n_used = (pad_end[-1] // MOE_BLOCK_ROWS).astype(jnp.int32)
    block_start = jnp.arange(n_blocks, dtype=jnp.int32) * MOE_BLOCK_ROWS
    block_e = jnp.minimum(jnp.searchsorted(pad_end, block_start, side='right'), N_EXPERTS - 1).astype(jnp.int32)
    block_e = jnp.where(jnp.arange(n_blocks) < n_used, block_e, block_e[jnp.maximum(n_used - 1, 0)])
    return block_e, n_used.reshape(1), row_tok, row_w.reshape(n_rows, 1), dest


def kernel(x, p, g_mix, w_in, w_pool, pool_scale, w_branch_a, sgu_ln_g, sgu_ln_b, w_spatial, b_spatial, w_branch_b, w_merge_gate, b_merge_gate, w_out, g_ffn, w_router_group, b_router_group, w_router_expert, b_router_expert, w_exp_gate, w_exp_up, w_exp_down, g_ple, w_ple_gate, b_ple_gate, w_ple_up, g_final):
    bsz, seq, d = x.shape
    assert (seq, d) == (SEQ, D_MODEL) and g_mix.shape[0] == 1
    n_tok = bsz * seq
    x2d = x.reshape(n_tok, d)
    row2d = lambda v: v.reshape(1, -1)

    za, sb = _mixer_front(
        x2d, row2d(g_mix[0]), w_in[0].astype(BF16), w_pool[0].astype(BF16), row2d(pool_scale[0]),
        row2d(sgu_ln_g[0]), row2d(sgu_ln_b[0]), w_spatial[0], b_spatial[0][:, :, None])

    w_router = jnp.concatenate(
        [w_router_group[0], jnp.transpose(w_router_expert[0], (1, 0, 2)).reshape(d, N_EXPERTS)], axis=1)
    b_router = jnp.concatenate([b_router_group[0], b_router_expert[0].reshape(N_EXPERTS)])
    pad = LANES - w_router.shape[1]
    w_router = jnp.pad(w_router, ((0, 0), (0, pad)))
    b_router = jnp.pad(b_router, (0, pad))

    x1, eid, wts = _mixer_back(
        x2d, za, sb, row2d(g_mix[0]), w_merge_gate[0].astype(BF16), row2d(b_merge_gate[0]),
        w_branch_a[0].astype(BF16), w_branch_b[0].astype(BF16), w_out[0].astype(BF16), row2d(g_ffn[0]),
        w_router, row2d(b_router))

    block_e, n_used, row_tok, row_w, dest = _dispatch_plan(eid[:, :TOP_K], wts[:, :TOP_K])
    ys = _expert_ffn(block_e, n_used, row_tok, x1, row2d(g_ffn[0]), w_exp_gate[0], w_exp_up[0],
                     w_exp_down[0], row_w)
    out = _tail(dest, x1, ys, p[0].reshape(n_tok, PLE_DIM), row2d(g_ple[0]), w_ple_gate[0].astype(BF16),
                row2d(b_ple_gate[0]), w_ple_up[0].astype(BF16), row2d(g_final))
    return out.reshape(bsz, seq, d)
```

```python
import functools

import jax
import jax.numpy as jnp
from jax import lax
from jax.experimental import pallas as pl
from jax.experimental.pallas import tpu as pltpu

F32 = jnp.float32
BF16 = jnp.bfloat16

D_MODEL = 2048
SEQ = 4096
EPS = 1e-6
PLE_DIM = 256
POOL_WINDOWS = (2, 4, 8, 16)
POOL_WIDTH = D_MODEL // 2
POOL_GROUP_DIM = POOL_WIDTH // len(POOL_WINDOWS)
POOL_HISTORY = max(POOL_WINDOWS)
SGU_BLOCK = 128
SGU_CHUNK = 64
SGU_GROUPS = 8
SGU_WIDTH = D_MODEL // 2
SGU_GROUP_DIM = SGU_WIDTH // SGU_GROUPS
N_IN = POOL_WIDTH + 2 * SGU_WIDTH
N_EXPERT_GROUPS = 4
EXPERTS_PER_GROUP = 8
N_EXPERTS = N_EXPERT_GROUPS * EXPERTS_PER_GROUP
TOP_K = 2
D_EXPERT = D_MODEL // 4
MOE_BLOCK_ROWS = 256

LANES = 128
ROUTER_EXPERT_LANE0 = N_EXPERT_GROUPS
VMEM_LIMIT_BYTES = 56 * 1024 * 1024

TM_FRONT = 256
TM_BACK = 256
TM_TAIL = 256
MERGE_CHUNK = 512
GATHER_UNROLL = 8


def _rms_scale(x):
    return x * lax.rsqrt(jnp.mean(x * x, axis=-1, keepdims=True) + EPS)


def _resident(shape):
    zeros = (0,) * len(shape)
    return pl.BlockSpec(shape, lambda *_: zeros, pipeline_mode=pl.Buffered(1))


def _mixer_front_kernel(x_ref, gmix_ref, win_ref, wpool_ref, pscale_ref, lng_ref, lnb_ref, ws_ref,
                        bsp_ref, za_ref, sb_ref, hist_ref):
    tm = x_ref.shape[0]
    tiles_per_seq = SEQ // tm
    seq_tile = lax.rem(pl.program_id(0), tiles_per_seq)

    @pl.when(seq_tile == 0)
    def _():
        hist_ref[...] = jnp.zeros_like(hist_ref)

    h = (_rms_scale(x_ref[...]) * gmix_ref[...]).astype(BF16)
    z = jnp.dot(h, win_ref[...], preferred_element_type=F32)

    a = z[:, :POOL_WIDTH]
    ext = jnp.concatenate([hist_ref[...], a], axis=0)
    hist_ref[...] = a[tm - POOL_HISTORY:, :]
    frames = (seq_tile * tm + 1 + lax.broadcasted_iota(jnp.int32, (tm, 1), 0)).astype(F32)
    for gi, w in enumerate(POOL_WINDOWS):
        cols = slice(gi * POOL_GROUP_DIM, (gi + 1) * POOL_GROUP_DIM)
        s = ext[:, cols]
        k = 1
        while k < w:
            s = s + pltpu.roll(s, k, 0)
            k *= 2
        wsum = s[POOL_HISTORY:, :]
        zg = wsum / jnp.minimum(frames, float(w)) - a[:, cols]
        yg = jnp.dot(zg.astype(BF16), wpool_ref[gi], preferred_element_type=F32)
        za_ref[:, cols] = (yg * pscale_ref[:, cols]).astype(BF16)

    u = jax.nn.gelu(z[:, POOL_WIDTH:POOL_WIDTH + SGU_WIDTH])
    v = jax.nn.gelu(z[:, POOL_WIDTH + SGU_WIDTH:])
    vc = v - jnp.mean(v, axis=-1, keepdims=True)
    var = jnp.mean(vc * vc, axis=-1, keepdims=True)
    vn = (vc * lax.rsqrt(var + EPS) * lng_ref[...] + lnb_ref[...]).astype(BF16)
    t_chunk = lax.broadcasted_iota(jnp.int32, (SGU_BLOCK, SGU_BLOCK), 0) // SGU_CHUNK
    s_chunk = lax.broadcasted_iota(jnp.int32, (SGU_BLOCK, SGU_BLOCK), 1) // SGU_CHUNK
    causal = s_chunk <= t_chunk
    nblk = tm // SGU_BLOCK
    for g in range(SGU_GROUPS):
        cols = slice(g * SGU_GROUP_DIM, (g + 1) * SGU_GROUP_DIM)
        wsg = jnp.where(causal, ws_ref[g], 0.0).astype(BF16)
        vg = jnp.concatenate([vn[j * SGU_BLOCK:(j + 1) * SGU_BLOCK, cols] for j in range(nblk)], axis=1)
        vm = jnp.dot(wsg, vg, preferred_element_type=F32) + bsp_ref[g]
        for j in range(nblk):
            rows = slice(j * SGU_BLOCK, (j + 1) * SGU_BLOCK)
            sb_ref[rows, cols] = (u[rows, cols] * vm[:, j * SGU_GROUP_DIM:(j + 1) * SGU_GROUP_DIM]).astype(BF16)


def _mixer_front(x2d, g_mix, w_in, w_pool, pool_scale, ln_g, ln_b, w_spatial, b_spatial):
    n_tok = x2d.shape[0]
    tm = TM_FRONT
    row = lambda i: (i, 0)
    return pl.pallas_call(
        _mixer_front_kernel,
        grid=(n_tok // tm,),
        in_specs=[
            pl.BlockSpec((tm, D_MODEL), row),
            _resident((1, D_MODEL)),
            _resident((D_MODEL, N_IN)),
            _resident((len(POOL_WINDOWS), POOL_GROUP_DIM, POOL_GROUP_DIM)),
            _resident((1, POOL_WIDTH)),
            _resident((1, SGU_WIDTH)),
            _resident((1, SGU_WIDTH)),
            _resident((SGU_GROUPS, SGU_BLOCK, SGU_BLOCK)),
            _resident((SGU_GROUPS, SGU_BLOCK, 1)),
        ],
        out_specs=[pl.BlockSpec((tm, POOL_WIDTH), row), pl.BlockSpec((tm, SGU_WIDTH), row)],
        out_shape=[jax.ShapeDtypeStruct((n_tok, POOL_WIDTH), BF16),
                   jax.ShapeDtypeStruct((n_tok, SGU_WIDTH), BF16)],
        scratch_shapes=[pltpu.VMEM((POOL_HISTORY, POOL_WIDTH), F32)],
        compiler_params=pltpu.CompilerParams(dimension_semantics=("arbitrary",),
                                             vmem_limit_bytes=VMEM_LIMIT_BYTES),
        name="mixer_front",
    )(x2d, g_mix, w_in, w_pool, pool_scale, ln_g, ln_b, w_spatial, b_spatial)


def _route(logits):
    lane = lax.broadcasted_iota(jnp.int32, logits.shape, 1).astype(F32)
    neg = -jnp.inf
    far = float(LANES)

    def first_argmax(vals):
        top = jnp.max(vals, axis=-1, keepdims=True)
        return top, jnp.min(jnp.where(vals == top, lane, far), axis=-1, keepdims=True)

    is_grp = lane < float(N_EXPERT_GROUPS)
    g_top, g_idx = first_argmax(jnp.where(is_grp, logits, neg))
    g_den = jnp.sum(jnp.where(is_grp, jnp.exp(logits - g_top), 0.0), axis=-1, keepdims=True)
    grp_p = 1.0 / g_den
    lo = float(ROUTER_EXPERT_LANE0) + g_idx * float(EXPERTS_PER_GROUP)
    e_log = jnp.where(lane >= lo, jnp.where(lane < lo + float(EXPERTS_PER_GROUP), logits, neg), neg)
    t1, i1 = first_argmax(e_log)
    t2, i2 = first_argmax(jnp.where(lane == i1, neg, e_log))
    r = jnp.exp(t2 - t1)
    w1 = grp_p / (1.0 + r)
    w2 = grp_p * r / (1.0 + r)
    e1 = i1 - float(ROUTER_EXPERT_LANE0)
    e2 = i2 - float(ROUTER_EXPERT_LANE0)
    eid = jnp.where(lane == 0.0, e1, jnp.where(lane == 1.0, e2, 0.0)).astype(jnp.int32)
    wts = jnp.where(lane == 0.0, w1, jnp.where(lane == 1.0, w2, 0.0))
    return eid, wts


def _mixer_back_kernel(x_ref, za_ref, sb_ref, gmix_ref, wm_ref, bm_ref, wa_ref, wb_ref, wo_ref,
                       gffn_ref, wr_ref, br_ref, x1_ref, eid_ref, wts_ref):
    x = x_ref[...]
    h = (_rms_scale(x) * gmix_ref[...]).astype(BF16)
    za = za_ref[...]
    sb = sb_ref[...]
    acc = jnp.zeros(x.shape, F32)
    for c in range(D_MODEL // MERGE_CHUNK):
        ca = slice(c * MERGE_CHUNK, (c + 1) * MERGE_CHUNK)
        cb = slice(D_MODEL + c * MERGE_CHUNK, D_MODEL + (c + 1) * MERGE_CHUNK)
        ga = jax.nn.sigmoid(jnp.dot(h, wm_ref[:, ca], preferred_element_type=F32) + bm_ref[:, ca])
        gb = jax.nn.sigmoid(jnp.dot(h, wm_ref[:, cb], preferred_element_type=F32) + bm_ref[:, cb])
        ya = jnp.dot(za, wa_ref[:, ca], preferred_element_type=F32)
        yb = jnp.dot(sb, wb_ref[:, ca], preferred_element_type=F32)
        merged = (ga * ya + gb * yb).astype(BF16)
        acc = acc + jnp.dot(merged, wo_ref[ca, :], preferred_element_type=F32)
    x1 = x + acc
    x1_ref[...] = x1
    h2 = _rms_scale(x1) * gffn_ref[...]
    logits = jnp.dot(h2, wr_ref[...], preferred_element_type=F32,
                     precision=lax.Precision.HIGHEST) + br_ref[...]
    eid, wts = _route(logits)
    eid_ref[...] = eid
    wts_ref[...] = wts


def _mixer_back(x2d, za, sb, g_mix, w_merge, b_merge, w_a, w_b, w_out, g_ffn, w_router, b_router):
    n_tok = x2d.shape[0]
    tm = TM_BACK
    row = lambda i: (i, 0)
    return pl.pallas_call(
        _mixer_back_kernel,
        grid=(n_tok // tm,),
        in_specs=[
            pl.BlockSpec((tm, D_MODEL), row),
            pl.BlockSpec((tm, POOL_WIDTH), row),
            pl.BlockSpec((tm, SGU_WIDTH), row),
            _resident((1, D_MODEL)),
            _resident((D_MODEL, 2 * D_MODEL)),
            _resident((1, 2 * D_MODEL)),
            _resident((POOL_WIDTH, D_MODEL)),
            _resident((SGU_WIDTH, D_MODEL)),
            _resident((D_MODEL, D_MODEL)),
            _resident((1, D_MODEL)),
            _resident((D_MODEL, LANES)),
            _resident((1, LANES)),
        ],
        out_specs=[pl.BlockSpec((tm, D_MODEL), row), pl.BlockSpec((tm, LANES), row),
                   pl.BlockSpec((tm, LANES), row)],
        out_shape=[jax.ShapeDtypeStruct((n_tok, D_MODEL), F32),
                   jax.ShapeDtypeStruct((n_tok, LANES), jnp.int32),
                   jax.ShapeDtypeStruct((n_tok, LANES), F32)],
        compiler_params=pltpu.CompilerParams(dimension_semantics=("arbitrary",),
                                             vmem_limit_bytes=VMEM_LIMIT_BYTES),
        name="mixer_back",
    )(x2d, za, sb, g_mix, w_merge, b_merge, w_a, w_b, w_out, g_ffn, w_router, b_router)


def _row_gather_copy(src_hbm, src_row, dst_vmem, dst_row, sem):
    return pltpu.make_async_copy(src_hbm.at[pl.ds(src_row, 1), :], dst_vmem.at[pl.ds(dst_row, 1), :], sem)


def _expert_kernel(be_ref, nused_ref, rowtok_ref, x1_hbm, gffn_ref, wg_ref, wu_ref, wd_ref, roww_ref,
                   ys_ref, xbuf, sems, wg_bf, wu_bf, wd_bf):
    rows = MOE_BLOCK_ROWS
    b = pl.program_id(0)
    n_used = nused_ref[0]
    slot = lax.rem(b, 2)

    def start_gather(blk, slt):
        def body(r, carry):
            _row_gather_copy(x1_hbm, rowtok_ref[blk * rows + r], xbuf.at[slt], r, sems.at[slt]).start()
            return carry
        lax.fori_loop(0, rows, body, 0, unroll=GATHER_UNROLL)

    @pl.when(b == 0)
    def _():
        start_gather(0, 0)

    @pl.when(b + 1 < n_used)
    def _():
        start_gather(b + 1, 1 - slot)

    @pl.when(b < n_used)
    def _():
        pltpu.make_async_copy(x1_hbm.at[pl.ds(0, rows), :], xbuf.at[slot], sems.at[slot]).wait()

        @pl.when(jnp.logical_or(b == 0, be_ref[b] != be_ref[jnp.maximum(b - 1, 0)]))
        def _():
            wg_bf[...] = wg_ref[...].astype(BF16)
            wu_bf[...] = wu_ref[...].astype(BF16)
            wd_bf[...] = wd_ref[...].astype(BF16)

        h2 = (_rms_scale(xbuf[slot]) * gffn_ref[...]).astype(BF16)
        gate = jnp.dot(h2, wg_bf[...], preferred_element_type=F32)
        up = jnp.dot(h2, wu_bf[...], preferred_element_type=F32)
        hid = (jax.nn.silu(gate) * up).astype(BF16)
        ys_ref[...] = jnp.dot(hid, wd_bf[...], preferred_element_type=F32) * roww_ref[...]

    @pl.when(b >= n_used)
    def _():
        ys_ref[...] = jnp.zeros_like(ys_ref)


def _expert_ffn(block_e, n_used, row_tok, x1, g_ffn, w_g, w_u, w_d, row_w):
    n_rows = row_tok.shape[0]
    n_blocks = n_rows // MOE_BLOCK_ROWS
    by_expert = lambda b, be, nu, rt: (be[b], 0, 0)
    grid_spec = pltpu.PrefetchScalarGridSpec(
        num_scalar_prefetch=3,
        grid=(n_blocks,),
        in_specs=[
            pl.BlockSpec(memory_space=pl.ANY),
            pl.BlockSpec((1, D_MODEL), lambda b, *_: (0, 0)),
            pl.BlockSpec((None, D_MODEL, D_EXPERT), by_expert),
            pl.BlockSpec((None, D_MODEL, D_EXPERT), by_expert),
            pl.BlockSpec((None, D_EXPERT, D_MODEL), by_expert),
            pl.BlockSpec((MOE_BLOCK_ROWS, 1), lambda b, *_: (b, 0)),
        ],
        out_specs=pl.BlockSpec((MOE_BLOCK_ROWS, D_MODEL), lambda b, *_: (b, 0)),
        scratch_shapes=[
            pltpu.VMEM((2, MOE_BLOCK_ROWS, D_MODEL), F32),
            pltpu.SemaphoreType.DMA((2,)),
            pltpu.VMEM((D_MODEL, D_EXPERT), BF16),
            pltpu.VMEM((D_MODEL, D_EXPERT), BF16),
            pltpu.VMEM((D_EXPERT, D_MODEL), BF16),
        ],
    )
    return pl.pallas_call(
        _expert_kernel,
        grid_spec=grid_spec,
        out_shape=jax.ShapeDtypeStruct((n_rows, D_MODEL), F32),
        compiler_params=pltpu.CompilerParams(dimension_semantics=("arbitrary",),
                                             vmem_limit_bytes=VMEM_LIMIT_BYTES),
        name="expert_ffn",
    )(block_e, n_used, row_tok, x1, g_ffn, w_g, w_u, w_d, row_w)


def _tail_kernel(dest_ref, x1_ref, ys_hbm, p_ref, gple_ref, wpg_ref, bpg_ref, wpu_ref, gfin_ref,
                 out_ref, ybuf, sems):
    tm = x1_ref.shape[0]
    i = pl.program_id(0)
    n_steps = pl.num_programs(0)
    slot = lax.rem(i, 2)

    def start_gather(step, slt):
        def body(r, carry):
            tok = step * tm + r
            for k in range(TOP_K):
                _row_gather_copy(ys_hbm, dest_ref[TOP_K * tok + k], ybuf.at[slt], k * tm + r,
                                 sems.at[slt]).start()
            return carry
        lax.fori_loop(0, tm, body, 0, unroll=GATHER_UNROLL)

    @pl.when(i == 0)
    def _():
        start_gather(0, 0)

    @pl.when(i + 1 < n_steps)
    def _():
        start_gather(i + 1, 1 - slot)

    pltpu.make_async_copy(ys_hbm.at[pl.ds(0, TOP_K * tm), :], ybuf.at[slot], sems.at[slot]).wait()
    x2 = x1_ref[...] + ybuf[slot, :tm, :] + ybuf[slot, tm:, :]
    hn = (_rms_scale(x2) * gple_ref[...]).astype(BF16)
    gate = jax.nn.sigmoid(jnp.dot(hn, wpg_ref[...], preferred_element_type=F32) + bpg_ref[...])
    up = jnp.dot(p_ref[...].astype(BF16), wpu_ref[...], preferred_element_type=F32)
    x3 = x2 + gate * up
    out_ref[...] = _rms_scale(x3) * gfin_ref[...]


def _tail(dest, x1, ys, p2d, g_ple, w_pg, b_pg, w_pu, g_final):
    n_tok = x1.shape[0]
    tm = TM_TAIL
    row = lambda i, *_: (i, 0)
    const = lambda i, *_: (0, 0)
    grid_spec = pltpu.PrefetchScalarGridSpec(
        num_scalar_prefetch=1,
        grid=(n_tok // tm,),
        in_specs=[
            pl.BlockSpec((tm, D_MODEL), row),
            pl.BlockSpec(memory_space=pl.ANY),
            pl.BlockSpec((tm, PLE_DIM), row),
            pl.BlockSpec((1, D_MODEL), const),
            pl.BlockSpec((D_MODEL, D_MODEL), const, pipeline_mode=pl.Buffered(1)),
            pl.BlockSpec((1, D_MODEL), const),
            pl.BlockSpec((PLE_DIM, D_MODEL), const),
            pl.BlockSpec((1, D_MODEL), const),
        ],
        out_specs=pl.BlockSpec((tm, D_MODEL), row),
        scratch_shapes=[pltpu.VMEM((2, TOP_K * tm, D_MODEL), F32), pltpu.SemaphoreType.DMA((2,))],
    )
    return pl.pallas_call(
        _tail_kernel,
        grid_spec=grid_spec,
        out_shape=jax.ShapeDtypeStruct((n_tok, D_MODEL), F32),
        compiler_params=pltpu.CompilerParams(dimension_semantics=("arbitrary",),
                                             vmem_limit_bytes=VMEM_LIMIT_BYTES),
        name="tail",
    )(dest, x1, ys, p2d, g_ple, w_pg, b_pg, w_pu, g_final)


def _dispatch_plan(expert_id, weights):
    n_tok = expert_id.shape[0]
    n_assign = n_tok * TOP_K
    n_blocks = -(-n_assign // MOE_BLOCK_ROWS) + N_EXPERTS
    n_rows = n_blocks * MOE_BLOCK_ROWS
    i32 = jnp.int32
    flat_e = expert_id.reshape(-1)
    flat_w = weights.reshape(-1)
    experts = jnp.arange(N_EXPERTS, dtype=i32)
    assign = jnp.arange(n_assign, dtype=i32)
    se, order = lax.sort((flat_e, assign), num_keys=1)
    onehot_sorted = se[:, None] == experts[None, :]
    counts = jnp.sum(onehot_sorted.astype(i32), axis=0)
    padded = (counts + MOE_BLOCK_ROWS - 1) // MOE_BLOCK_ROWS * MOE_BLOCK_ROWS
    pad_end = jnp.cumsum(padded)
    pad_start = pad_end - padded
    start = jnp.cumsum(counts) - counts
    row_of_sorted = assign + jnp.sum(jnp.where(onehot_sorted, (pad_start - start)[None, :], 0), axis=1)
    _, dest = lax.sort((order, row_of_sorted), num_keys=1)
    n_used = pad_end[-1] // MOE_BLOCK_ROWS
    rows = jnp.arange(n_rows, dtype=i32)
    row_e = jnp.minimum(jnp.sum((pad_end[None, :] <= rows[:, None]).astype(i32), axis=1), N_EXPERTS - 1)
    onehot_row = row_e[:, None] == experts[None, :]
    pick = lambda table: jnp.sum(jnp.where(onehot_row, table[None, :], 0), axis=1)
    offset = rows - pick(pad_start)
    valid = offset < pick(counts)
    src = order[jnp.clip(pick(start) + offset, 0, n_assign - 1)]
    row_tok = jnp.where(valid, src // TOP_K, 0)
    row_w = jnp.where(valid, flat_w[src], 0.0)
    block_e = row_e.reshape(n_blocks, MOE_BLOCK_ROWS)[:, 0]
    block_e = jnp.where(jnp.arange(n_blocks) < n_used, block_e, block_e[jnp.maximum(n_used - 1, 0)])
    return (block_e.astype(i32), n_used.astype(i32).reshape(1), row_tok.astype(i32),
            row_w.reshape(n_rows, 1), dest.astype(i32))


def kernel(x, p, g_mix, w_in, w_pool, pool_scale, w_branch_a, sgu_ln_g, sgu_ln_b, w_spatial, b_spatial, w_branch_b, w_merge_gate, b_merge_gate, w_out, g_ffn, w_router_group, b_router_group, w_router_expert, b_router_expert, w_exp_gate, w_exp_up, w_exp_down, g_ple, w_ple_gate, b_ple_gate, w_ple_up, g_final):
    bsz, seq, d = x.shape
    assert (seq, d) == (SEQ, D_MODEL) and g_mix.shape[0] == 1
    n_tok = bsz * seq
    x2d = x.reshape(n_tok, d)
    row2d = lambda v: v.reshape(1, -1)

    za, sb = _mixer_front(
        x2d, row2d(g_mix[0]), w_in[0].astype(BF16), w_pool[0].astype(BF16), row2d(pool_scale[0]),
        row2d(sgu_ln_g[0]), row2d(sgu_ln_b[0]), w_spatial[0], b_spatial[0][:, :, None])

    w_router = jnp.concatenate(
        [w_router_group[0], jnp.transpose(w_router_expert[0], (1, 0, 2)).reshape(d, N_EXPERTS)], axis=1)
    b_router = jnp.concatenate([b_router_group[0], b_router_expert[0].reshape(N_EXPERTS)])
    pad = LANES - w_router.shape[1]
    w_router = jnp.pad(w_router, ((0, 0), (0, pad)))
    b_router = jnp.pad(b_router, (0, pad))

    x1, eid, wts = _mixer_back(
        x2d, za, sb, row2d(g_mix[0]), w_merge_gate[0].astype(BF16), row2d(b_merge_gate[0]),
        w_branch_a[0].astype(BF16), w_branch_b[0].astype(BF16), w_out[0].astype(BF16), row2d(g_ffn[0]),
        w_router, row2d(b_router))

    block_e, n_used, row_tok, row_w, dest = _dispatch_plan(eid[:, :TOP_K], wts[:, :TOP_K])
    ys = _expert_ffn(block_e, n_used, row_tok, x1, row2d(g_ffn[0]), w_exp_gate[0], w_exp_up[0],
                     w_exp_down[0], row_w)
    out = _tail(dest, x1, ys, p[0].reshape(n_tok, PLE_DIM), row2d(g_ple[0]), w_ple_gate[0].astype(BF16),
                row2d(b_ple_gate[0]), w_ple_up[0].astype(BF16), row2d(g_final))
    return out.reshape(bsz, seq, d)
```

```python
import functools

import jax
import jax.numpy as jnp
from jax import lax
from jax.experimental import pallas as pl
from jax.experimental.pallas import tpu as pltpu

F32 = jnp.float32
BF16 = jnp.bfloat16

D_MODEL = 2048
SEQ = 4096
EPS = 1e-6
PLE_DIM = 256
POOL_WINDOWS = (2, 4, 8, 16)
POOL_WIDTH = D_MODEL // 2
POOL_GROUP_DIM = POOL_WIDTH // len(POOL_WINDOWS)
POOL_HISTORY = max(POOL_WINDOWS)
SGU_BLOCK = 128
SGU_CHUNK = 64
SGU_GROUPS = 8
SGU_WIDTH = D_MODEL // 2
SGU_GROUP_DIM = SGU_WIDTH // SGU_GROUPS
N_IN = POOL_WIDTH + 2 * SGU_WIDTH
N_EXPERT_GROUPS = 4
EXPERTS_PER_GROUP = 8
N_EXPERTS = N_EXPERT_GROUPS * EXPERTS_PER_GROUP
TOP_K = 2
D_EXPERT = D_MODEL // 4
MOE_BLOCK_ROWS = 256

LANES = 128
MXU_COLS = 256
ROUTER_EXPERT_LANE0 = N_EXPERT_GROUPS
VMEM_LIMIT_BYTES = 56 * 1024 * 1024

TM_FRONT = 256
TM_BACK = 256
TM_TAIL = 256
MERGE_CHUNK = 512
GATHER_UNROLL = 8


def _rms_scale(x):
    return x * lax.rsqrt(jnp.mean(x * x, axis=-1, keepdims=True) + EPS)


def _resident(shape):
    zeros = (0,) * len(shape)
    return pl.BlockSpec(shape, lambda *_: zeros, pipeline_mode=pl.Buffered(1))


def _mixer_front_kernel(x_ref, gmix_ref, win_ref, wpool_ref, pscale_ref, lng_ref, lnb_ref, ws_ref,
                        bsp_ref, za_ref, sb_ref, hist_ref):
    tm = x_ref.shape[0]
    tiles_per_seq = SEQ // tm
    seq_tile = lax.rem(pl.program_id(0), tiles_per_seq)

    @pl.when(seq_tile == 0)
    def _():
        hist_ref[...] = jnp.zeros_like(hist_ref)

    h = (_rms_scale(x_ref[...]) * gmix_ref[...]).astype(BF16)
    z = jnp.dot(h, win_ref[...], preferred_element_type=F32)

    a = z[:, :POOL_WIDTH]
    ext = jnp.concatenate([hist_ref[...], a], axis=0)
    hist_ref[...] = a[tm - POOL_HISTORY:, :]
    frames = (seq_tile * tm + 1 + lax.broadcasted_iota(jnp.int32, (tm, 1), 0)).astype(F32)
    for gi, w in enumerate(POOL_WINDOWS):
        cols = slice(gi * POOL_GROUP_DIM, (gi + 1) * POOL_GROUP_DIM)
        s = ext[:, cols]
        k = 1
        while k < w:
            s = s + pltpu.roll(s, k, 0)
            k *= 2
        wsum = s[POOL_HISTORY:, :]
        zg = wsum / jnp.minimum(frames, float(w)) - a[:, cols]
        yg = jnp.dot(zg.astype(BF16), wpool_ref[gi], preferred_element_type=F32)
        za_ref[:, cols] = (yg * pscale_ref[:, cols]).astype(BF16)

    u = jax.nn.gelu(z[:, POOL_WIDTH:POOL_WIDTH + SGU_WIDTH])
    v = jax.nn.gelu(z[:, POOL_WIDTH + SGU_WIDTH:])
    vc = v - jnp.mean(v, axis=-1, keepdims=True)
    var = jnp.mean(vc * vc, axis=-1, keepdims=True)
    vn = (vc * lax.rsqrt(var + EPS) * lng_ref[...] + lnb_ref[...]).astype(BF16)
    t_chunk = lax.broadcasted_iota(jnp.int32, (SGU_BLOCK, SGU_BLOCK), 0) // SGU_CHUNK
    s_chunk = lax.broadcasted_iota(jnp.int32, (SGU_BLOCK, SGU_BLOCK), 1) // SGU_CHUNK
    causal = s_chunk <= t_chunk
    nblk = tm // SGU_BLOCK
    for g in range(SGU_GROUPS):
        cols = slice(g * SGU_GROUP_DIM, (g + 1) * SGU_GROUP_DIM)
        wsg = jnp.where(causal, ws_ref[g], 0.0).astype(BF16)
        vg = jnp.concatenate([vn[j * SGU_BLOCK:(j + 1) * SGU_BLOCK, cols] for j in range(nblk)], axis=1)
        vm = jnp.dot(wsg, vg, preferred_element_type=F32) + bsp_ref[g]
        for j in range(nblk):
            rows = slice(j * SGU_BLOCK, (j + 1) * SGU_BLOCK)
            sb_ref[rows, cols] = (u[rows, cols] * vm[:, j * SGU_GROUP_DIM:(j + 1) * SGU_GROUP_DIM]).astype(BF16)


def _mixer_front(x2d, g_mix, w_in, w_pool, pool_scale, ln_g, ln_b, w_spatial, b_spatial):
    n_tok = x2d.shape[0]
    tm = TM_FRONT
    row = lambda i: (i, 0)
    return pl.pallas_call(
        _mixer_front_kernel,
        grid=(n_tok // tm,),
        in_specs=[
            pl.BlockSpec((tm, D_MODEL), row),
            _resident((1, D_MODEL)),
            _resident((D_MODEL, N_IN)),
            _resident((len(POOL_WINDOWS), POOL_GROUP_DIM, POOL_GROUP_DIM)),
            _resident((1, POOL_WIDTH)),
            _resident((1, SGU_WIDTH)),
            _resident((1, SGU_WIDTH)),
            _resident((SGU_GROUPS, SGU_BLOCK, SGU_BLOCK)),
            _resident((SGU_GROUPS, SGU_BLOCK, 1)),
        ],
        out_specs=[pl.BlockSpec((tm, POOL_WIDTH), row), pl.BlockSpec((tm, SGU_WIDTH), row)],
        out_shape=[jax.ShapeDtypeStruct((n_tok, POOL_WIDTH), BF16),
                   jax.ShapeDtypeStruct((n_tok, SGU_WIDTH), BF16)],
        scratch_shapes=[pltpu.VMEM((POOL_HISTORY, POOL_WIDTH), F32)],
        compiler_params=pltpu.CompilerParams(dimension_semantics=("arbitrary",),
                                             vmem_limit_bytes=VMEM_LIMIT_BYTES),
        name="mixer_front",
    )(x2d, g_mix, w_in, w_pool, pool_scale, ln_g, ln_b, w_spatial, b_spatial)


def _route(logits):
    lane = lax.broadcasted_iota(jnp.int32, logits.shape, 1).astype(F32)
    neg = -jnp.inf
    far = float(LANES)

    def first_argmax(vals):
        top = jnp.max(vals, axis=-1, keepdims=True)
        return top, jnp.min(jnp.where(vals == top, lane, far), axis=-1, keepdims=True)

    is_grp = lane < float(N_EXPERT_GROUPS)
    g_top, g_idx = first_argmax(jnp.where(is_grp, logits, neg))
    g_den = jnp.sum(jnp.where(is_grp, jnp.exp(logits - g_top), 0.0), axis=-1, keepdims=True)
    grp_p = 1.0 / g_den
    lo = float(ROUTER_EXPERT_LANE0) + g_idx * float(EXPERTS_PER_GROUP)
    e_log = jnp.where(lane >= lo, jnp.where(lane < lo + float(EXPERTS_PER_GROUP), logits, neg), neg)
    t1, i1 = first_argmax(e_log)
    t2, i2 = first_argmax(jnp.where(lane == i1, neg, e_log))
    r = jnp.exp(t2 - t1)
    w1 = grp_p / (1.0 + r)
    w2 = grp_p * r / (1.0 + r)
    e1 = i1 - float(ROUTER_EXPERT_LANE0)
    e2 = i2 - float(ROUTER_EXPERT_LANE0)
    eid = jnp.where(lane == 0.0, e1, jnp.where(lane == 1.0, e2, 0.0)).astype(jnp.int32)
    wts = jnp.where(lane == 0.0, w1, jnp.where(lane == 1.0, w2, 0.0))
    return eid, wts


def _mixer_back_kernel(x_ref, za_ref, sb_ref, gmix_ref, wm_ref, bm_ref, wa_ref, wb_ref, wo_ref,
                       gffn_ref, wr_ref, br_ref, x1_ref, eid_ref, wts_ref):
    x = x_ref[...]
    h = (_rms_scale(x) * gmix_ref[...]).astype(BF16)
    za = za_ref[...]
    sb = sb_ref[...]
    acc = jnp.zeros(x.shape, F32)
    for c in range(D_MODEL // MERGE_CHUNK):
        ca = slice(c * MERGE_CHUNK, (c + 1) * MERGE_CHUNK)
        cb = slice(D_MODEL + c * MERGE_CHUNK, D_MODEL + (c + 1) * MERGE_CHUNK)
        ga = jax.nn.sigmoid(jnp.dot(h, wm_ref[:, ca], preferred_element_type=F32) + bm_ref[:, ca])
        gb = jax.nn.sigmoid(jnp.dot(h, wm_ref[:, cb], preferred_element_type=F32) + bm_ref[:, cb])
        ya = jnp.dot(za, wa_ref[:, ca], preferred_element_type=F32)
        yb = jnp.dot(sb, wb_ref[:, ca], preferred_element_type=F32)
        merged = (ga * ya + gb * yb).astype(BF16)
        acc = acc + jnp.dot(merged, wo_ref[ca, :], preferred_element_type=F32)
    x1 = x + acc
    x1_ref[...] = x1
    h2 = _rms_scale(x1) * gffn_ref[...]
    h2_hi = h2.astype(BF16)
    h2_lo = (h2 - h2_hi.astype(F32)).astype(BF16)
    hi_terms = jnp.dot(h2_hi, wr_ref[...], preferred_element_type=F32)
    lo_term = jnp.dot(h2_lo, wr_ref[:, :LANES], preferred_element_type=F32)
    logits = hi_terms[:, :LANES] + hi_terms[:, LANES:] + lo_term + br_ref[...]
    eid, wts = _route(logits)
    eid_ref[...] = eid
    wts_ref[...] = wts


def _mixer_back(x2d, za, sb, g_mix, w_merge, b_merge, w_a, w_b, w_out, g_ffn, w_router, b_router):
    n_tok = x2d.shape[0]
    tm = TM_BACK
    row = lambda i: (i, 0)
    return pl.pallas_call(
        _mixer_back_kernel,
        grid=(n_tok // tm,),
        in_specs=[
            pl.BlockSpec((tm, D_MODEL), row),
            pl.BlockSpec((tm, POOL_WIDTH), row),
            pl.BlockSpec((tm, SGU_WIDTH), row),
            _resident((1, D_MODEL)),
            _resident((D_MODEL, 2 * D_MODEL)),
            _resident((1, 2 * D_MODEL)),
            _resident((POOL_WIDTH, D_MODEL)),
            _resident((SGU_WIDTH, D_MODEL)),
            _resident((D_MODEL, D_MODEL)),
            _resident((1, D_MODEL)),
            _resident((D_MODEL, 2 * LANES)),
            _resident((1, LANES)),
        ],
        out_specs=[pl.BlockSpec((tm, D_MODEL), row), pl.BlockSpec((tm, LANES), row),
                   pl.BlockSpec((tm, LANES), row)],
        out_shape=[jax.ShapeDtypeStruct((n_tok, D_MODEL), F32),
                   jax.ShapeDtypeStruct((n_tok, LANES), jnp.int32),
                   jax.ShapeDtypeStruct((n_tok, LANES), F32)],
        compiler_params=pltpu.CompilerParams(dimension_semantics=("arbitrary",),
                                             vmem_limit_bytes=VMEM_LIMIT_BYTES),
        name="mixer_back",
    )(x2d, za, sb, g_mix, w_merge, b_merge, w_a, w_b, w_out, g_ffn, w_router, b_router)


def _row_gather_copy(src_hbm, src_row, dst_vmem, dst_row, sem):
    return pltpu.make_async_copy(src_hbm.at[pl.ds(src_row, 1), :], dst_vmem.at[pl.ds(dst_row, 1), :], sem)


def _expert_kernel(be_ref, nused_ref, rowtok_ref, x1_hbm, gffn_ref, wg_ref, wu_ref, wd_ref, roww_ref,
                   ys_ref, xbuf, sems, wg_bf, wu_bf, wd_bf):
    rows = MOE_BLOCK_ROWS
    b = pl.program_id(0)
    n_used = nused_ref[0]
    slot = lax.rem(b, 2)

    def row_copy(blk, slt, r):
        return _row_gather_copy(x1_hbm, rowtok_ref[blk * rows + r], xbuf.at[slt], r, sems.at[slt])

    def wait_block(slt):
        pltpu.make_async_copy(x1_hbm.at[pl.ds(0, rows), :], xbuf.at[slt], sems.at[slt]).wait()

    @pl.when(b == 0)
    def _():
        def body(r, carry):
            row_copy(0, 0, r).start()
            return carry
        lax.fori_loop(0, rows, body, 0, unroll=GATHER_UNROLL)

    @pl.when(b < n_used)
    def _():
        wait_block(slot)

        @pl.when(jnp.logical_or(b == 0, be_ref[b] != be_ref[jnp.maximum(b - 1, 0)]))
        def _():
            wg_bf[...] = wg_ref[...].astype(BF16)
            wu_bf[...] = wu_ref[...].astype(BF16)
            wd_bf[...] = wd_ref[...].astype(BF16)

        nxt = jnp.minimum(b + 1, n_used - 1)
        other = 1 - slot
        h2 = (_rms_scale(xbuf[slot]) * gffn_ref[...]).astype(BF16)
        n_slabs = D_EXPERT // MXU_COLS
        per = rows // (2 * n_slabs)
        gate, up = [], []
        for n in range(n_slabs):
            cols = slice(n * MXU_COLS, (n + 1) * MXU_COLS)
            for r in range(2 * n * per, (2 * n + 1) * per):
                row_copy(nxt, other, r).start()
            gate.append(jnp.dot(h2, wg_bf[:, cols], preferred_element_type=F32))
            for r in range((2 * n + 1) * per, (2 * n + 2) * per):
                row_copy(nxt, other, r).start()
            up.append(jnp.dot(h2, wu_bf[:, cols], preferred_element_type=F32))
        gate = jnp.concatenate(gate, axis=1)
        up = jnp.concatenate(up, axis=1)
        hid = (jax.nn.silu(gate) * up).astype(BF16)
        ys_ref[...] = jnp.dot(hid, wd_bf[...], preferred_element_type=F32) * roww_ref[...]

        @pl.when(b == n_used - 1)
        def _():
            wait_block(other)

    @pl.when(b >= n_used)
    def _():
        ys_ref[...] = jnp.zeros_like(ys_ref)


def _expert_ffn(block_e, n_used, row_tok, x1, g_ffn, w_g, w_u, w_d, row_w):
    n_rows = row_tok.shape[0]
    n_blocks = n_rows // MOE_BLOCK_ROWS
    by_expert = lambda b, be, nu, rt: (be[b], 0, 0)
    grid_spec = pltpu.PrefetchScalarGridSpec(
        num_scalar_prefetch=3,
        grid=(n_blocks,),
        in_specs=[
            pl.BlockSpec(memory_space=pl.ANY),
            pl.BlockSpec((1, D_MODEL), lambda b, *_: (0, 0)),
            pl.BlockSpec((None, D_MODEL, D_EXPERT), by_expert),
            pl.BlockSpec((None, D_MODEL, D_EXPERT), by_expert),
            pl.BlockSpec((None, D_EXPERT, D_MODEL), by_expert),
            pl.BlockSpec((MOE_BLOCK_ROWS, 1), lambda b, *_: (b, 0)),
        ],
        out_specs=pl.BlockSpec((MOE_BLOCK_ROWS, D_MODEL), lambda b, *_: (b, 0)),
        scratch_shapes=[
            pltpu.VMEM((2, MOE_BLOCK_ROWS, D_MODEL), F32),
            pltpu.SemaphoreType.DMA((2,)),
            pltpu.VMEM((D_MODEL, D_EXPERT), BF16),
            pltpu.VMEM((D_MODEL, D_EXPERT), BF16),
            pltpu.VMEM((D_EXPERT, D_MODEL), BF16),
        ],
    )
    return pl.pallas_call(
        _expert_kernel,
        grid_spec=grid_spec,
        out_shape=jax.ShapeDtypeStruct((n_rows, D_MODEL), F32),
        compiler_params=pltpu.CompilerParams(dimension_semantics=("arbitrary",),
                                             vmem_limit_bytes=VMEM_LIMIT_BYTES),
        name="expert_ffn",
    )(block_e, n_used, row_tok, x1, g_ffn, w_g, w_u, w_d, row_w)


def _tail_kernel(dest_ref, x1_ref, ys_hbm, p_ref, gple_ref, wpg_ref, bpg_ref, wpu_ref, gfin_ref,
                 out_ref, ybuf, sems):
    tm = x1_ref.shape[0]
    i = pl.program_id(0)
    last = pl.num_programs(0) - 1
    slot = lax.rem(i, 2)

    def start_token(step, slt, r):
        for k in range(TOP_K):
            _row_gather_copy(ys_hbm, dest_ref[TOP_K * (step * tm + r) + k], ybuf.at[slt], k * tm + r,
                             sems.at[slt]).start()

    def wait_tile(slt):
        pltpu.make_async_copy(ys_hbm.at[pl.ds(0, TOP_K * tm), :], ybuf.at[slt], sems.at[slt]).wait()

    @pl.when(i == 0)
    def _():
        def body(r, carry):
            start_token(0, 0, r)
            return carry
        lax.fori_loop(0, tm, body, 0, unroll=GATHER_UNROLL)

    wait_tile(slot)
    nxt = jnp.minimum(i + 1, last)
    other = 1 - slot
    x2 = x1_ref[...] + ybuf[slot, :tm, :] + ybuf[slot, tm:, :]
    hn = (_rms_scale(x2) * gple_ref[...]).astype(BF16)
    up = jnp.dot(p_ref[...].astype(BF16), wpu_ref[...], preferred_element_type=F32)
    n_slabs = D_MODEL // MXU_COLS
    per = tm // n_slabs
    x3 = []
    for n in range(n_slabs):
        cols = slice(n * MXU_COLS, (n + 1) * MXU_COLS)
        for r in range(n * per, (n + 1) * per):
            start_token(nxt, other, r)
        gate = jax.nn.sigmoid(jnp.dot(hn, wpg_ref[:, cols], preferred_element_type=F32) + bpg_ref[:, cols])
        x3.append(x2[:, cols] + gate * up[:, cols])
    x3 = jnp.concatenate(x3, axis=1)
    out_ref[...] = _rms_scale(x3) * gfin_ref[...]

    @pl.when(i == last)
    def _():
        wait_tile(other)


def _tail(dest, x1, ys, p2d, g_ple, w_pg, b_pg, w_pu, g_final):
    n_tok = x1.shape[0]
    tm = TM_TAIL
    row = lambda i, *_: (i, 0)
    const = lambda i, *_: (0, 0)
    grid_spec = pltpu.PrefetchScalarGridSpec(
        num_scalar_prefetch=1,
        grid=(n_tok // tm,),
        in_specs=[
            pl.BlockSpec((tm, D_MODEL), row),
            pl.BlockSpec(memory_space=pl.ANY),
            pl.BlockSpec((tm, PLE_DIM), row),
            pl.BlockSpec((1, D_MODEL), const),
            pl.BlockSpec((D_MODEL, D_MODEL), const, pipeline_mode=pl.Buffered(1)),
            pl.BlockSpec((1, D_MODEL), const),
            pl.BlockSpec((PLE_DIM, D_MODEL), const),
            pl.BlockSpec((1, D_MODEL), const),
        ],
        out_specs=pl.BlockSpec((tm, D_MODEL), row),
        scratch_shapes=[pltpu.VMEM((2, TOP_K * tm, D_MODEL), F32), pltpu.SemaphoreType.DMA((2,))],
    )
    return pl.pallas_call(
        _tail_kernel,
        grid_spec=grid_spec,
        out_shape=jax.ShapeDtypeStruct((n_tok, D_MODEL), F32),
        compiler_params=pltpu.CompilerParams(dimension_semantics=("arbitrary",),
                                             vmem_limit_bytes=VMEM_LIMIT_BYTES),
        name="tail",
    )(dest, x1, ys, p2d, g_ple, w_pg, b_pg, w_pu, g_final)


def _dispatch_plan(expert_id, weights):
    n_tok = expert_id.shape[0]
    n_assign = n_tok * TOP_K
    n_blocks = -(-n_assign // MOE_BLOCK_ROWS) + N_EXPERTS
    n_rows = n_blocks * MOE_BLOCK_ROWS
    i32 = jnp.int32
    flat_e = expert_id.reshape(-1)
    flat_w = weights.reshape(-1)
    experts = jnp.arange(N_EXPERTS, dtype=i32)
    assign = jnp.arange(n_assign, dtype=i32)
    se, order = lax.sort((flat_e, assign), num_keys=1)
    onehot_sorted = se[:, None] == experts[None, :]
    counts = jnp.sum(onehot_sorted.astype(i32), axis=0)
    padded = (counts + MOE_BLOCK_ROWS - 1) // MOE_BLOCK_ROWS * MOE_BLOCK_ROWS
    pad_end = jnp.cumsum(padded)
    pad_start = pad_end - padded
    start = jnp.cumsum(counts) - counts
    row_of_sorted = assign + jnp.sum(jnp.where(onehot_sorted, (pad_start - start)[None, :], 0), axis=1)
    _, dest = lax.sort((order, row_of_sorted), num_keys=1)
    n_used = pad_end[-1] // MOE_BLOCK_ROWS
    rows = jnp.arange(n_rows, dtype=i32)
    row_e = jnp.minimum(jnp.sum((pad_end[None, :] <= rows[:, None]).astype(i32), axis=1), N_EXPERTS - 1)
    onehot_row = row_e[:, None] == experts[None, :]
    pick = lambda table: jnp.sum(jnp.where(onehot_row, table[None, :], 0), axis=1)
    offset = rows - pick(pad_start)
    valid = offset < pick(counts)
    src = order[jnp.clip(pick(start) + offset, 0, n_assign - 1)]
    row_tok = jnp.where(valid, src // TOP_K, 0)
    row_w = jnp.where(valid, flat_w[src], 0.0)
    block_e = row_e.reshape(n_blocks, MOE_BLOCK_ROWS)[:, 0]
    block_e = jnp.where(jnp.arange(n_blocks) < n_used, block_e, block_e[jnp.maximum(n_used - 1, 0)])
    return (block_e.astype(i32), n_used.astype(i32).reshape(1), row_tok.astype(i32),
            row_w.reshape(n_rows, 1), dest.astype(i32))


def kernel(x, p, g_mix, w_in, w_pool, pool_scale, w_branch_a, sgu_ln_g, sgu_ln_b, w_spatial, b_spatial, w_branch_b, w_merge_gate, b_merge_gate, w_out, g_ffn, w_router_group, b_router_group, w_router_expert, b_router_expert, w_exp_gate, w_exp_up, w_exp_down, g_ple, w_ple_gate, b_ple_gate, w_ple_up, g_final):
    bsz, seq, d = x.shape
    assert (seq, d) == (SEQ, D_MODEL) and g_mix.shape[0] == 1
    n_tok = bsz * seq
    x2d = x.reshape(n_tok, d)
    row2d = lambda v: v.reshape(1, -1)

    za, sb = _mixer_front(
        x2d, row2d(g_mix[0]), w_in[0].astype(BF16), w_pool[0].astype(BF16), row2d(pool_scale[0]),
        row2d(sgu_ln_g[0]), row2d(sgu_ln_b[0]), w_spatial[0], b_spatial[0][:, :, None])

    w_router = jnp.concatenate(
        [w_router_group[0], jnp.transpose(w_router_expert[0], (1, 0, 2)).reshape(d, N_EXPERTS)], axis=1)
    b_router = jnp.concatenate([b_router_group[0], b_router_expert[0].reshape(N_EXPERTS)])
    pad = LANES - w_router.shape[1]
    w_router = jnp.pad(w_router, ((0, 0), (0, pad)))
    b_router = jnp.pad(b_router, (0, pad))
    w_router_hi = w_router.astype(BF16)
    w_router_lo = (w_router - w_router_hi.astype(F32)).astype(BF16)
    w_router_split = jnp.concatenate([w_router_hi, w_router_lo], axis=1)

    x1, eid, wts = _mixer_back(
        x2d, za, sb, row2d(g_mix[0]), w_merge_gate[0].astype(BF16), row2d(b_merge_gate[0]),
        w_branch_a[0].astype(BF16), w_branch_b[0].astype(BF16), w_out[0].astype(BF16), row2d(g_ffn[0]),
        w_router_split, row2d(b_router))

    block_e, n_used, row_tok, row_w, dest = _dispatch_plan(eid[:, :TOP_K], wts[:, :TOP_K])
    ys = _expert_ffn(block_e, n_used, row_tok, x1, row2d(g_ffn[0]), w_exp_gate[0], w_exp_up[0],
                     w_exp_down[0], row_w)
    out = _tail(dest, x1, ys, p[0].reshape(n_tok, PLE_DIM), row2d(g_ple[0]), w_ple_gate[0].astype(BF16),
                row2d(b_ple_gate[0]), w_ple_up[0].astype(BF16), row2d(g_final))
    return out.reshape(bsz, seq, d)
```

```python
import functools

import jax
import jax.numpy as jnp
from jax import lax
from jax.experimental import pallas as pl
from jax.experimental.pallas import tpu as pltpu

F32 = jnp.float32
BF16 = jnp.bfloat16

D_MODEL = 2048
SEQ = 4096
EPS = 1e-6
PLE_DIM = 256
POOL_WINDOWS = (2, 4, 8, 16)
POOL_WIDTH = D_MODEL // 2
POOL_GROUP_DIM = POOL_WIDTH // len(POOL_WINDOWS)
POOL_HISTORY = max(POOL_WINDOWS)
SGU_BLOCK = 128
SGU_CHUNK = 64
SGU_GROUPS = 8
SGU_WIDTH = D_MODEL // 2
SGU_GROUP_DIM = SGU_WIDTH // SGU_GROUPS
N_IN = POOL_WIDTH + 2 * SGU_WIDTH
N_EXPERT_GROUPS = 4
EXPERTS_PER_GROUP = 8
N_EXPERTS = N_EXPERT_GROUPS * EXPERTS_PER_GROUP
TOP_K = 2
D_EXPERT = D_MODEL // 4
MOE_BLOCK_ROWS = 256

LANES = 128
MXU_COLS = 256
ROUTER_EXPERT_LANE0 = N_EXPERT_GROUPS
VMEM_LIMIT_BYTES = 56 * 1024 * 1024

TM_FRONT = 256
TM_BACK = 256
TM_TAIL = 256
MERGE_CHUNK = 512
GATHER_UNROLL = 8


def _rms_scale(x):
    return x * lax.rsqrt(jnp.mean(x * x, axis=-1, keepdims=True) + EPS)


def _resident(shape):
    zeros = (0,) * len(shape)
    return pl.BlockSpec(shape, lambda *_: zeros, pipeline_mode=pl.Buffered(1))


def _mixer_front_kernel(x_ref, gmix_ref, win_ref, wpool_ref, pscale_ref, lng_ref, lnb_ref, ws_ref,
                        bsp_ref, za_ref, sb_ref, hist_ref):
    tm = x_ref.shape[0]
    tiles_per_seq = SEQ // tm
    seq_tile = lax.rem(pl.program_id(0), tiles_per_seq)

    @pl.when(seq_tile == 0)
    def _():
        hist_ref[...] = jnp.zeros_like(hist_ref)

    h = (_rms_scale(x_ref[...]) * gmix_ref[...]).astype(BF16)
    z = jnp.dot(h, win_ref[...], preferred_element_type=F32)

    a = z[:, :POOL_WIDTH]
    ext = jnp.concatenate([hist_ref[...], a], axis=0)
    hist_ref[...] = a[tm - POOL_HISTORY:, :]
    frames = (seq_tile * tm + 1 + lax.broadcasted_iota(jnp.int32, (tm, 1), 0)).astype(F32)
    for gi, w in enumerate(POOL_WINDOWS):
        cols = slice(gi * POOL_GROUP_DIM, (gi + 1) * POOL_GROUP_DIM)
        s = ext[:, cols]
        k = 1
        while k < w:
            s = s + pltpu.roll(s, k, 0)
            k *= 2
        wsum = s[POOL_HISTORY:, :]
        zg = wsum / jnp.minimum(frames, float(w)) - a[:, cols]
        yg = jnp.dot(zg.astype(BF16), wpool_ref[gi], preferred_element_type=F32)
        za_ref[:, cols] = (yg * pscale_ref[:, cols]).astype(BF16)

    u = jax.nn.gelu(z[:, POOL_WIDTH:POOL_WIDTH + SGU_WIDTH])
    v = jax.nn.gelu(z[:, POOL_WIDTH + SGU_WIDTH:])
    vc = v - jnp.mean(v, axis=-1, keepdims=True)
    var = jnp.mean(vc * vc, axis=-1, keepdims=True)
    vn = (vc * lax.rsqrt(var + EPS) * lng_ref[...] + lnb_ref[...]).astype(BF16)
    t_chunk = lax.broadcasted_iota(jnp.int32, (SGU_BLOCK, SGU_BLOCK), 0) // SGU_CHUNK
    s_chunk = lax.broadcasted_iota(jnp.int32, (SGU_BLOCK, SGU_BLOCK), 1) // SGU_CHUNK
    causal = s_chunk <= t_chunk
    nblk = tm // SGU_BLOCK
    for g in range(SGU_GROUPS):
        cols = slice(g * SGU_GROUP_DIM, (g + 1) * SGU_GROUP_DIM)
        wsg = jnp.where(causal, ws_ref[g], 0.0).astype(BF16)
        vg = jnp.concatenate([vn[j * SGU_BLOCK:(j + 1) * SGU_BLOCK, cols] for j in range(nblk)], axis=1)
        vm = jnp.dot(wsg, vg, preferred_element_type=F32) + bsp_ref[g]
        for j in range(nblk):
            rows = slice(j * SGU_BLOCK, (j + 1) * SGU_BLOCK)
            sb_ref[rows, cols] = (u[rows, cols] * vm[:, j * SGU_GROUP_DIM:(j + 1) * SGU_GROUP_DIM]).astype(BF16)


def _mixer_front(x2d, g_mix, w_in, w_pool, pool_scale, ln_g, ln_b, w_spatial, b_spatial):
    n_tok = x2d.shape[0]
    tm = TM_FRONT
    row = lambda i: (i, 0)
    return pl.pallas_call(
        _mixer_front_kernel,
        grid=(n_tok // tm,),
        in_specs=[
            pl.BlockSpec((tm, D_MODEL), row),
            _resident((1, D_MODEL)),
            _resident((D_MODEL, N_IN)),
            _resident((len(POOL_WINDOWS), POOL_GROUP_DIM, POOL_GROUP_DIM)),
            _resident((1, POOL_WIDTH)),
            _resident((1, SGU_WIDTH)),
            _resident((1, SGU_WIDTH)),
            _resident((SGU_GROUPS, SGU_BLOCK, SGU_BLOCK)),
            _resident((SGU_GROUPS, SGU_BLOCK, 1)),
        ],
        out_specs=[pl.BlockSpec((tm, POOL_WIDTH), row), pl.BlockSpec((tm, SGU_WIDTH), row)],
        out_shape=[jax.ShapeDtypeStruct((n_tok, POOL_WIDTH), BF16),
                   jax.ShapeDtypeStruct((n_tok, SGU_WIDTH), BF16)],
        scratch_shapes=[pltpu.VMEM((POOL_HISTORY, POOL_WIDTH), F32)],
        compiler_params=pltpu.CompilerParams(dimension_semantics=("arbitrary",),
                                             vmem_limit_bytes=VMEM_LIMIT_BYTES),
        name="mixer_front",
    )(x2d, g_mix, w_in, w_pool, pool_scale, ln_g, ln_b, w_spatial, b_spatial)


def _route(logits):
    lane = lax.broadcasted_iota(jnp.int32, logits.shape, 1).astype(F32)
    neg = -jnp.inf
    far = float(LANES)

    def first_argmax(vals):
        top = jnp.max(vals, axis=-1, keepdims=True)
        return top, jnp.min(jnp.where(vals == top, lane, far), axis=-1, keepdims=True)

    is_grp = lane < float(N_EXPERT_GROUPS)
    g_top, g_idx = first_argmax(jnp.where(is_grp, logits, neg))
    g_den = jnp.sum(jnp.where(is_grp, jnp.exp(logits - g_top), 0.0), axis=-1, keepdims=True)
    grp_p = 1.0 / g_den
    lo = float(ROUTER_EXPERT_LANE0) + g_idx * float(EXPERTS_PER_GROUP)
    e_log = jnp.where(lane >= lo, jnp.where(lane < lo + float(EXPERTS_PER_GROUP), logits, neg), neg)
    t1, i1 = first_argmax(e_log)
    t2, i2 = first_argmax(jnp.where(lane == i1, neg, e_log))
    r = jnp.exp(t2 - t1)
    w1 = grp_p / (1.0 + r)
    w2 = grp_p * r / (1.0 + r)
    e1 = i1 - float(ROUTER_EXPERT_LANE0)
    e2 = i2 - float(ROUTER_EXPERT_LANE0)
    eid = jnp.where(lane == 0.0, e1, jnp.where(lane == 1.0, e2, 0.0)).astype(jnp.int32)
    wts = jnp.where(lane == 0.0, w1, jnp.where(lane == 1.0, w2, 0.0))
    return eid, wts


def _mixer_back_kernel(x_ref, za_ref, sb_ref, gmix_ref, wm_ref, bm_ref, wa_ref, wb_ref, wo_ref,
                       gffn_ref, wr_ref, br_ref, x1_ref, eid_ref, wts_ref):
    x = x_ref[...]
    h = (_rms_scale(x) * gmix_ref[...]).astype(BF16)
    za = za_ref[...]
    sb = sb_ref[...]
    acc = jnp.zeros(x.shape, F32)
    for c in range(D_MODEL // MERGE_CHUNK):
        ca = slice(c * MERGE_CHUNK, (c + 1) * MERGE_CHUNK)
        cb = slice(D_MODEL + c * MERGE_CHUNK, D_MODEL + (c + 1) * MERGE_CHUNK)
        ga = jax.nn.sigmoid(jnp.dot(h, wm_ref[:, ca], preferred_element_type=F32) + bm_ref[:, ca])
        gb = jax.nn.sigmoid(jnp.dot(h, wm_ref[:, cb], preferred_element_type=F32) + bm_ref[:, cb])
        ya = jnp.dot(za, wa_ref[:, ca], preferred_element_type=F32)
        yb = jnp.dot(sb, wb_ref[:, ca], preferred_element_type=F32)
        merged = (ga * ya + gb * yb).astype(BF16)
        acc = acc + jnp.dot(merged, wo_ref[ca, :], preferred_element_type=F32)
    x1 = x + acc
    x1_ref[...] = x1
    h2 = _rms_scale(x1) * gffn_ref[...]
    h2_hi = h2.astype(BF16)
    h2_lo = (h2 - h2_hi.astype(F32)).astype(BF16)
    hi_terms = jnp.dot(h2_hi, wr_ref[...], preferred_element_type=F32)
    lo_term = jnp.dot(h2_lo, wr_ref[:, :LANES], preferred_element_type=F32)
    logits = hi_terms[:, :LANES] + hi_terms[:, LANES:] + lo_term + br_ref[...]
    eid, wts = _route(logits)
    eid_ref[...] = eid
    wts_ref[...] = wts


def _mixer_back(x2d, za, sb, g_mix, w_merge, b_merge, w_a, w_b, w_out, g_ffn, w_router, b_router):
    n_tok = x2d.shape[0]
    tm = TM_BACK
    row = lambda i: (i, 0)
    return pl.pallas_call(
        _mixer_back_kernel,
        grid=(n_tok // tm,),
        in_specs=[
            pl.BlockSpec((tm, D_MODEL), row),
            pl.BlockSpec((tm, POOL_WIDTH), row),
            pl.BlockSpec((tm, SGU_WIDTH), row),
            _resident((1, D_MODEL)),
            _resident((D_MODEL, 2 * D_MODEL)),
            _resident((1, 2 * D_MODEL)),
            _resident((POOL_WIDTH, D_MODEL)),
            _resident((SGU_WIDTH, D_MODEL)),
            _resident((D_MODEL, D_MODEL)),
            _resident((1, D_MODEL)),
            _resident((D_MODEL, 2 * LANES)),
            _resident((1, LANES)),
        ],
        out_specs=[pl.BlockSpec((tm, D_MODEL), row), pl.BlockSpec((tm, LANES), row),
                   pl.BlockSpec((tm, LANES), row)],
        out_shape=[jax.ShapeDtypeStruct((n_tok, D_MODEL), F32),
                   jax.ShapeDtypeStruct((n_tok, LANES), jnp.int32),
                   jax.ShapeDtypeStruct((n_tok, LANES), F32)],
        compiler_params=pltpu.CompilerParams(dimension_semantics=("arbitrary",),
                                             vmem_limit_bytes=VMEM_LIMIT_BYTES),
        name="mixer_back",
    )(x2d, za, sb, g_mix, w_merge, b_merge, w_a, w_b, w_out, g_ffn, w_router, b_router)


def _row_gather_copy(src_hbm, src_row, dst_vmem, dst_row, sem):
    return pltpu.make_async_copy(src_hbm.at[pl.ds(src_row, 1), :], dst_vmem.at[pl.ds(dst_row, 1), :], sem)


def _expert_kernel(be_ref, nexte_ref, nused_ref, rowtok_ref, x1_hbm, gffn_ref, wg_hbm, wu_hbm, wd_hbm,
                   roww_ref, ys_ref, xbuf, sems, wg_st, wu_st, wd_st, wsems, wg_bf, wu_bf, wd_bf):
    rows = MOE_BLOCK_ROWS
    b = pl.program_id(0)
    n_used = nused_ref[0]
    slot = lax.rem(b, 2)

    def row_copy(blk, slt, r):
        return _row_gather_copy(x1_hbm, rowtok_ref[blk * rows + r], xbuf.at[slt], r, sems.at[slt])

    def wait_block(slt):
        pltpu.make_async_copy(x1_hbm.at[pl.ds(0, rows), :], xbuf.at[slt], sems.at[slt]).wait()

    @pl.when(b == 0)
    def _():
        def body(r, carry):
            row_copy(0, 0, r).start()
            return carry
        lax.fori_loop(0, rows, body, 0, unroll=GATHER_UNROLL)

    def weight_copies(e):
        return [pltpu.make_async_copy(src.at[e], dst, wsems.at[j])
                for j, (src, dst) in enumerate(((wg_hbm, wg_st), (wu_hbm, wu_st), (wd_hbm, wd_st)))]

    @pl.when(b < n_used)
    def _():
        @pl.when(b == 0)
        def _():
            for cp in weight_copies(be_ref[0]):
                cp.start()

        @pl.when(jnp.logical_or(b == 0, be_ref[b] != be_ref[jnp.maximum(b - 1, 0)]))
        def _():
            for cp in weight_copies(be_ref[b]):
                cp.wait()
            wg_bf[...] = wg_st[...].astype(BF16)
            wu_bf[...] = wu_st[...].astype(BF16)
            wd_bf[...] = wd_st[...].astype(BF16)
            nxt_e = nexte_ref[b]

            @pl.when(nxt_e >= 0)
            def _():
                for cp in weight_copies(nxt_e):
                    cp.start()

        wait_block(slot)

        nxt = jnp.minimum(b + 1, n_used - 1)
        other = 1 - slot
        h2 = (_rms_scale(xbuf[slot]) * gffn_ref[...]).astype(BF16)
        n_slabs = D_EXPERT // MXU_COLS
        per = rows // (2 * n_slabs)
        gate, up = [], []
        for n in range(n_slabs):
            cols = slice(n * MXU_COLS, (n + 1) * MXU_COLS)
            for r in range(2 * n * per, (2 * n + 1) * per):
                row_copy(nxt, other, r).start()
            gate.append(jnp.dot(h2, wg_bf[:, cols], preferred_element_type=F32))
            for r in range((2 * n + 1) * per, (2 * n + 2) * per):
                row_copy(nxt, other, r).start()
            up.append(jnp.dot(h2, wu_bf[:, cols], preferred_element_type=F32))
        gate = jnp.concatenate(gate, axis=1)
        up = jnp.concatenate(up, axis=1)
        hid = (jax.nn.silu(gate) * up).astype(BF16)
        ys_ref[...] = jnp.dot(hid, wd_bf[...], preferred_element_type=F32) * roww_ref[...]

        @pl.when(b == n_used - 1)
        def _():
            wait_block(other)

    @pl.when(b >= n_used)
    def _():
        ys_ref[...] = jnp.zeros_like(ys_ref)


def _expert_ffn(block_e, next_e, n_used, row_tok, x1, g_ffn, w_g, w_u, w_d, row_w):
    n_rows = row_tok.shape[0]
    n_blocks = n_rows // MOE_BLOCK_ROWS
    grid_spec = pltpu.PrefetchScalarGridSpec(
        num_scalar_prefetch=4,
        grid=(n_blocks,),
        in_specs=[
            pl.BlockSpec(memory_space=pl.ANY),
            pl.BlockSpec((1, D_MODEL), lambda b, *_: (0, 0)),
            pl.BlockSpec(memory_space=pl.ANY),
            pl.BlockSpec(memory_space=pl.ANY),
            pl.BlockSpec(memory_space=pl.ANY),
            pl.BlockSpec((MOE_BLOCK_ROWS, 1), lambda b, *_: (b, 0)),
        ],
        out_specs=pl.BlockSpec((MOE_BLOCK_ROWS, D_MODEL), lambda b, *_: (b, 0)),
        scratch_shapes=[
            pltpu.VMEM((2, MOE_BLOCK_ROWS, D_MODEL), F32),
            pltpu.SemaphoreType.DMA((2,)),
            pltpu.VMEM((D_MODEL, D_EXPERT), F32),
            pltpu.VMEM((D_MODEL, D_EXPERT), F32),
            pltpu.VMEM((D_EXPERT, D_MODEL), F32),
            pltpu.SemaphoreType.DMA((3,)),
            pltpu.VMEM((D_MODEL, D_EXPERT), BF16),
            pltpu.VMEM((D_MODEL, D_EXPERT), BF16),
            pltpu.VMEM((D_EXPERT, D_MODEL), BF16),
        ],
    )
    return pl.pallas_call(
        _expert_kernel,
        grid_spec=grid_spec,
        out_shape=jax.ShapeDtypeStruct((n_rows, D_MODEL), F32),
        compiler_params=pltpu.CompilerParams(dimension_semantics=("arbitrary",),
                                             vmem_limit_bytes=VMEM_LIMIT_BYTES),
        name="expert_ffn",
    )(block_e, next_e, n_used, row_tok, x1, g_ffn, w_g, w_u, w_d, row_w)


def _tail_kernel(dest_ref, x1_ref, ys_hbm, p_ref, gple_ref, wpg_ref, bpg_ref, wpu_ref, gfin_ref,
                 out_ref, ybuf, sems):
    tm = x1_ref.shape[0]
    i = pl.program_id(0)
    last = pl.num_programs(0) - 1
    slot = lax.rem(i, 2)

    def start_token(step, slt, r):
        for k in range(TOP_K):
            _row_gather_copy(ys_hbm, dest_ref[TOP_K * (step * tm + r) + k], ybuf.at[slt], k * tm + r,
                             sems.at[slt]).start()

    def wait_tile(slt):
        pltpu.make_async_copy(ys_hbm.at[pl.ds(0, TOP_K * tm), :], ybuf.at[slt], sems.at[slt]).wait()

    @pl.when(i == 0)
    def _():
        def body(r, carry):
            start_token(0, 0, r)
            return carry
        lax.fori_loop(0, tm, body, 0, unroll=GATHER_UNROLL)

    wait_tile(slot)
    nxt = jnp.minimum(i + 1, last)
    other = 1 - slot
    x2 = x1_ref[...] + ybuf[slot, :tm, :] + ybuf[slot, tm:, :]
    hn = (_rms_scale(x2) * gple_ref[...]).astype(BF16)
    up = jnp.dot(p_ref[...].astype(BF16), wpu_ref[...], preferred_element_type=F32)
    n_slabs = D_MODEL // MXU_COLS
    per = tm // n_slabs
    x3 = []
    for n in range(n_slabs):
        cols = slice(n * MXU_COLS, (n + 1) * MXU_COLS)
        for r in range(n * per, (n + 1) * per):
            start_token(nxt, other, r)
        gate = jax.nn.sigmoid(jnp.dot(hn, wpg_ref[:, cols], preferred_element_type=F32) + bpg_ref[:, cols])
        x3.append(x2[:, cols] + gate * up[:, cols])
    x3 = jnp.concatenate(x3, axis=1)
    out_ref[...] = _rms_scale(x3) * gfin_ref[...]

    @pl.when(i == last)
    def _():
        wait_tile(other)


def _tail(dest, x1, ys, p2d, g_ple, w_pg, b_pg, w_pu, g_final):
    n_tok = x1.shape[0]
    tm = TM_TAIL
    row = lambda i, *_: (i, 0)
    const = lambda i, *_: (0, 0)
    grid_spec = pltpu.PrefetchScalarGridSpec(
        num_scalar_prefetch=1,
        grid=(n_tok // tm,),
        in_specs=[
            pl.BlockSpec((tm, D_MODEL), row),
            pl.BlockSpec(memory_space=pl.ANY),
            pl.BlockSpec((tm, PLE_DIM), row),
            pl.BlockSpec((1, D_MODEL), const),
            pl.BlockSpec((D_MODEL, D_MODEL), const, pipeline_mode=pl.Buffered(1)),
            pl.BlockSpec((1, D_MODEL), const),
            pl.BlockSpec((PLE_DIM, D_MODEL), const),
            pl.BlockSpec((1, D_MODEL), const),
        ],
        out_specs=pl.BlockSpec((tm, D_MODEL), row),
        scratch_shapes=[pltpu.VMEM((2, TOP_K * tm, D_MODEL), F32), pltpu.SemaphoreType.DMA((2,))],
    )
    return pl.pallas_call(
        _tail_kernel,
        grid_spec=grid_spec,
        out_shape=jax.ShapeDtypeStruct((n_tok, D_MODEL), F32),
        compiler_params=pltpu.CompilerParams(dimension_semantics=("arbitrary",),
                                             vmem_limit_bytes=VMEM_LIMIT_BYTES),
        name="tail",
    )(dest, x1, ys, p2d, g_ple, w_pg, b_pg, w_pu, g_final)


def _dispatch_plan(expert_id, weights):
    n_tok = expert_id.shape[0]
    n_assign = n_tok * TOP_K
    n_blocks = -(-n_assign // MOE_BLOCK_ROWS) + N_EXPERTS
    n_rows = n_blocks * MOE_BLOCK_ROWS
    i32 = jnp.int32
    flat_e = expert_id.reshape(-1)
    flat_w = weights.reshape(-1)
    experts = jnp.arange(N_EXPERTS, dtype=i32)
    assign = jnp.arange(n_assign, dtype=i32)
    se, order = lax.sort((flat_e, assign), num_keys=1)
    onehot_sorted = se[:, None] == experts[None, :]
    counts = jnp.sum(onehot_sorted.astype(i32), axis=0)
    padded = (counts + MOE_BLOCK_ROWS - 1) // MOE_BLOCK_ROWS * MOE_BLOCK_ROWS
    pad_end = jnp.cumsum(padded)
    pad_start = pad_end - padded
    start = jnp.cumsum(counts) - counts
    row_of_sorted = assign + jnp.sum(jnp.where(onehot_sorted, (pad_start - start)[None, :], 0), axis=1)
    _, dest = lax.sort((order, row_of_sorted), num_keys=1)
    n_used = pad_end[-1] // MOE_BLOCK_ROWS
    rows = jnp.arange(n_rows, dtype=i32)
    row_e = jnp.minimum(jnp.sum((pad_end[None, :] <= rows[:, None]).astype(i32), axis=1), N_EXPERTS - 1)
    onehot_row = row_e[:, None] == experts[None, :]
    pick = lambda table: jnp.sum(jnp.where(onehot_row, table[None, :], 0), axis=1)
    offset = rows - pick(pad_start)
    valid = offset < pick(counts)
    src = order[jnp.clip(pick(start) + offset, 0, n_assign - 1)]
    row_tok = jnp.where(valid, src // TOP_K, 0)
    row_w = jnp.where(valid, flat_w[src], 0.0)
    block_e = row_e.reshape(n_blocks, MOE_BLOCK_ROWS)[:, 0]
    block_e = jnp.where(jnp.arange(n_blocks) < n_used, block_e, block_e[jnp.maximum(n_used - 1, 0)])
    later_used = jnp.logical_and(experts[None, :] > experts[:, None], (counts > 0)[None, :])
    next_used = jnp.min(jnp.where(later_used, experts[None, :], N_EXPERTS), axis=1)
    next_used = jnp.where(next_used < N_EXPERTS, next_used, -1)
    next_e = jnp.sum(jnp.where(block_e[:, None] == experts[None, :], next_used[None, :], 0), axis=1)
    return (block_e.astype(i32), next_e.astype(i32), n_used.astype(i32).reshape(1), row_tok.astype(i32),
            row_w.reshape(n_rows, 1), dest.astype(i32))


def kernel(x, p, g_mix, w_in, w_pool, pool_scale, w_branch_a, sgu_ln_g, sgu_ln_b, w_spatial, b_spatial, w_branch_b, w_merge_gate, b_merge_gate, w_out, g_ffn, w_router_group, b_router_group, w_router_expert, b_router_expert, w_exp_gate, w_exp_up, w_exp_down, g_ple, w_ple_gate, b_ple_gate, w_ple_up, g_final):
    bsz, seq, d = x.shape
    assert (seq, d) == (SEQ, D_MODEL) and g_mix.shape[0] == 1
    n_tok = bsz * seq
    x2d = x.reshape(n_tok, d)
    row2d = lambda v: v.reshape(1, -1)

    za, sb = _mixer_front(
        x2d, row2d(g_mix[0]), w_in[0].astype(BF16), w_pool[0].astype(BF16), row2d(pool_scale[0]),
        row2d(sgu_ln_g[0]), row2d(sgu_ln_b[0]), w_spatial[0], b_spatial[0][:, :, None])

    w_router = jnp.concatenate(
        [w_router_group[0], jnp.transpose(w_router_expert[0], (1, 0, 2)).reshape(d, N_EXPERTS)], axis=1)
    b_router = jnp.concatenate([b_router_group[0], b_router_expert[0].reshape(N_EXPERTS)])
    pad = LANES - w_router.shape[1]
    w_router = jnp.pad(w_router, ((0, 0), (0, pad)))
    b_router = jnp.pad(b_router, (0, pad))
    w_router_hi = w_router.astype(BF16)
    w_router_lo = (w_router - w_router_hi.astype(F32)).astype(BF16)
    w_router_split = jnp.concatenate([w_router_hi, w_router_lo], axis=1)

    x1, eid, wts = _mixer_back(
        x2d, za, sb, row2d(g_mix[0]), w_merge_gate[0].astype(BF16), row2d(b_merge_gate[0]),
        w_branch_a[0].astype(BF16), w_branch_b[0].astype(BF16), w_out[0].astype(BF16), row2d(g_ffn[0]),
        w_router_split, row2d(b_router))

    block_e, next_e, n_used, row_tok, row_w, dest = _dispatch_plan(eid[:, :TOP_K], wts[:, :TOP_K])
    ys = _expert_ffn(block_e, next_e, n_used, row_tok, x1, row2d(g_ffn[0]), w_exp_gate[0], w_exp_up[0],
                     w_exp_down[0], row_w)
    out = _tail(dest, x1, ys, p[0].reshape(n_tok, PLE_DIM), row2d(g_ple[0]), w_ple_gate[0].astype(BF16),
                row2d(b_ple_gate[0]), w_ple_up[0].astype(BF16), row2d(g_final))
    return out.reshape(bsz, seq, d)
```

```python
import functools

import jax
import jax.numpy as jnp
from jax import lax
from jax.experimental import pallas as pl
from jax.experimental.pallas import tpu as pltpu

F32 = jnp.float32
BF16 = jnp.bfloat16

D_MODEL = 2048
SEQ = 4096
EPS = 1e-6
PLE_DIM = 256
POOL_WINDOWS = (2, 4, 8, 16)
POOL_WIDTH = D_MODEL // 2
POOL_GROUP_DIM = POOL_WIDTH // len(POOL_WINDOWS)
POOL_HISTORY = max(POOL_WINDOWS)
SGU_BLOCK = 128
SGU_CHUNK = 64
SGU_GROUPS = 8
SGU_WIDTH = D_MODEL // 2
SGU_GROUP_DIM = SGU_WIDTH // SGU_GROUPS
N_IN = POOL_WIDTH + 2 * SGU_WIDTH
N_EXPERT_GROUPS = 4
EXPERTS_PER_GROUP = 8
N_EXPERTS = N_EXPERT_GROUPS * EXPERTS_PER_GROUP
TOP_K = 2
D_EXPERT = D_MODEL // 4
MOE_BLOCK_ROWS = 256

LANES = 128
MXU_COLS = 256
ROUTER_EXPERT_LANE0 = N_EXPERT_GROUPS
VMEM_LIMIT_BYTES = 56 * 1024 * 1024

TM_FRONT = 256
TM_BACK = 256
TM_TAIL = 256
MERGE_CHUNK = 512
GATHER_UNROLL = 8


def _rms_scale(x):
    return x * lax.rsqrt(jnp.mean(x * x, axis=-1, keepdims=True) + EPS)


def _resident(shape):
    zeros = (0,) * len(shape)
    return pl.BlockSpec(shape, lambda *_: zeros, pipeline_mode=pl.Buffered(1))


def _mixer_front_kernel(x_ref, gmix_ref, win_ref, wpool_ref, pscale_ref, lng_ref, lnb_ref, ws_ref,
                        bsp_ref, za_ref, sb_ref, hist_ref):
    tm = x_ref.shape[0]
    tiles_per_seq = SEQ // tm
    seq_tile = lax.rem(pl.program_id(0), tiles_per_seq)

    @pl.when(seq_tile == 0)
    def _():
        hist_ref[...] = jnp.zeros_like(hist_ref)

    h = (_rms_scale(x_ref[...]) * gmix_ref[...]).astype(BF16)
    z = jnp.dot(h, win_ref[...], preferred_element_type=F32)

    a = z[:, :POOL_WIDTH]
    ext = jnp.concatenate([hist_ref[...], a], axis=0)
    hist_ref[...] = a[tm - POOL_HISTORY:, :]
    frames = (seq_tile * tm + 1 + lax.broadcasted_iota(jnp.int32, (tm, 1), 0)).astype(F32)
    for gi, w in enumerate(POOL_WINDOWS):
        cols = slice(gi * POOL_GROUP_DIM, (gi + 1) * POOL_GROUP_DIM)
        s = ext[:, cols]
        k = 1
        while k < w:
            s = s + pltpu.roll(s, k, 0)
            k *= 2
        wsum = s[POOL_HISTORY:, :]
        zg = wsum / jnp.minimum(frames, float(w)) - a[:, cols]
        yg = jnp.dot(zg.astype(BF16), wpool_ref[gi], preferred_element_type=F32)
        za_ref[:, cols] = (yg * pscale_ref[:, cols]).astype(BF16)

    u = jax.nn.gelu(z[:, POOL_WIDTH:POOL_WIDTH + SGU_WIDTH])
    v = jax.nn.gelu(z[:, POOL_WIDTH + SGU_WIDTH:])
    vc = v - jnp.mean(v, axis=-1, keepdims=True)
    var = jnp.mean(vc * vc, axis=-1, keepdims=True)
    vn = (vc * lax.rsqrt(var + EPS) * lng_ref[...] + lnb_ref[...]).astype(BF16)
    t_chunk = lax.broadcasted_iota(jnp.int32, (SGU_BLOCK, SGU_BLOCK), 0) // SGU_CHUNK
    s_chunk = lax.broadcasted_iota(jnp.int32, (SGU_BLOCK, SGU_BLOCK), 1) // SGU_CHUNK
    causal = s_chunk <= t_chunk
    nblk = tm // SGU_BLOCK
    for g in range(SGU_GROUPS):
        cols = slice(g * SGU_GROUP_DIM, (g + 1) * SGU_GROUP_DIM)
        wsg = jnp.where(causal, ws_ref[g], 0.0).astype(BF16)
        vg = jnp.concatenate([vn[j * SGU_BLOCK:(j + 1) * SGU_BLOCK, cols] for j in range(nblk)], axis=1)
        vm = jnp.dot(wsg, vg, preferred_element_type=F32) + bsp_ref[g]
        for j in range(nblk):
            rows = slice(j * SGU_BLOCK, (j + 1) * SGU_BLOCK)
            sb_ref[rows, cols] = (u[rows, cols] * vm[:, j * SGU_GROUP_DIM:(j + 1) * SGU_GROUP_DIM]).astype(BF16)


def _mixer_front(x2d, g_mix, w_in, w_pool, pool_scale, ln_g, ln_b, w_spatial, b_spatial):
    n_tok = x2d.shape[0]
    tm = TM_FRONT
    row = lambda i: (i, 0)
    return pl.pallas_call(
        _mixer_front_kernel,
        grid=(n_tok // tm,),
        in_specs=[
            pl.BlockSpec((tm, D_MODEL), row),
            _resident((1, D_MODEL)),
            _resident((D_MODEL, N_IN)),
            _resident((len(POOL_WINDOWS), POOL_GROUP_DIM, POOL_GROUP_DIM)),
            _resident((1, POOL_WIDTH)),
            _resident((1, SGU_WIDTH)),
            _resident((1, SGU_WIDTH)),
            _resident((SGU_GROUPS, SGU_BLOCK, SGU_BLOCK)),
            _resident((SGU_GROUPS, SGU_BLOCK, 1)),
        ],
        out_specs=[pl.BlockSpec((tm, POOL_WIDTH), row), pl.BlockSpec((tm, SGU_WIDTH), row)],
        out_shape=[jax.ShapeDtypeStruct((n_tok, POOL_WIDTH), BF16),
                   jax.ShapeDtypeStruct((n_tok, SGU_WIDTH), BF16)],
        scratch_shapes=[pltpu.VMEM((POOL_HISTORY, POOL_WIDTH), F32)],
        compiler_params=pltpu.CompilerParams(dimension_semantics=("arbitrary",),
                                             vmem_limit_bytes=VMEM_LIMIT_BYTES),
        name="mixer_front",
    )(x2d, g_mix, w_in, w_pool, pool_scale, ln_g, ln_b, w_spatial, b_spatial)


def _route(logits):
    lane = lax.broadcasted_iota(jnp.int32, logits.shape, 1).astype(F32)
    neg = -jnp.inf
    far = float(LANES)

    def first_argmax(vals):
        top = jnp.max(vals, axis=-1, keepdims=True)
        return top, jnp.min(jnp.where(vals == top, lane, far), axis=-1, keepdims=True)

    is_grp = lane < float(N_EXPERT_GROUPS)
    g_top, g_idx = first_argmax(jnp.where(is_grp, logits, neg))
    g_den = jnp.sum(jnp.where(is_grp, jnp.exp(logits - g_top), 0.0), axis=-1, keepdims=True)
    grp_p = 1.0 / g_den
    lo = float(ROUTER_EXPERT_LANE0) + g_idx * float(EXPERTS_PER_GROUP)
    e_log = jnp.where(lane >= lo, jnp.where(lane < lo + float(EXPERTS_PER_GROUP), logits, neg), neg)
    t1, i1 = first_argmax(e_log)
    t2, i2 = first_argmax(jnp.where(lane == i1, neg, e_log))
    r = jnp.exp(t2 - t1)
    w1 = grp_p / (1.0 + r)
    w2 = grp_p * r / (1.0 + r)
    e1 = i1 - float(ROUTER_EXPERT_LANE0)
    e2 = i2 - float(ROUTER_EXPERT_LANE0)
    eid = jnp.where(lane == 0.0, e1, jnp.where(lane == 1.0, e2, 0.0)).astype(jnp.int32)
    wts = jnp.where(lane == 0.0, w1, jnp.where(lane == 1.0, w2, 0.0))
    return eid, wts


def _mixer_back_kernel(x_ref, za_ref, sb_ref, gmix_ref, wm_ref, bm_ref, wa_ref, wb_ref, wo_ref,
                       gffn_ref, wr_ref, br_ref, x1_ref, eid_ref, wts_ref):
    x = x_ref[...]
    h = (_rms_scale(x) * gmix_ref[...]).astype(BF16)
    za = za_ref[...]
    sb = sb_ref[...]
    acc = jnp.zeros(x.shape, F32)
    for c in range(D_MODEL // MERGE_CHUNK):
        ca = slice(c * MERGE_CHUNK, (c + 1) * MERGE_CHUNK)
        cb = slice(D_MODEL + c * MERGE_CHUNK, D_MODEL + (c + 1) * MERGE_CHUNK)
        ga = jax.nn.sigmoid(jnp.dot(h, wm_ref[:, ca], preferred_element_type=F32) + bm_ref[:, ca])
        gb = jax.nn.sigmoid(jnp.dot(h, wm_ref[:, cb], preferred_element_type=F32) + bm_ref[:, cb])
        ya = jnp.dot(za, wa_ref[:, ca], preferred_element_type=F32)
        yb = jnp.dot(sb, wb_ref[:, ca], preferred_element_type=F32)
        merged = (ga * ya + gb * yb).astype(BF16)
        acc = acc + jnp.dot(merged, wo_ref[ca, :], preferred_element_type=F32)
    x1 = x + acc
    x1_ref[...] = x1
    h2 = _rms_scale(x1) * gffn_ref[...]
    h2_hi = h2.astype(BF16)
    h2_lo = (h2 - h2_hi.astype(F32)).astype(BF16)
    hi_terms = jnp.dot(h2_hi, wr_ref[...], preferred_element_type=F32)
    lo_term = jnp.dot(h2_lo, wr_ref[:, :LANES], preferred_element_type=F32)
    logits = hi_terms[:, :LANES] + hi_terms[:, LANES:] + lo_term + br_ref[...]
    eid, wts = _route(logits)
    eid_ref[...] = eid
    wts_ref[...] = wts


def _mixer_back(x2d, za, sb, g_mix, w_merge, b_merge, w_a, w_b, w_out, g_ffn, w_router, b_router):
    n_tok = x2d.shape[0]
    tm = TM_BACK
    row = lambda i: (i, 0)
    return pl.pallas_call(
        _mixer_back_kernel,
        grid=(n_tok // tm,),
        in_specs=[
            pl.BlockSpec((tm, D_MODEL), row),
            pl.BlockSpec((tm, POOL_WIDTH), row),
            pl.BlockSpec((tm, SGU_WIDTH), row),
            _resident((1, D_MODEL)),
            _resident((D_MODEL, 2 * D_MODEL)),
            _resident((1, 2 * D_MODEL)),
            _resident((POOL_WIDTH, D_MODEL)),
            _resident((SGU_WIDTH, D_MODEL)),
            _resident((D_MODEL, D_MODEL)),
            _resident((1, D_MODEL)),
            _resident((D_MODEL, 2 * LANES)),
            _resident((1, LANES)),
        ],
        out_specs=[pl.BlockSpec((tm, D_MODEL), row), pl.BlockSpec((tm, LANES), row),
                   pl.BlockSpec((tm, LANES), row)],
        out_shape=[jax.ShapeDtypeStruct((n_tok, D_MODEL), F32),
                   jax.ShapeDtypeStruct((n_tok, LANES), jnp.int32),
                   jax.ShapeDtypeStruct((n_tok, LANES), F32)],
        compiler_params=pltpu.CompilerParams(dimension_semantics=("arbitrary",),
                                             vmem_limit_bytes=VMEM_LIMIT_BYTES),
        name="mixer_back",
    )(x2d, za, sb, g_mix, w_merge, b_merge, w_a, w_b, w_out, g_ffn, w_router, b_router)


def _row_gather_copy(src_hbm, src_row, dst_vmem, dst_row, sem):
    return pltpu.make_async_copy(src_hbm.at[pl.ds(src_row, 1), :], dst_vmem.at[pl.ds(dst_row, 1), :], sem)


def _expert_kernel(be_ref, nexte_ref, nused_ref, rowtok_ref, x1_hbm, gffn_ref, wg_hbm, wu_hbm, wd_hbm,
                   roww_ref, ys_ref, xbuf, sems, wg_st, wu_st, wd_st, wsems, wg_bf, wu_bf, wd_bf):
    rows = MOE_BLOCK_ROWS
    b = pl.program_id(0)
    n_used = nused_ref[0]
    slot = lax.rem(b, 2)

    def row_copy(blk, slt, r):
        return _row_gather_copy(x1_hbm, rowtok_ref[blk * rows + r], xbuf.at[slt], r, sems.at[slt])

    def wait_block(slt):
        pltpu.make_async_copy(x1_hbm.at[pl.ds(0, rows), :], xbuf.at[slt], sems.at[slt]).wait()

    @pl.when(b == 0)
    def _():
        def body(r, carry):
            row_copy(0, 0, r).start()
            return carry
        lax.fori_loop(0, rows, body, 0, unroll=GATHER_UNROLL)

    def weight_copies(e):
        return [pltpu.make_async_copy(src.at[e], dst, wsems.at[j])
                for j, (src, dst) in enumerate(((wg_hbm, wg_st), (wu_hbm, wu_st), (wd_hbm, wd_st)))]

    @pl.when(b < n_used)
    def _():
        @pl.when(b == 0)
        def _():
            for cp in weight_copies(be_ref[0]):
                cp.start()

        @pl.when(jnp.logical_or(b == 0, be_ref[b] != be_ref[jnp.maximum(b - 1, 0)]))
        def _():
            for cp in weight_copies(be_ref[b]):
                cp.wait()
            wg_bf[...] = wg_st[...].astype(BF16)
            wu_bf[...] = wu_st[...].astype(BF16)
            wd_bf[...] = wd_st[...].astype(BF16)
            nxt_e = nexte_ref[b]

            @pl.when(nxt_e >= 0)
            def _():
                for cp in weight_copies(nxt_e):
                    cp.start()

        nxt = jnp.minimum(b + 1, n_used - 1)
        other = 1 - slot
        for r in range(rows):
            row_copy(nxt, other, r).start()
        wait_block(slot)

        h2 = (_rms_scale(xbuf[slot]) * gffn_ref[...]).astype(BF16)
        gate = jnp.dot(h2, wg_bf[...], preferred_element_type=F32)
        up = jnp.dot(h2, wu_bf[...], preferred_element_type=F32)
        hid = (jax.nn.silu(gate) * up).astype(BF16)
        ys_ref[...] = jnp.dot(hid, wd_bf[...], preferred_element_type=F32) * roww_ref[...]

        @pl.when(b == n_used - 1)
        def _():
            wait_block(other)

    @pl.when(b >= n_used)
    def _():
        ys_ref[...] = jnp.zeros_like(ys_ref)


def _expert_ffn(block_e, next_e, n_used, row_tok, x1, g_ffn, w_g, w_u, w_d, row_w):
    n_rows = row_tok.shape[0]
    n_blocks = n_rows // MOE_BLOCK_ROWS
    grid_spec = pltpu.PrefetchScalarGridSpec(
        num_scalar_prefetch=4,
        grid=(n_blocks,),
        in_specs=[
            pl.BlockSpec(memory_space=pl.ANY),
            pl.BlockSpec((1, D_MODEL), lambda b, *_: (0, 0)),
            pl.BlockSpec(memory_space=pl.ANY),
            pl.BlockSpec(memory_space=pl.ANY),
            pl.BlockSpec(memory_space=pl.ANY),
            pl.BlockSpec((MOE_BLOCK_ROWS, 1), lambda b, *_: (b, 0)),
        ],
        out_specs=pl.BlockSpec((MOE_BLOCK_ROWS, D_MODEL), lambda b, *_: (b, 0)),
        scratch_shapes=[
            pltpu.VMEM((2, MOE_BLOCK_ROWS, D_MODEL), F32),
            pltpu.SemaphoreType.DMA((2,)),
            pltpu.VMEM((D_MODEL, D_EXPERT), F32),
            pltpu.VMEM((D_MODEL, D_EXPERT), F32),
            pltpu.VMEM((D_EXPERT, D_MODEL), F32),
            pltpu.SemaphoreType.DMA((3,)),
            pltpu.VMEM((D_MODEL, D_EXPERT), BF16),
            pltpu.VMEM((D_MODEL, D_EXPERT), BF16),
            pltpu.VMEM((D_EXPERT, D_MODEL), BF16),
        ],
    )
    return pl.pallas_call(
        _expert_kernel,
        grid_spec=grid_spec,
        out_shape=jax.ShapeDtypeStruct((n_rows, D_MODEL), F32),
        compiler_params=pltpu.CompilerParams(dimension_semantics=("arbitrary",),
                                             vmem_limit_bytes=VMEM_LIMIT_BYTES),
        name="expert_ffn",
    )(block_e, next_e, n_used, row_tok, x1, g_ffn, w_g, w_u, w_d, row_w)


def _tail_kernel(dest_ref, x1_ref, ys_hbm, p_ref, gple_ref, wpg_ref, bpg_ref, wpu_ref, gfin_ref,
                 out_ref, ybuf, sems):
    tm = x1_ref.shape[0]
    i = pl.program_id(0)
    last = pl.num_programs(0) - 1
    slot = lax.rem(i, 2)

    def start_token(step, slt, r):
        for k in range(TOP_K):
            _row_gather_copy(ys_hbm, dest_ref[TOP_K * (step * tm + r) + k], ybuf.at[slt], k * tm + r,
                             sems.at[slt]).start()

    def wait_tile(slt):
        pltpu.make_async_copy(ys_hbm.at[pl.ds(0, TOP_K * tm), :], ybuf.at[slt], sems.at[slt]).wait()

    @pl.when(i == 0)
    def _():
        def body(r, carry):
            start_token(0, 0, r)
            return carry
        lax.fori_loop(0, tm, body, 0, unroll=GATHER_UNROLL)

    wait_tile(slot)
    nxt = jnp.minimum(i + 1, last)
    other = 1 - slot
    x2 = x1_ref[...] + ybuf[slot, :tm, :] + ybuf[slot, tm:, :]
    hn = (_rms_scale(x2) * gple_ref[...]).astype(BF16)
    up = jnp.dot(p_ref[...].astype(BF16), wpu_ref[...], preferred_element_type=F32)
    n_slabs = D_MODEL // MXU_COLS
    per = tm // n_slabs
    x3 = []
    for n in range(n_slabs):
        cols = slice(n * MXU_COLS, (n + 1) * MXU_COLS)
        for r in range(n * per, (n + 1) * per):
            start_token(nxt, other, r)
        gate = jax.nn.sigmoid(jnp.dot(hn, wpg_ref[:, cols], preferred_element_type=F32) + bpg_ref[:, cols])
        x3.append(x2[:, cols] + gate * up[:, cols])
    x3 = jnp.concatenate(x3, axis=1)
    out_ref[...] = _rms_scale(x3) * gfin_ref[...]

    @pl.when(i == last)
    def _():
        wait_tile(other)


def _tail(dest, x1, ys, p2d, g_ple, w_pg, b_pg, w_pu, g_final):
    n_tok = x1.shape[0]
    tm = TM_TAIL
    row = lambda i, *_: (i, 0)
    const = lambda i, *_: (0, 0)
    grid_spec = pltpu.PrefetchScalarGridSpec(
        num_scalar_prefetch=1,
        grid=(n_tok // tm,),
        in_specs=[
            pl.BlockSpec((tm, D_MODEL), row),
            pl.BlockSpec(memory_space=pl.ANY),
            pl.BlockSpec((tm, PLE_DIM), row),
            pl.BlockSpec((1, D_MODEL), const),
            pl.BlockSpec((D_MODEL, D_MODEL), const, pipeline_mode=pl.Buffered(1)),
            pl.BlockSpec((1, D_MODEL), const),
            pl.BlockSpec((PLE_DIM, D_MODEL), const),
            pl.BlockSpec((1, D_MODEL), const),
        ],
        out_specs=pl.BlockSpec((tm, D_MODEL), row),
        scratch_shapes=[pltpu.VMEM((2, TOP_K * tm, D_MODEL), F32), pltpu.SemaphoreType.DMA((2,))],
    )
    return pl.pallas_call(
        _tail_kernel,
        grid_spec=grid_spec,
        out_shape=jax.ShapeDtypeStruct((n_tok, D_MODEL), F32),
        compiler_params=pltpu.CompilerParams(dimension_semantics=("arbitrary",),
                                             vmem_limit_bytes=VMEM_LIMIT_BYTES),
        name="tail",
    )(dest, x1, ys, p2d, g_ple, w_pg, b_pg, w_pu, g_final)


def _dispatch_plan(expert_id, weights):
    n_tok = expert_id.shape[0]
    n_assign = n_tok * TOP_K
    n_blocks = -(-n_assign // MOE_BLOCK_ROWS) + N_EXPERTS
    n_rows = n_blocks * MOE_BLOCK_ROWS
    i32 = jnp.int32
    flat_e = expert_id.reshape(-1)
    flat_w = weights.reshape(-1)
    experts = jnp.arange(N_EXPERTS, dtype=i32)
    assign = jnp.arange(n_assign, dtype=i32)
    se, order = lax.sort((flat_e, assign), num_keys=1)
    onehot_sorted = se[:, None] == experts[None, :]
    counts = jnp.sum(onehot_sorted.astype(i32), axis=0)
    padded = (counts + MOE_BLOCK_ROWS - 1) // MOE_BLOCK_ROWS * MOE_BLOCK_ROWS
    pad_end = jnp.cumsum(padded)
    pad_start = pad_end - padded
    start = jnp.cumsum(counts) - counts
    row_of_sorted = assign + jnp.sum(jnp.where(onehot_sorted, (pad_start - start)[None, :], 0), axis=1)
    _, dest = lax.sort((order, row_of_sorted), num_keys=1)
    n_used = pad_end[-1] // MOE_BLOCK_ROWS
    rows = jnp.arange(n_rows, dtype=i32)
    row_e = jnp.minimum(jnp.sum((pad_end[None, :] <= rows[:, None]).astype(i32), axis=1), N_EXPERTS - 1)
    onehot_row = row_e[:, None] == experts[None, :]
    pick = lambda table: jnp.sum(jnp.where(onehot_row, table[None, :], 0), axis=1)
    offset = rows - pick(pad_start)
    valid = offset < pick(counts)
    src = order[jnp.clip(pick(start) + offset, 0, n_assign - 1)]
    row_tok = jnp.where(valid, src // TOP_K, 0)
    row_w = jnp.where(valid, flat_w[src], 0.0)
    block_e = row_e.reshape(n_blocks, MOE_BLOCK_ROWS)[:, 0]
    block_e = jnp.where(jnp.arange(n_blocks) < n_used, block_e, block_e[jnp.maximum(n_used - 1, 0)])
    later_used = jnp.logical_and(experts[None, :] > experts[:, None], (counts > 0)[None, :])
    next_used = jnp.min(jnp.where(later_used, experts[None, :], N_EXPERTS), axis=1)
    next_used = jnp.where(next_used < N_EXPERTS, next_used, -1)
    next_e = jnp.sum(jnp.where(block_e[:, None] == experts[None, :], next_used[None, :], 0), axis=1)
    return (block_e.astype(i32), next_e.astype(i32), n_used.astype(i32).reshape(1), row_tok.astype(i32),
            row_w.reshape(n_rows, 1), dest.astype(i32))


def kernel(x, p, g_mix, w_in, w_pool, pool_scale, w_branch_a, sgu_ln_g, sgu_ln_b, w_spatial, b_spatial, w_branch_b, w_merge_gate, b_merge_gate, w_out, g_ffn, w_router_group, b_router_group, w_router_expert, b_router_expert, w_exp_gate, w_exp_up, w_exp_down, g_ple, w_ple_gate, b_ple_gate, w_ple_up, g_final):
    bsz, seq, d = x.shape
    assert (seq, d) == (SEQ, D_MODEL) and g_mix.shape[0] == 1
    n_tok = bsz * seq
    x2d = x.reshape(n_tok, d)
    row2d = lambda v: v.reshape(1, -1)

    za, sb = _mixer_front(
        x2d, row2d(g_mix[0]), w_in[0].astype(BF16), w_pool[0].astype(BF16), row2d(pool_scale[0]),
        row2d(sgu_ln_g[0]), row2d(sgu_ln_b[0]), w_spatial[0], b_spatial[0][:, :, None])

    w_router = jnp.concatenate(
        [w_router_group[0], jnp.transpose(w_router_expert[0], (1, 0, 2)).reshape(d, N_EXPERTS)], axis=1)
    b_router = jnp.concatenate([b_router_group[0], b_router_expert[0].reshape(N_EXPERTS)])
    pad = LANES - w_router.shape[1]
    w_router = jnp.pad(w_router, ((0, 0), (0, pad)))
    b_router = jnp.pad(b_router, (0, pad))
    w_router_hi = w_router.astype(BF16)
    w_router_lo = (w_router - w_router_hi.astype(F32)).astype(BF16)
    w_router_split = jnp.concatenate([w_router_hi, w_router_lo], axis=1)

    x1, eid, wts = _mixer_back(
        x2d, za, sb, row2d(g_mix[0]), w_merge_gate[0].astype(BF16), row2d(b_merge_gate[0]),
        w_branch_a[0].astype(BF16), w_branch_b[0].astype(BF16), w_out[0].astype(BF16), row2d(g_ffn[0]),
        w_router_split, row2d(b_router))

    block_e, next_e, n_used, row_tok, row_w, dest = _dispatch_plan(eid[:, :TOP_K], wts[:, :TOP_K])
    ys = _expert_ffn(block_e, next_e, n_used, row_tok, x1, row2d(g_ffn[0]), w_exp_gate[0], w_exp_up[0],
                     w_exp_down[0], row_w)
    out = _tail(dest, x1, ys, p[0].reshape(n_tok, PLE_DIM), row2d(g_ple[0]), w_ple_gate[0].astype(BF16),
                row2d(b_ple_gate[0]), w_ple_up[0].astype(BF16), row2d(g_final))
    return out.reshape(bsz, seq, d)
```

```python
import functools

import jax
import jax.numpy as jnp
from jax import lax
from jax.experimental import pallas as pl
from jax.experimental.pallas import tpu as pltpu

F32 = jnp.float32
BF16 = jnp.bfloat16

D_MODEL = 2048
SEQ = 4096
EPS = 1e-6
PLE_DIM = 256
POOL_WINDOWS = (2, 4, 8, 16)
POOL_WIDTH = D_MODEL // 2
POOL_GROUP_DIM = POOL_WIDTH // len(POOL_WINDOWS)
POOL_HISTORY = max(POOL_WINDOWS)
SGU_BLOCK = 128
SGU_CHUNK = 64
SGU_GROUPS = 8
SGU_WIDTH = D_MODEL // 2
SGU_GROUP_DIM = SGU_WIDTH // SGU_GROUPS
N_IN = POOL_WIDTH + 2 * SGU_WIDTH
N_EXPERT_GROUPS = 4
EXPERTS_PER_GROUP = 8
N_EXPERTS = N_EXPERT_GROUPS * EXPERTS_PER_GROUP
TOP_K = 2
D_EXPERT = D_MODEL // 4
MOE_BLOCK_ROWS = 256

LANES = 128
MXU_COLS = 256
ROUTER_EXPERT_LANE0 = N_EXPERT_GROUPS
VMEM_LIMIT_BYTES = 56 * 1024 * 1024

TM_FRONT = 256
TM_BACK = 256
TM_TAIL = 256
MERGE_CHUNK = 512
WEIGHT_DMA_PRIORITY = 1
GATHER_UNROLL = 8


def _rms_scale(x):
    return x * lax.rsqrt(jnp.mean(x * x, axis=-1, keepdims=True) + EPS)


def _resident(shape):
    zeros = (0,) * len(shape)
    return pl.BlockSpec(shape, lambda *_: zeros, pipeline_mode=pl.Buffered(1))


def _mixer_front_kernel(x_ref, gmix_ref, win_ref, wpool_ref, pscale_ref, lng_ref, lnb_ref, ws_ref,
                        bsp_ref, za_ref, sb_ref, hist_ref):
    tm = x_ref.shape[0]
    tiles_per_seq = SEQ // tm
    seq_tile = lax.rem(pl.program_id(0), tiles_per_seq)

    @pl.when(seq_tile == 0)
    def _():
        hist_ref[...] = jnp.zeros_like(hist_ref)

    h = (_rms_scale(x_ref[...]) * gmix_ref[...]).astype(BF16)
    z = jnp.dot(h, win_ref[...], preferred_element_type=F32)

    a = z[:, :POOL_WIDTH]
    ext = jnp.concatenate([hist_ref[...], a], axis=0)
    hist_ref[...] = a[tm - POOL_HISTORY:, :]
    frames = (seq_tile * tm + 1 + lax.broadcasted_iota(jnp.int32, (tm, 1), 0)).astype(F32)
    for gi, w in enumerate(POOL_WINDOWS):
        cols = slice(gi * POOL_GROUP_DIM, (gi + 1) * POOL_GROUP_DIM)
        s = ext[:, cols]
        k = 1
        while k < w:
            s = s + pltpu.roll(s, k, 0)
            k *= 2
        wsum = s[POOL_HISTORY:, :]
        zg = wsum / jnp.minimum(frames, float(w)) - a[:, cols]
        yg = jnp.dot(zg.astype(BF16), wpool_ref[gi], preferred_element_type=F32)
        za_ref[:, cols] = (yg * pscale_ref[:, cols]).astype(BF16)

    u = jax.nn.gelu(z[:, POOL_WIDTH:POOL_WIDTH + SGU_WIDTH])
    v = jax.nn.gelu(z[:, POOL_WIDTH + SGU_WIDTH:])
    vc = v - jnp.mean(v, axis=-1, keepdims=True)
    var = jnp.mean(vc * vc, axis=-1, keepdims=True)
    vn = (vc * lax.rsqrt(var + EPS) * lng_ref[...] + lnb_ref[...]).astype(BF16)
    t_chunk = lax.broadcasted_iota(jnp.int32, (SGU_BLOCK, SGU_BLOCK), 0) // SGU_CHUNK
    s_chunk = lax.broadcasted_iota(jnp.int32, (SGU_BLOCK, SGU_BLOCK), 1) // SGU_CHUNK
    causal = s_chunk <= t_chunk
    nblk = tm // SGU_BLOCK
    for g in range(SGU_GROUPS):
        cols = slice(g * SGU_GROUP_DIM, (g + 1) * SGU_GROUP_DIM)
        wsg = jnp.where(causal, ws_ref[g], 0.0).astype(BF16)
        vg = jnp.concatenate([vn[j * SGU_BLOCK:(j + 1) * SGU_BLOCK, cols] for j in range(nblk)], axis=1)
        vm = jnp.dot(wsg, vg, preferred_element_type=F32) + bsp_ref[g]
        for j in range(nblk):
            rows = slice(j * SGU_BLOCK, (j + 1) * SGU_BLOCK)
            sb_ref[rows, cols] = (u[rows, cols] * vm[:, j * SGU_GROUP_DIM:(j + 1) * SGU_GROUP_DIM]).astype(BF16)


def _mixer_front(x2d, g_mix, w_in, w_pool, pool_scale, ln_g, ln_b, w_spatial, b_spatial):
    n_tok = x2d.shape[0]
    tm = TM_FRONT
    row = lambda i: (i, 0)
    return pl.pallas_call(
        _mixer_front_kernel,
        grid=(n_tok // tm,),
        in_specs=[
            pl.BlockSpec((tm, D_MODEL), row),
            _resident((1, D_MODEL)),
            _resident((D_MODEL, N_IN)),
            _resident((len(POOL_WINDOWS), POOL_GROUP_DIM, POOL_GROUP_DIM)),
            _resident((1, POOL_WIDTH)),
            _resident((1, SGU_WIDTH)),
            _resident((1, SGU_WIDTH)),
            _resident((SGU_GROUPS, SGU_BLOCK, SGU_BLOCK)),
            _resident((SGU_GROUPS, SGU_BLOCK, 1)),
        ],
        out_specs=[pl.BlockSpec((tm, POOL_WIDTH), row), pl.BlockSpec((tm, SGU_WIDTH), row)],
        out_shape=[jax.ShapeDtypeStruct((n_tok, POOL_WIDTH), BF16),
                   jax.ShapeDtypeStruct((n_tok, SGU_WIDTH), BF16)],
        scratch_shapes=[pltpu.VMEM((POOL_HISTORY, POOL_WIDTH), F32)],
        compiler_params=pltpu.CompilerParams(dimension_semantics=("arbitrary",),
                                             vmem_limit_bytes=VMEM_LIMIT_BYTES),
        name="mixer_front",
    )(x2d, g_mix, w_in, w_pool, pool_scale, ln_g, ln_b, w_spatial, b_spatial)


def _route(logits):
    lane = lax.broadcasted_iota(jnp.int32, logits.shape, 1).astype(F32)
    neg = -jnp.inf
    far = float(LANES)

    def first_argmax(vals):
        top = jnp.max(vals, axis=-1, keepdims=True)
        return top, jnp.min(jnp.where(vals == top, lane, far), axis=-1, keepdims=True)

    is_grp = lane < float(N_EXPERT_GROUPS)
    g_top, g_idx = first_argmax(jnp.where(is_grp, logits, neg))
    g_den = jnp.sum(jnp.where(is_grp, jnp.exp(logits - g_top), 0.0), axis=-1, keepdims=True)
    grp_p = 1.0 / g_den
    lo = float(ROUTER_EXPERT_LANE0) + g_idx * float(EXPERTS_PER_GROUP)
    e_log = jnp.where(lane >= lo, jnp.where(lane < lo + float(EXPERTS_PER_GROUP), logits, neg), neg)
    t1, i1 = first_argmax(e_log)
    t2, i2 = first_argmax(jnp.where(lane == i1, neg, e_log))
    r = jnp.exp(t2 - t1)
    w1 = grp_p / (1.0 + r)
    w2 = grp_p * r / (1.0 + r)
    e1 = i1 - float(ROUTER_EXPERT_LANE0)
    e2 = i2 - float(ROUTER_EXPERT_LANE0)
    eid = jnp.where(lane == 0.0, e1, jnp.where(lane == 1.0, e2, 0.0)).astype(jnp.int32)
    wts = jnp.where(lane == 0.0, w1, jnp.where(lane == 1.0, w2, 0.0))
    return eid, wts


def _mixer_back_kernel(x_ref, za_ref, sb_ref, gmix_ref, wm_ref, bm_ref, wa_ref, wb_ref, wo_ref,
                       gffn_ref, wr_ref, br_ref, x1_ref, eid_ref, wts_ref):
    x = x_ref[...]
    h = (_rms_scale(x) * gmix_ref[...]).astype(BF16)
    za = za_ref[...]
    sb = sb_ref[...]
    acc = jnp.zeros(x.shape, F32)
    for c in range(D_MODEL // MERGE_CHUNK):
        ca = slice(c * MERGE_CHUNK, (c + 1) * MERGE_CHUNK)
        cb = slice(D_MODEL + c * MERGE_CHUNK, D_MODEL + (c + 1) * MERGE_CHUNK)
        ga = jax.nn.sigmoid(jnp.dot(h, wm_ref[:, ca], preferred_element_type=F32) + bm_ref[:, ca])
        gb = jax.nn.sigmoid(jnp.dot(h, wm_ref[:, cb], preferred_element_type=F32) + bm_ref[:, cb])
        ya = jnp.dot(za, wa_ref[:, ca], preferred_element_type=F32)
        yb = jnp.dot(sb, wb_ref[:, ca], preferred_element_type=F32)
        merged = (ga * ya + gb * yb).astype(BF16)
        acc = acc + jnp.dot(merged, wo_ref[ca, :], preferred_element_type=F32)
    x1 = x + acc
    x1_ref[...] = x1
    h2 = _rms_scale(x1) * gffn_ref[...]
    h2_hi = h2.astype(BF16)
    h2_lo = (h2 - h2_hi.astype(F32)).astype(BF16)
    hi_terms = jnp.dot(h2_hi, wr_ref[...], preferred_element_type=F32)
    lo_term = jnp.dot(h2_lo, wr_ref[:, :LANES], preferred_element_type=F32)
    logits = hi_terms[:, :LANES] + hi_terms[:, LANES:] + lo_term + br_ref[...]
    eid, wts = _route(logits)
    eid_ref[...] = eid
    wts_ref[...] = wts


def _mixer_back(x2d, za, sb, g_mix, w_merge, b_merge, w_a, w_b, w_out, g_ffn, w_router, b_router):
    n_tok = x2d.shape[0]
    tm = TM_BACK
    row = lambda i: (i, 0)
    return pl.pallas_call(
        _mixer_back_kernel,
        grid=(n_tok // tm,),
        in_specs=[
            pl.BlockSpec((tm, D_MODEL), row),
            pl.BlockSpec((tm, POOL_WIDTH), row),
            pl.BlockSpec((tm, SGU_WIDTH), row),
            _resident((1, D_MODEL)),
            _resident((D_MODEL, 2 * D_MODEL)),
            _resident((1, 2 * D_MODEL)),
            _resident((POOL_WIDTH, D_MODEL)),
            _resident((SGU_WIDTH, D_MODEL)),
            _resident((D_MODEL, D_MODEL)),
            _resident((1, D_MODEL)),
            _resident((D_MODEL, 2 * LANES)),
            _resident((1, LANES)),
        ],
        out_specs=[pl.BlockSpec((tm, D_MODEL), row), pl.BlockSpec((tm, LANES), row),
                   pl.BlockSpec((tm, LANES), row)],
        out_shape=[jax.ShapeDtypeStruct((n_tok, D_MODEL), F32),
                   jax.ShapeDtypeStruct((n_tok, LANES), jnp.int32),
                   jax.ShapeDtypeStruct((n_tok, LANES), F32)],
        compiler_params=pltpu.CompilerParams(dimension_semantics=("arbitrary",),
                                             vmem_limit_bytes=VMEM_LIMIT_BYTES),
        name="mixer_back",
    )(x2d, za, sb, g_mix, w_merge, b_merge, w_a, w_b, w_out, g_ffn, w_router, b_router)


def _row_gather_copy(src_hbm, src_row, dst_vmem, dst_row, sem):
    return pltpu.make_async_copy(src_hbm.at[pl.ds(src_row, 1), :], dst_vmem.at[pl.ds(dst_row, 1), :], sem)


def _expert_kernel(be_ref, nexte_ref, nused_ref, rowtok_ref, x1_hbm, gffn_ref, wg_hbm, wu_hbm, wd_hbm,
                   roww_ref, ys_ref, xbuf, sems, wg_st, wu_st, wd_st, wsems, wg_bf, wu_bf, wd_bf):
    rows = MOE_BLOCK_ROWS
    b = pl.program_id(0)
    n_used = nused_ref[0]
    slot = lax.rem(b, 2)

    def row_copy(blk, slt, r):
        return _row_gather_copy(x1_hbm, rowtok_ref[blk * rows + r], xbuf.at[slt], r, sems.at[slt])

    def wait_block(slt):
        pltpu.make_async_copy(x1_hbm.at[pl.ds(0, rows), :], xbuf.at[slt], sems.at[slt]).wait()

    @pl.when(b == 0)
    def _():
        def body(r, carry):
            row_copy(0, 0, r).start()
            return carry
        lax.fori_loop(0, rows, body, 0, unroll=GATHER_UNROLL)

    def weight_copies(e):
        return [pltpu.make_async_copy(src.at[e], dst, wsems.at[j])
                for j, (src, dst) in enumerate(((wg_hbm, wg_st), (wu_hbm, wu_st), (wd_hbm, wd_st)))]

    @pl.when(b < n_used)
    def _():
        @pl.when(b == 0)
        def _():
            for cp in weight_copies(be_ref[0]):
                cp.start()

        @pl.when(jnp.logical_or(b == 0, be_ref[b] != be_ref[jnp.maximum(b - 1, 0)]))
        def _():
            for cp in weight_copies(be_ref[b]):
                cp.wait()
            wg_bf[...] = wg_st[...].astype(BF16)
            wu_bf[...] = wu_st[...].astype(BF16)
            wd_bf[...] = wd_st[...].astype(BF16)
            nxt_e = nexte_ref[b]

            @pl.when(nxt_e >= 0)
            def _():
                for cp in weight_copies(nxt_e):
                    cp.start(priority=WEIGHT_DMA_PRIORITY)

        nxt = jnp.minimum(b + 1, n_used - 1)
        other = 1 - slot
        for r in range(rows):
            row_copy(nxt, other, r).start()
        wait_block(slot)

        h2 = (_rms_scale(xbuf[slot]) * gffn_ref[...]).astype(BF16)
        gate = jnp.dot(h2, wg_bf[...], preferred_element_type=F32)
        up = jnp.dot(h2, wu_bf[...], preferred_element_type=F32)
        hid = (jax.nn.silu(gate) * up).astype(BF16)
        ys_ref[...] = jnp.dot(hid, wd_bf[...], preferred_element_type=F32) * roww_ref[...]

        @pl.when(b == n_used - 1)
        def _():
            wait_block(other)

    @pl.when(b >= n_used)
    def _():
        ys_ref[...] = jnp.zeros_like(ys_ref)


def _expert_ffn(block_e, next_e, n_used, row_tok, x1, g_ffn, w_g, w_u, w_d, row_w):
    n_rows = row_tok.shape[0]
    n_blocks = n_rows // MOE_BLOCK_ROWS
    grid_spec = pltpu.PrefetchScalarGridSpec(
        num_scalar_prefetch=4,
        grid=(n_blocks,),
        in_specs=[
            pl.BlockSpec(memory_space=pl.ANY),
            pl.BlockSpec((1, D_MODEL), lambda b, *_: (0, 0)),
            pl.BlockSpec(memory_space=pl.ANY),
            pl.BlockSpec(memory_space=pl.ANY),
            pl.BlockSpec(memory_space=pl.ANY),
            pl.BlockSpec((MOE_BLOCK_ROWS, 1), lambda b, *_: (b, 0)),
        ],
        out_specs=pl.BlockSpec((MOE_BLOCK_ROWS, D_MODEL), lambda b, *_: (b, 0)),
        scratch_shapes=[
            pltpu.VMEM((2, MOE_BLOCK_ROWS, D_MODEL), F32),
            pltpu.SemaphoreType.DMA((2,)),
            pltpu.VMEM((D_MODEL, D_EXPERT), F32),
            pltpu.VMEM((D_MODEL, D_EXPERT), F32),
            pltpu.VMEM((D_EXPERT, D_MODEL), F32),
            pltpu.SemaphoreType.DMA((3,)),
            pltpu.VMEM((D_MODEL, D_EXPERT), BF16),
            pltpu.VMEM((D_MODEL, D_EXPERT), BF16),
            pltpu.VMEM((D_EXPERT, D_MODEL), BF16),
        ],
    )
    return pl.pallas_call(
        _expert_kernel,
        grid_spec=grid_spec,
        out_shape=jax.ShapeDtypeStruct((n_rows, D_MODEL), F32),
        compiler_params=pltpu.CompilerParams(dimension_semantics=("arbitrary",),
                                             vmem_limit_bytes=VMEM_LIMIT_BYTES),
        name="expert_ffn",
    )(block_e, next_e, n_used, row_tok, x1, g_ffn, w_g, w_u, w_d, row_w)


def _tail_kernel(dest_ref, x1_ref, ys_hbm, p_ref, gple_ref, wpg_ref, bpg_ref, wpu_ref, gfin_ref,
                 out_ref, ybuf, sems):
    tm = x1_ref.shape[0]
    i = pl.program_id(0)
    last = pl.num_programs(0) - 1
    slot = lax.rem(i, 2)

    def start_token(step, slt, r):
        for k in range(TOP_K):
            _row_gather_copy(ys_hbm, dest_ref[TOP_K * (step * tm + r) + k], ybuf.at[slt], k * tm + r,
                             sems.at[slt]).start()

    def wait_tile(slt):
        pltpu.make_async_copy(ys_hbm.at[pl.ds(0, TOP_K * tm), :], ybuf.at[slt], sems.at[slt]).wait()

    @pl.when(i == 0)
    def _():
        def body(r, carry):
            start_token(0, 0, r)
            return carry
        lax.fori_loop(0, tm, body, 0, unroll=GATHER_UNROLL)

    wait_tile(slot)
    nxt = jnp.minimum(i + 1, last)
    other = 1 - slot
    x2 = x1_ref[...] + ybuf[slot, :tm, :] + ybuf[slot, tm:, :]
    hn = (_rms_scale(x2) * gple_ref[...]).astype(BF16)
    up = jnp.dot(p_ref[...].astype(BF16), wpu_ref[...], preferred_element_type=F32)
    n_slabs = D_MODEL // MXU_COLS
    per = tm // n_slabs
    x3 = []
    for n in range(n_slabs):
        cols = slice(n * MXU_COLS, (n + 1) * MXU_COLS)
        for r in range(n * per, (n + 1) * per):
            start_token(nxt, other, r)
        gate = jax.nn.sigmoid(jnp.dot(hn, wpg_ref[:, cols], preferred_element_type=F32) + bpg_ref[:, cols])
        x3.append(x2[:, cols] + gate * up[:, cols])
    x3 = jnp.concatenate(x3, axis=1)
    out_ref[...] = _rms_scale(x3) * gfin_ref[...]

    @pl.when(i == last)
    def _():
        wait_tile(other)


def _tail(dest, x1, ys, p2d, g_ple, w_pg, b_pg, w_pu, g_final):
    n_tok = x1.shape[0]
    tm = TM_TAIL
    row = lambda i, *_: (i, 0)
    const = lambda i, *_: (0, 0)
    grid_spec = pltpu.PrefetchScalarGridSpec(
        num_scalar_prefetch=1,
        grid=(n_tok // tm,),
        in_specs=[
            pl.BlockSpec((tm, D_MODEL), row),
            pl.BlockSpec(memory_space=pl.ANY),
            pl.BlockSpec((tm, PLE_DIM), row),
            pl.BlockSpec((1, D_MODEL), const),
            pl.BlockSpec((D_MODEL, D_MODEL), const, pipeline_mode=pl.Buffered(1)),
            pl.BlockSpec((1, D_MODEL), const),
            pl.BlockSpec((PLE_DIM, D_MODEL), const),
            pl.BlockSpec((1, D_MODEL), const),
        ],
        out_specs=pl.BlockSpec((tm, D_MODEL), row),
        scratch_shapes=[pltpu.VMEM((2, TOP_K * tm, D_MODEL), F32), pltpu.SemaphoreType.DMA((2,))],
    )
    return pl.pallas_call(
        _tail_kernel,
        grid_spec=grid_spec,
        out_shape=jax.ShapeDtypeStruct((n_tok, D_MODEL), F32),
        compiler_params=pltpu.CompilerParams(dimension_semantics=("arbitrary",),
                                             vmem_limit_bytes=VMEM_LIMIT_BYTES),
        name="tail",
    )(dest, x1, ys, p2d, g_ple, w_pg, b_pg, w_pu, g_final)


def _dispatch_plan(expert_id, weights):
    n_tok = expert_id.shape[0]
    n_assign = n_tok * TOP_K
    n_blocks = -(-n_assign // MOE_BLOCK_ROWS) + N_EXPERTS
    n_rows = n_blocks * MOE_BLOCK_ROWS
    i32 = jnp.int32
    flat_e = expert_id.reshape(-1)
    flat_w = weights.reshape(-1)
    experts = jnp.arange(N_EXPERTS, dtype=i32)
    assign = jnp.arange(n_assign, dtype=i32)
    se, order = lax.sort((flat_e, assign), num_keys=1)
    onehot_sorted = se[:, None] == experts[None, :]
    counts = jnp.sum(onehot_sorted.astype(i32), axis=0)
    padded = (counts + MOE_BLOCK_ROWS - 1) // MOE_BLOCK_ROWS * MOE_BLOCK_ROWS
    pad_end = jnp.cumsum(padded)
    pad_start = pad_end - padded
    start = jnp.cumsum(counts) - counts
    row_of_sorted = assign + jnp.sum(jnp.where(onehot_sorted, (pad_start - start)[None, :], 0), axis=1)
    _, dest = lax.sort((order, row_of_sorted), num_keys=1)
    n_used = pad_end[-1] // MOE_BLOCK_ROWS
    rows = jnp.arange(n_rows, dtype=i32)
    row_e = jnp.minimum(jnp.sum((pad_end[None, :] <= rows[:, None]).astype(i32), axis=1), N_EXPERTS - 1)
    onehot_row = row_e[:, None] == experts[None, :]
    pick = lambda table: jnp.sum(jnp.where(onehot_row, table[None, :], 0), axis=1)
    offset = rows - pick(pad_start)
    valid = offset < pick(counts)
    src = order[jnp.clip(pick(start) + offset, 0, n_assign - 1)]
    row_tok = jnp.where(valid, src // TOP_K, 0)
    row_w = jnp.where(valid, flat_w[src], 0.0)
    block_e = row_e.reshape(n_blocks, MOE_BLOCK_ROWS)[:, 0]
    block_e = jnp.where(jnp.arange(n_blocks) < n_used, block_e, block_e[jnp.maximum(n_used - 1, 0)])
    later_used = jnp.logical_and(experts[None, :] > experts[:, None], (counts > 0)[None, :])
    next_used = jnp.min(jnp.where(later_used, experts[None, :], N_EXPERTS), axis=1)
    next_used = jnp.where(next_used < N_EXPERTS, next_used, -1)
    next_e = jnp.sum(jnp.where(block_e[:, None] == experts[None, :], next_used[None, :], 0), axis=1)
    return (block_e.astype(i32), next_e.astype(i32), n_used.astype(i32).reshape(1), row_tok.astype(i32),
            row_w.reshape(n_rows, 1), dest.astype(i32))


def kernel(x, p, g_mix, w_in, w_pool, pool_scale, w_branch_a, sgu_ln_g, sgu_ln_b, w_spatial, b_spatial, w_branch_b, w_merge_gate, b_merge_gate, w_out, g_ffn, w_router_group, b_router_group, w_router_expert, b_router_expert, w_exp_gate, w_exp_up, w_exp_down, g_ple, w_ple_gate, b_ple_gate, w_ple_up, g_final):
    bsz, seq, d = x.shape
    assert (seq, d) == (SEQ, D_MODEL) and g_mix.shape[0] == 1
    n_tok = bsz * seq
    x2d = x.reshape(n_tok, d)
    row2d = lambda v: v.reshape(1, -1)

    za, sb = _mixer_front(
        x2d, row2d(g_mix[0]), w_in[0].astype(BF16), w_pool[0].astype(BF16), row2d(pool_scale[0]),
        row2d(sgu_ln_g[0]), row2d(sgu_ln_b[0]), w_spatial[0], b_spatial[0][:, :, None])

    w_router = jnp.concatenate(
        [w_router_group[0], jnp.transpose(w_router_expert[0], (1, 0, 2)).reshape(d, N_EXPERTS)], axis=1)
    b_router = jnp.concatenate([b_router_group[0], b_router_expert[0].reshape(N_EXPERTS)])
    pad = LANES - w_router.shape[1]
    w_router = jnp.pad(w_router, ((0, 0), (0, pad)))
    b_router = jnp.pad(b_router, (0, pad))
    w_router_hi = w_router.astype(BF16)
    w_router_lo = (w_router - w_router_hi.astype(F32)).astype(BF16)
    w_router_split = jnp.concatenate([w_router_hi, w_router_lo], axis=1)

    x1, eid, wts = _mixer_back(
        x2d, za, sb, row2d(g_mix[0]), w_merge_gate[0].astype(BF16), row2d(b_merge_gate[0]),
        w_branch_a[0].astype(BF16), w_branch_b[0].astype(BF16), w_out[0].astype(BF16), row2d(g_ffn[0]),
        w_router_split, row2d(b_router))

    block_e, next_e, n_used, row_tok, row_w, dest = _dispatch_plan(eid[:, :TOP_K], wts[:, :TOP_K])
    ys = _expert_ffn(block_e, next_e, n_used, row_tok, x1, row2d(g_ffn[0]), w_exp_gate[0], w_exp_up[0],
                     w_exp_down[0], row_w)
    out = _tail(dest, x1, ys, p[0].reshape(n_tok, PLE_DIM), row2d(g_ple[0]), w_ple_gate[0].astype(BF16),
                row2d(b_ple_gate[0]), w_ple_up[0].astype(BF16), row2d(g_final))
    return out.reshape(bsz, seq, d)
```

```python
import functools

import jax
import jax.numpy as jnp
from jax import lax
from jax.experimental import pallas as pl
from jax.experimental.pallas import tpu as pltpu

F32 = jnp.float32
BF16 = jnp.bfloat16

D_MODEL = 2048
SEQ = 4096
EPS = 1e-6
PLE_DIM = 256
POOL_WINDOWS = (2, 4, 8, 16)
POOL_WIDTH = D_MODEL // 2
POOL_GROUP_DIM = POOL_WIDTH // len(POOL_WINDOWS)
POOL_HISTORY = max(POOL_WINDOWS)
SGU_BLOCK = 128
SGU_CHUNK = 64
SGU_GROUPS = 8
SGU_WIDTH = D_MODEL // 2
SGU_GROUP_DIM = SGU_WIDTH // SGU_GROUPS
N_IN = POOL_WIDTH + 2 * SGU_WIDTH
N_EXPERT_GROUPS = 4
EXPERTS_PER_GROUP = 8
N_EXPERTS = N_EXPERT_GROUPS * EXPERTS_PER_GROUP
TOP_K = 2
D_EXPERT = D_MODEL // 4
MOE_BLOCK_ROWS = 256

LANES = 128
MXU_COLS = 256
ROUTER_EXPERT_LANE0 = N_EXPERT_GROUPS
VMEM_LIMIT_BYTES = 56 * 1024 * 1024

TM_FRONT = 256
TM_BACK = 256
TM_TAIL = 256
MERGE_CHUNK = 512
SLAB_COLS = LANES
TOKEN_SLABS = D_MODEL // SLAB_COLS
WEIGHT_DMA_PRIORITY = 1
GATHER_UNROLL = 8


def _rms_scale(x):
    return x * lax.rsqrt(jnp.mean(x * x, axis=-1, keepdims=True) + EPS)


def _token_slab(ref, first_token, n_tokens, s):
    return ref.at[pl.ds(first_token * TOKEN_SLABS + s, n_tokens, stride=TOKEN_SLABS), :]


def _resident(shape):
    zeros = (0,) * len(shape)
    return pl.BlockSpec(shape, lambda *_: zeros, pipeline_mode=pl.Buffered(1))


def _mixer_front_kernel(x_ref, gmix_ref, win_ref, wpool_ref, pscale_ref, lng_ref, lnb_ref, ws_ref,
                        bsp_ref, za_ref, sb_ref, hist_ref):
    tm = x_ref.shape[0]
    tiles_per_seq = SEQ // tm
    seq_tile = lax.rem(pl.program_id(0), tiles_per_seq)

    @pl.when(seq_tile == 0)
    def _():
        hist_ref[...] = jnp.zeros_like(hist_ref)

    h = (_rms_scale(x_ref[...]) * gmix_ref[...]).astype(BF16)
    z = jnp.dot(h, win_ref[...], preferred_element_type=F32)

    a = z[:, :POOL_WIDTH]
    ext = jnp.concatenate([hist_ref[...], a], axis=0)
    hist_ref[...] = a[tm - POOL_HISTORY:, :]
    frames = (seq_tile * tm + 1 + lax.broadcasted_iota(jnp.int32, (tm, 1), 0)).astype(F32)
    for gi, w in enumerate(POOL_WINDOWS):
        cols = slice(gi * POOL_GROUP_DIM, (gi + 1) * POOL_GROUP_DIM)
        s = ext[:, cols]
        k = 1
        while k < w:
            s = s + pltpu.roll(s, k, 0)
            k *= 2
        wsum = s[POOL_HISTORY:, :]
        zg = wsum / jnp.minimum(frames, float(w)) - a[:, cols]
        yg = jnp.dot(zg.astype(BF16), wpool_ref[gi], preferred_element_type=F32)
        za_ref[:, cols] = (yg * pscale_ref[:, cols]).astype(BF16)

    u = jax.nn.gelu(z[:, POOL_WIDTH:POOL_WIDTH + SGU_WIDTH])
    v = jax.nn.gelu(z[:, POOL_WIDTH + SGU_WIDTH:])
    vc = v - jnp.mean(v, axis=-1, keepdims=True)
    var = jnp.mean(vc * vc, axis=-1, keepdims=True)
    vn = (vc * lax.rsqrt(var + EPS) * lng_ref[...] + lnb_ref[...]).astype(BF16)
    t_chunk = lax.broadcasted_iota(jnp.int32, (SGU_BLOCK, SGU_BLOCK), 0) // SGU_CHUNK
    s_chunk = lax.broadcasted_iota(jnp.int32, (SGU_BLOCK, SGU_BLOCK), 1) // SGU_CHUNK
    causal = s_chunk <= t_chunk
    nblk = tm // SGU_BLOCK
    for g in range(SGU_GROUPS):
        cols = slice(g * SGU_GROUP_DIM, (g + 1) * SGU_GROUP_DIM)
        wsg = jnp.where(causal, ws_ref[g], 0.0).astype(BF16)
        vg = jnp.concatenate([vn[j * SGU_BLOCK:(j + 1) * SGU_BLOCK, cols] for j in range(nblk)], axis=1)
        vm = jnp.dot(wsg, vg, preferred_element_type=F32) + bsp_ref[g]
        for j in range(nblk):
            rows = slice(j * SGU_BLOCK, (j + 1) * SGU_BLOCK)
            sb_ref[rows, cols] = (u[rows, cols] * vm[:, j * SGU_GROUP_DIM:(j + 1) * SGU_GROUP_DIM]).astype(BF16)


def _mixer_front(x2d, g_mix, w_in, w_pool, pool_scale, ln_g, ln_b, w_spatial, b_spatial):
    n_tok = x2d.shape[0]
    tm = TM_FRONT
    row = lambda i: (i, 0)
    return pl.pallas_call(
        _mixer_front_kernel,
        grid=(n_tok // tm,),
        in_specs=[
            pl.BlockSpec((tm, D_MODEL), row),
            _resident((1, D_MODEL)),
            _resident((D_MODEL, N_IN)),
            _resident((len(POOL_WINDOWS), POOL_GROUP_DIM, POOL_GROUP_DIM)),
            _resident((1, POOL_WIDTH)),
            _resident((1, SGU_WIDTH)),
            _resident((1, SGU_WIDTH)),
            _resident((SGU_GROUPS, SGU_BLOCK, SGU_BLOCK)),
            _resident((SGU_GROUPS, SGU_BLOCK, 1)),
        ],
        out_specs=[pl.BlockSpec((tm, POOL_WIDTH), row), pl.BlockSpec((tm, SGU_WIDTH), row)],
        out_shape=[jax.ShapeDtypeStruct((n_tok, POOL_WIDTH), BF16),
                   jax.ShapeDtypeStruct((n_tok, SGU_WIDTH), BF16)],
        scratch_shapes=[pltpu.VMEM((POOL_HISTORY, POOL_WIDTH), F32)],
        compiler_params=pltpu.CompilerParams(dimension_semantics=("arbitrary",),
                                             vmem_limit_bytes=VMEM_LIMIT_BYTES),
        name="mixer_front",
    )(x2d, g_mix, w_in, w_pool, pool_scale, ln_g, ln_b, w_spatial, b_spatial)


def _route(logits):
    lane = lax.broadcasted_iota(jnp.int32, logits.shape, 1).astype(F32)
    neg = -jnp.inf
    far = float(LANES)

    def first_argmax(vals):
        top = jnp.max(vals, axis=-1, keepdims=True)
        return top, jnp.min(jnp.where(vals == top, lane, far), axis=-1, keepdims=True)

    is_grp = lane < float(N_EXPERT_GROUPS)
    g_top, g_idx = first_argmax(jnp.where(is_grp, logits, neg))
    g_den = jnp.sum(jnp.where(is_grp, jnp.exp(logits - g_top), 0.0), axis=-1, keepdims=True)
    grp_p = 1.0 / g_den
    lo = float(ROUTER_EXPERT_LANE0) + g_idx * float(EXPERTS_PER_GROUP)
    e_log = jnp.where(lane >= lo, jnp.where(lane < lo + float(EXPERTS_PER_GROUP), logits, neg), neg)
    t1, i1 = first_argmax(e_log)
    t2, i2 = first_argmax(jnp.where(lane == i1, neg, e_log))
    r = jnp.exp(t2 - t1)
    w1 = grp_p / (1.0 + r)
    w2 = grp_p * r / (1.0 + r)
    e1 = i1 - float(ROUTER_EXPERT_LANE0)
    e2 = i2 - float(ROUTER_EXPERT_LANE0)
    eid = jnp.where(lane == 0.0, e1, jnp.where(lane == 1.0, e2, 0.0)).astype(jnp.int32)
    wts = jnp.where(lane == 0.0, w1, jnp.where(lane == 1.0, w2, 0.0))
    return eid, wts


def _mixer_back_kernel(x_ref, za_ref, sb_ref, gmix_ref, wm_ref, bm_ref, wa_ref, wb_ref, wo_ref,
                       gffn_ref, wr_ref, br_ref, x1_ref, eid_ref, wts_ref):
    x = x_ref[...]
    h = (_rms_scale(x) * gmix_ref[...]).astype(BF16)
    za = za_ref[...]
    sb = sb_ref[...]
    acc = jnp.zeros(x.shape, F32)
    for c in range(D_MODEL // MERGE_CHUNK):
        ca = slice(c * MERGE_CHUNK, (c + 1) * MERGE_CHUNK)
        cb = slice(D_MODEL + c * MERGE_CHUNK, D_MODEL + (c + 1) * MERGE_CHUNK)
        ga = jax.nn.sigmoid(jnp.dot(h, wm_ref[:, ca], preferred_element_type=F32) + bm_ref[:, ca])
        gb = jax.nn.sigmoid(jnp.dot(h, wm_ref[:, cb], preferred_element_type=F32) + bm_ref[:, cb])
        ya = jnp.dot(za, wa_ref[:, ca], preferred_element_type=F32)
        yb = jnp.dot(sb, wb_ref[:, ca], preferred_element_type=F32)
        merged = (ga * ya + gb * yb).astype(BF16)
        acc = acc + jnp.dot(merged, wo_ref[ca, :], preferred_element_type=F32)
    x1 = x + acc
    for s in range(TOKEN_SLABS):
        _token_slab(x1_ref, 0, x.shape[0], s)[...] = x1[:, s * SLAB_COLS:(s + 1) * SLAB_COLS]
    h2 = _rms_scale(x1) * gffn_ref[...]
    h2_hi = h2.astype(BF16)
    h2_lo = (h2 - h2_hi.astype(F32)).astype(BF16)
    hi_terms = jnp.dot(h2_hi, wr_ref[...], preferred_element_type=F32)
    lo_term = jnp.dot(h2_lo, wr_ref[:, :LANES], preferred_element_type=F32)
    logits = hi_terms[:, :LANES] + hi_terms[:, LANES:] + lo_term + br_ref[...]
    eid, wts = _route(logits)
    eid_ref[...] = eid
    wts_ref[...] = wts


def _mixer_back(x2d, za, sb, g_mix, w_merge, b_merge, w_a, w_b, w_out, g_ffn, w_router, b_router):
    n_tok = x2d.shape[0]
    tm = TM_BACK
    row = lambda i: (i, 0)
    return pl.pallas_call(
        _mixer_back_kernel,
        grid=(n_tok // tm,),
        in_specs=[
            pl.BlockSpec((tm, D_MODEL), row),
            pl.BlockSpec((tm, POOL_WIDTH), row),
            pl.BlockSpec((tm, SGU_WIDTH), row),
            _resident((1, D_MODEL)),
            _resident((D_MODEL, 2 * D_MODEL)),
            _resident((1, 2 * D_MODEL)),
            _resident((POOL_WIDTH, D_MODEL)),
            _resident((SGU_WIDTH, D_MODEL)),
            _resident((D_MODEL, D_MODEL)),
            _resident((1, D_MODEL)),
            _resident((D_MODEL, 2 * LANES)),
            _resident((1, LANES)),
        ],
        out_specs=[pl.BlockSpec((tm * TOKEN_SLABS, SLAB_COLS), row), pl.BlockSpec((tm, LANES), row),
                   pl.BlockSpec((tm, LANES), row)],
        out_shape=[jax.ShapeDtypeStruct((n_tok * TOKEN_SLABS, SLAB_COLS), F32),
                   jax.ShapeDtypeStruct((n_tok, LANES), jnp.int32),
                   jax.ShapeDtypeStruct((n_tok, LANES), F32)],
        compiler_params=pltpu.CompilerParams(dimension_semantics=("arbitrary",),
                                             vmem_limit_bytes=VMEM_LIMIT_BYTES),
        name="mixer_back",
    )(x2d, za, sb, g_mix, w_merge, b_merge, w_a, w_b, w_out, g_ffn, w_router, b_router)


def _token_copy(src_hbm, src_token, dst_vmem, dst_token, sem):
    src_row = pl.multiple_of(src_token * TOKEN_SLABS, TOKEN_SLABS)
    return pltpu.make_async_copy(src_hbm.at[pl.ds(src_row, TOKEN_SLABS), :],
                                 dst_vmem.at[pl.ds(dst_token * TOKEN_SLABS, TOKEN_SLABS), :], sem)


def _tokens_wait(src_hbm, dst_vmem, sem):
    pltpu.make_async_copy(src_hbm.at[pl.ds(0, dst_vmem.shape[0]), :], dst_vmem, sem).wait()


def _expert_kernel(be_ref, nexte_ref, nused_ref, rowtok_ref, x1_hbm, gffn_ref, wg_hbm, wu_hbm, wd_hbm,
                   roww_ref, ys_ref, xbuf, sems, wg_st, wu_st, wd_st, wsems, wg_bf, wu_bf, wd_bf):
    rows = MOE_BLOCK_ROWS
    b = pl.program_id(0)
    n_used = nused_ref[0]
    slot = lax.rem(b, 2)

    def row_copy(blk, slt, r):
        return _token_copy(x1_hbm, rowtok_ref[blk * rows + r], xbuf.at[slt], r, sems.at[slt])

    def wait_block(slt):
        _tokens_wait(x1_hbm, xbuf.at[slt], sems.at[slt])

    @pl.when(b == 0)
    def _():
        def body(r, carry):
            row_copy(0, 0, r).start()
            return carry
        lax.fori_loop(0, rows, body, 0, unroll=GATHER_UNROLL)

    def weight_copies(e):
        return [pltpu.make_async_copy(src.at[e], dst, wsems.at[j])
                for j, (src, dst) in enumerate(((wg_hbm, wg_st), (wu_hbm, wu_st), (wd_hbm, wd_st)))]

    @pl.when(b < n_used)
    def _():
        @pl.when(b == 0)
        def _():
            for cp in weight_copies(be_ref[0]):
                cp.start()

        @pl.when(jnp.logical_or(b == 0, be_ref[b] != be_ref[jnp.maximum(b - 1, 0)]))
        def _():
            for cp in weight_copies(be_ref[b]):
                cp.wait()
            wg_bf[...] = wg_st[...].astype(BF16)
            wu_bf[...] = wu_st[...].astype(BF16)
            wd_bf[...] = wd_st[...].astype(BF16)
            nxt_e = nexte_ref[b]

            @pl.when(nxt_e >= 0)
            def _():
                for cp in weight_copies(nxt_e):
                    cp.start(priority=WEIGHT_DMA_PRIORITY)

        nxt = jnp.minimum(b + 1, n_used - 1)
        other = 1 - slot
        for r in range(rows):
            row_copy(nxt, other, r).start()
        wait_block(slot)

        xs = [_token_slab(xbuf.at[slot], 0, rows, s)[...] for s in range(TOKEN_SLABS)]
        scale = lax.rsqrt(sum(jnp.sum(v * v, axis=-1, keepdims=True) for v in xs) / D_MODEL + EPS)
        h2 = jnp.concatenate(
            [(v * scale * gffn_ref[:, s * SLAB_COLS:(s + 1) * SLAB_COLS]).astype(BF16) for s, v in enumerate(xs)],
            axis=1)
        gate = jnp.dot(h2, wg_bf[...], preferred_element_type=F32)
        up = jnp.dot(h2, wu_bf[...], preferred_element_type=F32)
        hid = (jax.nn.silu(gate) * up).astype(BF16)
        y = jnp.dot(hid, wd_bf[...], preferred_element_type=F32) * roww_ref[...]
        for s in range(TOKEN_SLABS):
            _token_slab(ys_ref, 0, rows, s)[...] = y[:, s * SLAB_COLS:(s + 1) * SLAB_COLS]

        @pl.when(b == n_used - 1)
        def _():
            wait_block(other)

    @pl.when(b >= n_used)
    def _():
        ys_ref[...] = jnp.zeros_like(ys_ref)


def _expert_ffn(block_e, next_e, n_used, row_tok, x1, g_ffn, w_g, w_u, w_d, row_w):
    n_rows = row_tok.shape[0]
    n_blocks = n_rows // MOE_BLOCK_ROWS
    grid_spec = pltpu.PrefetchScalarGridSpec(
        num_scalar_prefetch=4,
        grid=(n_blocks,),
        in_specs=[
            pl.BlockSpec(memory_space=pl.ANY),
            pl.BlockSpec((1, D_MODEL), lambda b, *_: (0, 0)),
            pl.BlockSpec(memory_space=pl.ANY),
            pl.BlockSpec(memory_space=pl.ANY),
            pl.BlockSpec(memory_space=pl.ANY),
            pl.BlockSpec((MOE_BLOCK_ROWS, 1), lambda b, *_: (b, 0)),
        ],
        out_specs=pl.BlockSpec((MOE_BLOCK_ROWS * TOKEN_SLABS, SLAB_COLS), lambda b, *_: (b, 0)),
        scratch_shapes=[
            pltpu.VMEM((2, MOE_BLOCK_ROWS * TOKEN_SLABS, SLAB_COLS), F32),
            pltpu.SemaphoreType.DMA((2,)),
            pltpu.VMEM((D_MODEL, D_EXPERT), F32),
            pltpu.VMEM((D_MODEL, D_EXPERT), F32),
            pltpu.VMEM((D_EXPERT, D_MODEL), F32),
            pltpu.SemaphoreType.DMA((3,)),
            pltpu.VMEM((D_MODEL, D_EXPERT), BF16),
            pltpu.VMEM((D_MODEL, D_EXPERT), BF16),
            pltpu.VMEM((D_EXPERT, D_MODEL), BF16),
        ],
    )
    return pl.pallas_call(
        _expert_kernel,
        grid_spec=grid_spec,
        out_shape=jax.ShapeDtypeStruct((n_rows * TOKEN_SLABS, SLAB_COLS), F32),
        compiler_params=pltpu.CompilerParams(dimension_semantics=("arbitrary",),
                                             vmem_limit_bytes=VMEM_LIMIT_BYTES),
        name="expert_ffn",
    )(block_e, next_e, n_used, row_tok, x1, g_ffn, w_g, w_u, w_d, row_w)


def _tail_kernel(dest_ref, x1_ref, ys_hbm, p_ref, gple_ref, wpg_ref, bpg_ref, wpu_ref, gfin_ref,
                 out_ref, ybuf, sems):
    tm = out_ref.shape[0]
    i = pl.program_id(0)
    last = pl.num_programs(0) - 1
    slot = lax.rem(i, 2)

    def start_token(step, slt, r):
        for k in range(TOP_K):
            _token_copy(ys_hbm, dest_ref[TOP_K * (step * tm + r) + k], ybuf.at[slt], k * tm + r,
                        sems.at[slt]).start()

    def wait_tile(slt):
        _tokens_wait(ys_hbm, ybuf.at[slt], sems.at[slt])

    @pl.when(i == 0)
    def _():
        def body(r, carry):
            start_token(0, 0, r)
            return carry
        lax.fori_loop(0, tm, body, 0, unroll=GATHER_UNROLL)

    wait_tile(slot)
    nxt = jnp.minimum(i + 1, last)
    other = 1 - slot
    yb = ybuf.at[slot]
    x2 = [_token_slab(x1_ref, 0, tm, s)[...] + _token_slab(yb, 0, tm, s)[...] + _token_slab(yb, tm, tm, s)[...]
          for s in range(TOKEN_SLABS)]
    scale = lax.rsqrt(sum(jnp.sum(v * v, axis=-1, keepdims=True) for v in x2) / D_MODEL + EPS)
    hn = jnp.concatenate(
        [(v * scale * gple_ref[:, s * SLAB_COLS:(s + 1) * SLAB_COLS]).astype(BF16) for s, v in enumerate(x2)],
        axis=1)
    up = jnp.dot(p_ref[...].astype(BF16), wpu_ref[...], preferred_element_type=F32)
    n_slabs = D_MODEL // MXU_COLS
    per = tm // n_slabs
    x2 = jnp.concatenate(x2, axis=1)
    x3 = []
    for n in range(n_slabs):
        cols = slice(n * MXU_COLS, (n + 1) * MXU_COLS)
        for r in range(n * per, (n + 1) * per):
            start_token(nxt, other, r)
        gate = jax.nn.sigmoid(jnp.dot(hn, wpg_ref[:, cols], preferred_element_type=F32) + bpg_ref[:, cols])
        x3.append(x2[:, cols] + gate * up[:, cols])
    x3 = jnp.concatenate(x3, axis=1)
    out_ref[...] = _rms_scale(x3) * gfin_ref[...]

    @pl.when(i == last)
    def _():
        wait_tile(other)


def _tail(dest, x1, ys, p2d, g_ple, w_pg, b_pg, w_pu, g_final):
    n_tok = p2d.shape[0]
    tm = TM_TAIL
    row = lambda i, *_: (i, 0)
    const = lambda i, *_: (0, 0)
    grid_spec = pltpu.PrefetchScalarGridSpec(
        num_scalar_prefetch=1,
        grid=(n_tok // tm,),
        in_specs=[
            pl.BlockSpec((tm * TOKEN_SLABS, SLAB_COLS), row),
            pl.BlockSpec(memory_space=pl.ANY),
            pl.BlockSpec((tm, PLE_DIM), row),
            pl.BlockSpec((1, D_MODEL), const),
            pl.BlockSpec((D_MODEL, D_MODEL), const, pipeline_mode=pl.Buffered(1)),
            pl.BlockSpec((1, D_MODEL), const),
            pl.BlockSpec((PLE_DIM, D_MODEL), const),
            pl.BlockSpec((1, D_MODEL), const),
        ],
        out_specs=pl.BlockSpec((tm, D_MODEL), row),
        scratch_shapes=[pltpu.VMEM((2, TOP_K * tm * TOKEN_SLABS, SLAB_COLS), F32),
                        pltpu.SemaphoreType.DMA((2,))],
    )
    return pl.pallas_call(
        _tail_kernel,
        grid_spec=grid_spec,
        out_shape=jax.ShapeDtypeStruct((n_tok, D_MODEL), F32),
        compiler_params=pltpu.CompilerParams(dimension_semantics=("arbitrary",),
                                             vmem_limit_bytes=VMEM_LIMIT_BYTES),
        name="tail",
    )(dest, x1, ys, p2d, g_ple, w_pg, b_pg, w_pu, g_final)


def _dispatch_plan(expert_id, weights):
    n_tok = expert_id.shape[0]
    n_assign = n_tok * TOP_K
    n_blocks = -(-n_assign // MOE_BLOCK_ROWS) + N_EXPERTS
    n_rows = n_blocks * MOE_BLOCK_ROWS
    i32 = jnp.int32
    flat_e = expert_id.reshape(-1)
    flat_w = weights.reshape(-1)
    experts = jnp.arange(N_EXPERTS, dtype=i32)
    assign = jnp.arange(n_assign, dtype=i32)
    se, order = lax.sort((flat_e, assign), num_keys=1)
    onehot_sorted = se[:, None] == experts[None, :]
    counts = jnp.sum(onehot_sorted.astype(i32), axis=0)
    padded = (counts + MOE_BLOCK_ROWS - 1) // MOE_BLOCK_ROWS * MOE_BLOCK_ROWS
    pad_end = jnp.cumsum(padded)
    pad_start = pad_end - padded
    start = jnp.cumsum(counts) - counts
    row_of_sorted = assign + jnp.sum(jnp.where(onehot_sorted, (pad_start - start)[None, :], 0), axis=1)
    _, dest = lax.sort((order, row_of_sorted), num_keys=1)
    n_used = pad_end[-1] // MOE_BLOCK_ROWS
    rows = jnp.arange(n_rows, dtype=i32)
    row_e = jnp.minimum(jnp.sum((pad_end[None, :] <= rows[:, None]).astype(i32), axis=1), N_EXPERTS - 1)
    onehot_row = row_e[:, None] == experts[None, :]
    pick = lambda table: jnp.sum(jnp.where(onehot_row, table[None, :], 0), axis=1)
    offset = rows - pick(pad_start)
    valid = offset < pick(counts)
    src = order[jnp.clip(pick(start) + offset, 0, n_assign - 1)]
    row_tok = jnp.where(valid, src // TOP_K, 0)
    row_w = jnp.where(valid, flat_w[src], 0.0)
    block_e = row_e.reshape(n_blocks, MOE_BLOCK_ROWS)[:, 0]
    block_e = jnp.where(jnp.arange(n_blocks) < n_used, block_e, block_e[jnp.maximum(n_used - 1, 0)])
    later_used = jnp.logical_and(experts[None, :] > experts[:, None], (counts > 0)[None, :])
    next_used = jnp.min(jnp.where(later_used, experts[None, :], N_EXPERTS), axis=1)
    next_used = jnp.where(next_used < N_EXPERTS, next_used, -1)
    next_e = jnp.sum(jnp.where(block_e[:, None] == experts[None, :], next_used[None, :], 0), axis=1)
    return (block_e.astype(i32), next_e.astype(i32), n_used.astype(i32).reshape(1), row_tok.astype(i32),
            row_w.reshape(n_rows, 1), dest.astype(i32))


def kernel(x, p, g_mix, w_in, w_pool, pool_scale, w_branch_a, sgu_ln_g, sgu_ln_b, w_spatial, b_spatial, w_branch_b, w_merge_gate, b_merge_gate, w_out, g_ffn, w_router_group, b_router_group, w_router_expert, b_router_expert, w_exp_gate, w_exp_up, w_exp_down, g_ple, w_ple_gate, b_ple_gate, w_ple_up, g_final):
    bsz, seq, d = x.shape
    assert (seq, d) == (SEQ, D_MODEL) and g_mix.shape[0] == 1
    n_tok = bsz * seq
    x2d = x.reshape(n_tok, d)
    row2d = lambda v: v.reshape(1, -1)

    za, sb = _mixer_front(
        x2d, row2d(g_mix[0]), w_in[0].astype(BF16), w_pool[0].astype(BF16), row2d(pool_scale[0]),
        row2d(sgu_ln_g[0]), row2d(sgu_ln_b[0]), w_spatial[0], b_spatial[0][:, :, None])

    w_router = jnp.concatenate(
        [w_router_group[0], jnp.transpose(w_router_expert[0], (1, 0, 2)).reshape(d, N_EXPERTS)], axis=1)
    b_router = jnp.concatenate([b_router_group[0], b_router_expert[0].reshape(N_EXPERTS)])
    pad = LANES - w_router.shape[1]
    w_router = jnp.pad(w_router, ((0, 0), (0, pad)))
    b_router = jnp.pad(b_router, (0, pad))
    w_router_hi = w_router.astype(BF16)
    w_router_lo = (w_router - w_router_hi.astype(F32)).astype(BF16)
    w_router_split = jnp.concatenate([w_router_hi, w_router_lo], axis=1)

    x1, eid, wts = _mixer_back(
        x2d, za, sb, row2d(g_mix[0]), w_merge_gate[0].astype(BF16), row2d(b_merge_gate[0]),
        w_branch_a[0].astype(BF16), w_branch_b[0].astype(BF16), w_out[0].astype(BF16), row2d(g_ffn[0]),
        w_router_split, row2d(b_router))

    block_e, next_e, n_used, row_tok, row_w, dest = _dispatch_plan(eid[:, :TOP_K], wts[:, :TOP_K])
    ys = _expert_ffn(block_e, next_e, n_used, row_tok, x1, row2d(g_ffn[0]), w_exp_gate[0], w_exp_up[0],
                     w_exp_down[0], row_w)
    out = _tail(dest, x1, ys, p[0].reshape(n_tok, PLE_DIM), row2d(g_ple[0]), w_ple_gate[0].astype(BF16),
                row2d(b_ple_gate[0]), w_ple_up[0].astype(BF16), row2d(g_final))
    return out.reshape(bsz, seq, d)
```

```python
import jax
import jax.numpy as jnp
from jax import lax
from jax.experimental import pallas as pl
from jax.experimental.pallas import tpu as pltpu

F32 = jnp.float32
BF16 = jnp.bfloat16

D_MODEL = 2048
SEQ = 4096
EPS = 1e-6
PLE_DIM = 256
POOL_WINDOWS = (2, 4, 8, 16)
POOL_WIDTH = D_MODEL // 2
POOL_GROUP_DIM = POOL_WIDTH // len(POOL_WINDOWS)
POOL_HISTORY = max(POOL_WINDOWS)
SGU_BLOCK = 128
SGU_CHUNK = 64
SGU_GROUPS = 8
SGU_WIDTH = D_MODEL // 2
SGU_GROUP_DIM = SGU_WIDTH // SGU_GROUPS
N_IN = POOL_WIDTH + 2 * SGU_WIDTH
N_EXPERT_GROUPS = 4
EXPERTS_PER_GROUP = 8
N_EXPERTS = N_EXPERT_GROUPS * EXPERTS_PER_GROUP
TOP_K = 2
D_EXPERT = D_MODEL // 4
MOE_BLOCK_ROWS = 256

LANES = 128
MXU_COLS = 256
ROUTER_EXPERT_LANE0 = N_EXPERT_GROUPS
VMEM_LIMIT_BYTES = 56 * 1024 * 1024

TM_FRONT = 256
TM_BACK = 256
TM_TAIL = 256
MERGE_CHUNK = 512
WEIGHT_DMA_PRIORITY = 1
GATHER_UNROLL = 8


def _rms_scale(x):
    return x * lax.rsqrt(jnp.mean(x * x, axis=-1, keepdims=True) + EPS)


def _resident(shape):
    zeros = (0,) * len(shape)
    return pl.BlockSpec(shape, lambda *_: zeros, pipeline_mode=pl.Buffered(1))


def _mixer_front_kernel(x_ref, gmix_ref, win_ref, wpool_ref, pscale_ref, lng_ref, lnb_ref, ws_ref,
                        bsp_ref, za_ref, sb_ref, hist_ref):
    tm = x_ref.shape[0]
    tiles_per_seq = SEQ // tm
    seq_tile = lax.rem(pl.program_id(0), tiles_per_seq)

    @pl.when(seq_tile == 0)
    def _():
        hist_ref[...] = jnp.zeros_like(hist_ref)

    h = (_rms_scale(x_ref[...]) * gmix_ref[...]).astype(BF16)
    z = jnp.dot(h, win_ref[...], preferred_element_type=F32)

    a = z[:, :POOL_WIDTH]
    ext = jnp.concatenate([hist_ref[...], a], axis=0)
    hist_ref[...] = a[tm - POOL_HISTORY:, :]
    frames = (seq_tile * tm + 1 + lax.broadcasted_iota(jnp.int32, (tm, 1), 0)).astype(F32)
    for gi, w in enumerate(POOL_WINDOWS):
        cols = slice(gi * POOL_GROUP_DIM, (gi + 1) * POOL_GROUP_DIM)
        s = ext[:, cols]
        k = 1
        while k < w:
            s = s + pltpu.roll(s, k, 0)
            k *= 2
        wsum = s[POOL_HISTORY:, :]
        zg = wsum / jnp.minimum(frames, float(w)) - a[:, cols]
        yg = jnp.dot(zg.astype(BF16), wpool_ref[gi], preferred_element_type=F32)
        za_ref[:, cols] = (yg * pscale_ref[:, cols]).astype(BF16)

    u = jax.nn.gelu(z[:, POOL_WIDTH:POOL_WIDTH + SGU_WIDTH])
    v = jax.nn.gelu(z[:, POOL_WIDTH + SGU_WIDTH:])
    vc = v - jnp.mean(v, axis=-1, keepdims=True)
    var = jnp.mean(vc * vc, axis=-1, keepdims=True)
    vn = (vc * lax.rsqrt(var + EPS) * lng_ref[...] + lnb_ref[...]).astype(BF16)
    t_chunk = lax.broadcasted_iota(jnp.int32, (SGU_BLOCK, SGU_BLOCK), 0) // SGU_CHUNK
    s_chunk = lax.broadcasted_iota(jnp.int32, (SGU_BLOCK, SGU_BLOCK), 1) // SGU_CHUNK
    causal = s_chunk <= t_chunk
    nblk = tm // SGU_BLOCK
    for g in range(SGU_GROUPS):
        cols = slice(g * SGU_GROUP_DIM, (g + 1) * SGU_GROUP_DIM)
        wsg = jnp.where(causal, ws_ref[g], 0.0).astype(BF16)
        vg = jnp.concatenate([vn[j * SGU_BLOCK:(j + 1) * SGU_BLOCK, cols] for j in range(nblk)], axis=1)
        vm = jnp.dot(wsg, vg, preferred_element_type=F32) + bsp_ref[g]
        for j in range(nblk):
            rows = slice(j * SGU_BLOCK, (j + 1) * SGU_BLOCK)
            sb_ref[rows, cols] = (u[rows, cols] * vm[:, j * SGU_GROUP_DIM:(j + 1) * SGU_GROUP_DIM]).astype(BF16)


def _mixer_front(x2d, g_mix, w_in, w_pool, pool_scale, ln_g, ln_b, w_spatial, b_spatial):
    n_tok = x2d.shape[0]
    tm = TM_FRONT
    row = lambda i: (i, 0)
    return pl.pallas_call(
        _mixer_front_kernel,
        grid=(n_tok // tm,),
        in_specs=[
            pl.BlockSpec((tm, D_MODEL), row),
            _resident((1, D_MODEL)),
            _resident((D_MODEL, N_IN)),
            _resident((len(POOL_WINDOWS), POOL_GROUP_DIM, POOL_GROUP_DIM)),
            _resident((1, POOL_WIDTH)),
            _resident((1, SGU_WIDTH)),
            _resident((1, SGU_WIDTH)),
            _resident((SGU_GROUPS, SGU_BLOCK, SGU_BLOCK)),
            _resident((SGU_GROUPS, SGU_BLOCK, 1)),
        ],
        out_specs=[pl.BlockSpec((tm, POOL_WIDTH), row), pl.BlockSpec((tm, SGU_WIDTH), row)],
        out_shape=[jax.ShapeDtypeStruct((n_tok, POOL_WIDTH), BF16),
                   jax.ShapeDtypeStruct((n_tok, SGU_WIDTH), BF16)],
        scratch_shapes=[pltpu.VMEM((POOL_HISTORY, POOL_WIDTH), F32)],
        compiler_params=pltpu.CompilerParams(dimension_semantics=("arbitrary",),
                                             vmem_limit_bytes=VMEM_LIMIT_BYTES),
        name="mixer_front",
    )(x2d, g_mix, w_in, w_pool, pool_scale, ln_g, ln_b, w_spatial, b_spatial)


def _route(logits):
    lane = lax.broadcasted_iota(jnp.int32, logits.shape, 1).astype(F32)
    neg = -jnp.inf
    far = float(LANES)

    def first_argmax(vals):
        top = jnp.max(vals, axis=-1, keepdims=True)
        return top, jnp.min(jnp.where(vals == top, lane, far), axis=-1, keepdims=True)

    is_grp = lane < float(N_EXPERT_GROUPS)
    g_top, g_idx = first_argmax(jnp.where(is_grp, logits, neg))
    g_den = jnp.sum(jnp.where(is_grp, jnp.exp(logits - g_top), 0.0), axis=-1, keepdims=True)
    grp_p = 1.0 / g_den
    lo = float(ROUTER_EXPERT_LANE0) + g_idx * float(EXPERTS_PER_GROUP)
    e_log = jnp.where(lane >= lo, jnp.where(lane < lo + float(EXPERTS_PER_GROUP), logits, neg), neg)
    t1, i1 = first_argmax(e_log)
    t2, i2 = first_argmax(jnp.where(lane == i1, neg, e_log))
    r = jnp.exp(t2 - t1)
    w1 = grp_p / (1.0 + r)
    w2 = grp_p * r / (1.0 + r)
    e1 = i1 - float(ROUTER_EXPERT_LANE0)
    e2 = i2 - float(ROUTER_EXPERT_LANE0)
    eid = jnp.where(lane == 0.0, e1, jnp.where(lane == 1.0, e2, 0.0)).astype(jnp.int32)
    wts = jnp.where(lane == 0.0, w1, jnp.where(lane == 1.0, w2, 0.0))
    return eid, wts


def _mixer_back_kernel(x_ref, za_ref, sb_ref, gmix_ref, wm_ref, bm_ref, wa_ref, wb_ref, wo_ref,
                       gffn_ref, wr_ref, br_ref, x1_ref, eid_ref, wts_ref):
    x = x_ref[...]
    h = (_rms_scale(x) * gmix_ref[...]).astype(BF16)
    za = za_ref[...]
    sb = sb_ref[...]
    acc = jnp.zeros(x.shape, F32)
    for c in range(D_MODEL // MERGE_CHUNK):
        ca = slice(c * MERGE_CHUNK, (c + 1) * MERGE_CHUNK)
        cb = slice(D_MODEL + c * MERGE_CHUNK, D_MODEL + (c + 1) * MERGE_CHUNK)
        ga = jax.nn.sigmoid(jnp.dot(h, wm_ref[:, ca], preferred_element_type=F32) + bm_ref[:, ca])
        gb = jax.nn.sigmoid(jnp.dot(h, wm_ref[:, cb], preferred_element_type=F32) + bm_ref[:, cb])
        ya = jnp.dot(za, wa_ref[:, ca], preferred_element_type=F32)
        yb = jnp.dot(sb, wb_ref[:, ca], preferred_element_type=F32)
        merged = (ga * ya + gb * yb).astype(BF16)
        acc = acc + jnp.dot(merged, wo_ref[ca, :], preferred_element_type=F32)
    x1 = x + acc
    x1_ref[...] = x1
    h2 = _rms_scale(x1) * gffn_ref[...]
    h2_hi = h2.astype(BF16)
    h2_lo = (h2 - h2_hi.astype(F32)).astype(BF16)
    hi_terms = jnp.dot(h2_hi, wr_ref[...], preferred_element_type=F32)
    lo_term = jnp.dot(h2_lo, wr_ref[:, :LANES], preferred_element_type=F32)
    logits = hi_terms[:, :LANES] + hi_terms[:, LANES:] + lo_term + br_ref[...]
    eid, wts = _route(logits)
    eid_ref[...] = eid
    wts_ref[...] = wts


def _mixer_back(x2d, za, sb, g_mix, w_merge, b_merge, w_a, w_b, w_out, g_ffn, w_router, b_router):
    n_tok = x2d.shape[0]
    tm = TM_BACK
    row = lambda i: (i, 0)
    return pl.pallas_call(
        _mixer_back_kernel,
        grid=(n_tok // tm,),
        in_specs=[
            pl.BlockSpec((tm, D_MODEL), row),
            pl.BlockSpec((tm, POOL_WIDTH), row),
            pl.BlockSpec((tm, SGU_WIDTH), row),
            _resident((1, D_MODEL)),
            _resident((D_MODEL, 2 * D_MODEL)),
            _resident((1, 2 * D_MODEL)),
            _resident((POOL_WIDTH, D_MODEL)),
            _resident((SGU_WIDTH, D_MODEL)),
            _resident((D_MODEL, D_MODEL)),
            _resident((1, D_MODEL)),
            _resident((D_MODEL, 2 * LANES)),
            _resident((1, LANES)),
        ],
        out_specs=[pl.BlockSpec((tm, D_MODEL), row), pl.BlockSpec((tm, LANES), row),
                   pl.BlockSpec((tm, LANES), row)],
        out_shape=[jax.ShapeDtypeStruct((n_tok, D_MODEL), F32),
                   jax.ShapeDtypeStruct((n_tok, LANES), jnp.int32),
                   jax.ShapeDtypeStruct((n_tok, LANES), F32)],
        compiler_params=pltpu.CompilerParams(dimension_semantics=("arbitrary",),
                                             vmem_limit_bytes=VMEM_LIMIT_BYTES),
        name="mixer_back",
    )(x2d, za, sb, g_mix, w_merge, b_merge, w_a, w_b, w_out, g_ffn, w_router, b_router)


def _row_gather_copy(src_hbm, src_row, dst_vmem, dst_row, sem):
    return pltpu.make_async_copy(src_hbm.at[pl.ds(src_row, 1), :], dst_vmem.at[pl.ds(dst_row, 1), :], sem)


def _expert_kernel(be_ref, nexte_ref, nused_ref, rowtok_ref, x1_hbm, gffn_ref, wg_hbm, wu_hbm, wd_hbm,
                   ys_ref, xbuf, sems, wg_st, wu_st, wd_st, wsems, wg_bf, wu_bf, wd_bf):
    rows = MOE_BLOCK_ROWS
    b = pl.program_id(0)
    n_used = nused_ref[0]
    slot = lax.rem(b, 2)

    def row_copy(blk, slt, r):
        return _row_gather_copy(x1_hbm, rowtok_ref[blk * rows + r], xbuf.at[slt], r, sems.at[slt])

    def wait_block(slt):
        pltpu.make_async_copy(x1_hbm.at[pl.ds(0, rows), :], xbuf.at[slt], sems.at[slt]).wait()

    def weight_copies(e):
        return [pltpu.make_async_copy(src.at[e], dst, wsems.at[j])
                for j, (src, dst) in enumerate(((wg_hbm, wg_st), (wu_hbm, wu_st), (wd_hbm, wd_st)))]

    @pl.when(b == 0)
    def _():
        def body(r, carry):
            row_copy(0, 0, r).start()
            return carry
        lax.fori_loop(0, rows, body, 0, unroll=GATHER_UNROLL)

    @pl.when(b < n_used)
    def _():
        @pl.when(b == 0)
        def _():
            for cp in weight_copies(be_ref[0]):
                cp.start()

        @pl.when(jnp.logical_or(b == 0, be_ref[b] != be_ref[jnp.maximum(b - 1, 0)]))
        def _():
            for cp in weight_copies(be_ref[b]):
                cp.wait()
            wg_bf[...] = wg_st[...].astype(BF16)
            wu_bf[...] = wu_st[...].astype(BF16)
            wd_bf[...] = wd_st[...].astype(BF16)
            nxt_e = nexte_ref[b]

            @pl.when(nxt_e >= 0)
            def _():
                for cp in weight_copies(nxt_e):
                    cp.start(priority=WEIGHT_DMA_PRIORITY)

        nxt = jnp.minimum(b + 1, n_used - 1)
        other = 1 - slot
        for r in range(rows):
            row_copy(nxt, other, r).start()
        wait_block(slot)

        h2 = (_rms_scale(xbuf[slot]) * gffn_ref[...]).astype(BF16)
        gate = jnp.dot(h2, wg_bf[...], preferred_element_type=F32)
        up = jnp.dot(h2, wu_bf[...], preferred_element_type=F32)
        hid = (jax.nn.silu(gate) * up).astype(BF16)
        ys_ref[...] = jnp.dot(hid, wd_bf[...], preferred_element_type=F32)

        @pl.when(b == n_used - 1)
        def _():
            wait_block(other)

    @pl.when(b >= n_used)
    def _():
        ys_ref[...] = jnp.zeros_like(ys_ref)


def _expert_ffn(block_e, next_e, n_used, row_tok, x1, g_ffn, w_g, w_u, w_d):
    n_rows = row_tok.shape[0]
    n_blocks = n_rows // MOE_BLOCK_ROWS
    grid_spec = pltpu.PrefetchScalarGridSpec(
        num_scalar_prefetch=4,
        grid=(n_blocks,),
        in_specs=[
            pl.BlockSpec(memory_space=pl.ANY),
            pl.BlockSpec((1, D_MODEL), lambda b, *_: (0, 0)),
            pl.BlockSpec(memory_space=pl.ANY),
            pl.BlockSpec(memory_space=pl.ANY),
            pl.BlockSpec(memory_space=pl.ANY),
        ],
        out_specs=pl.BlockSpec((MOE_BLOCK_ROWS, D_MODEL), lambda b, *_: (b, 0)),
        scratch_shapes=[
            pltpu.VMEM((2, MOE_BLOCK_ROWS, D_MODEL), F32),
            pltpu.SemaphoreType.DMA((2,)),
            pltpu.VMEM((D_MODEL, D_EXPERT), F32),
            pltpu.VMEM((D_MODEL, D_EXPERT), F32),
            pltpu.VMEM((D_EXPERT, D_MODEL), F32),
            pltpu.SemaphoreType.DMA((3,)),
            pltpu.VMEM((D_MODEL, D_EXPERT), BF16),
            pltpu.VMEM((D_MODEL, D_EXPERT), BF16),
            pltpu.VMEM((D_EXPERT, D_MODEL), BF16),
        ],
    )
    return pl.pallas_call(
        _expert_kernel,
        grid_spec=grid_spec,
        out_shape=jax.ShapeDtypeStruct((n_rows, D_MODEL), F32),
        compiler_params=pltpu.CompilerParams(dimension_semantics=("arbitrary",),
                                             vmem_limit_bytes=VMEM_LIMIT_BYTES),
        name="expert_ffn",
    )(block_e, next_e, n_used, row_tok, x1, g_ffn, w_g, w_u, w_d)


def _tail_kernel(dest_ref, x1_ref, wts_ref, ys_hbm, p_ref, gple_ref, wpg_ref, bpg_ref, wpu_ref, gfin_ref,
                 out_ref, ybuf, sems):
    tm = x1_ref.shape[0]
    i = pl.program_id(0)
    last = pl.num_programs(0) - 1
    slot = lax.rem(i, 2)

    def start_token(step, slt, r):
        for k in range(TOP_K):
            _row_gather_copy(ys_hbm, dest_ref[TOP_K * (step * tm + r) + k], ybuf.at[slt], k * tm + r,
                             sems.at[slt]).start()

    def wait_tile(slt):
        pltpu.make_async_copy(ys_hbm.at[pl.ds(0, TOP_K * tm), :], ybuf.at[slt], sems.at[slt]).wait()

    @pl.when(i == 0)
    def _():
        def body(r, carry):
            start_token(0, 0, r)
            return carry
        lax.fori_loop(0, tm, body, 0, unroll=GATHER_UNROLL)

    wait_tile(slot)
    nxt = jnp.minimum(i + 1, last)
    other = 1 - slot
    wts = wts_ref[...]
    x2 = x1_ref[...]
    for k in range(TOP_K):
        x2 = x2 + wts[:, k:k + 1] * ybuf[slot, k * tm:(k + 1) * tm, :]
    hn = (_rms_scale(x2) * gple_ref[...]).astype(BF16)
    up = jnp.dot(p_ref[...].astype(BF16), wpu_ref[...], preferred_element_type=F32)
    n_slabs = D_MODEL // MXU_COLS
    per = tm // n_slabs
    x3 = []
    for n in range(n_slabs):
        cols = slice(n * MXU_COLS, (n + 1) * MXU_COLS)
        for r in range(n * per, (n + 1) * per):
            start_token(nxt, other, r)
        gate = jax.nn.sigmoid(jnp.dot(hn, wpg_ref[:, cols], preferred_element_type=F32) + bpg_ref[:, cols])
        x3.append(x2[:, cols] + gate * up[:, cols])
    x3 = jnp.concatenate(x3, axis=1)
    out_ref[...] = _rms_scale(x3) * gfin_ref[...]

    @pl.when(i == last)
    def _():
        wait_tile(other)


def _tail(dest, x1, wts, ys, p2d, g_ple, w_pg, b_pg, w_pu, g_final):
    n_tok = x1.shape[0]
    tm = TM_TAIL
    row = lambda i, *_: (i, 0)
    const = lambda i, *_: (0, 0)
    grid_spec = pltpu.PrefetchScalarGridSpec(
        num_scalar_prefetch=1,
        grid=(n_tok // tm,),
        in_specs=[
            pl.BlockSpec((tm, D_MODEL), row),
            pl.BlockSpec((tm, LANES), row),
            pl.BlockSpec(memory_space=pl.ANY),
            pl.BlockSpec((tm, PLE_DIM), row),
            pl.BlockSpec((1, D_MODEL), const),
            pl.BlockSpec((D_MODEL, D_MODEL), const, pipeline_mode=pl.Buffered(1)),
            pl.BlockSpec((1, D_MODEL), const),
            pl.BlockSpec((PLE_DIM, D_MODEL), const),
            pl.BlockSpec((1, D_MODEL), const),
        ],
        out_specs=pl.BlockSpec((tm, D_MODEL), row),
        scratch_shapes=[pltpu.VMEM((2, TOP_K * tm, D_MODEL), F32), pltpu.SemaphoreType.DMA((2,))],
    )
    return pl.pallas_call(
        _tail_kernel,
        grid_spec=grid_spec,
        out_shape=jax.ShapeDtypeStruct((n_tok, D_MODEL), F32),
        compiler_params=pltpu.CompilerParams(dimension_semantics=("arbitrary",),
                                             vmem_limit_bytes=VMEM_LIMIT_BYTES),
        name="tail",
    )(dest, x1, wts, ys, p2d, g_ple, w_pg, b_pg, w_pu, g_final)


def _dispatch_plan(expert_id):
    n_tok = expert_id.shape[0]
    n_assign = n_tok * TOP_K
    n_blocks = -(-n_assign // MOE_BLOCK_ROWS) + N_EXPERTS
    n_rows = n_blocks * MOE_BLOCK_ROWS
    i32 = jnp.int32
    flat_e = expert_id.reshape(-1)
    experts = jnp.arange(N_EXPERTS, dtype=i32)
    assign = jnp.arange(n_assign, dtype=i32)
    se, order = lax.sort((flat_e, assign), num_keys=1)
    onehot_sorted = se[:, None] == experts[None, :]
    counts = jnp.sum(onehot_sorted.astype(i32), axis=0)
    padded = (counts + MOE_BLOCK_ROWS - 1) // MOE_BLOCK_ROWS * MOE_BLOCK_ROWS
    pad_end = jnp.cumsum(padded)
    pad_start = pad_end - padded
    start = jnp.cumsum(counts) - counts
    row_of_sorted = assign + jnp.sum(jnp.where(onehot_sorted, (pad_start - start)[None, :], 0), axis=1)
    _, dest = lax.sort((order, row_of_sorted), num_keys=1)
    n_used = pad_end[-1] // MOE_BLOCK_ROWS
    rows = jnp.arange(n_rows, dtype=i32)
    row_e = jnp.minimum(jnp.sum((pad_end[None, :] <= rows[:, None]).astype(i32), axis=1), N_EXPERTS - 1)
    onehot_row = row_e[:, None] == experts[None, :]
    pick = lambda table: jnp.sum(jnp.where(onehot_row, table[None, :], 0), axis=1)
    offset = rows - pick(pad_start)
    valid = offset < pick(counts)
    src = order[jnp.clip(pick(start) + offset, 0, n_assign - 1)]
    row_tok = jnp.where(valid, src // TOP_K, 0)
    block_e = row_e.reshape(n_blocks, MOE_BLOCK_ROWS)[:, 0]
    block_e = jnp.where(jnp.arange(n_blocks) < n_used, block_e, block_e[jnp.maximum(n_used - 1, 0)])
    later_used = jnp.logical_and(experts[None, :] > experts[:, None], (counts > 0)[None, :])
    next_used = jnp.min(jnp.where(later_used, experts[None, :], N_EXPERTS), axis=1)
    next_used = jnp.where(next_used < N_EXPERTS, next_used, -1)
    next_e = jnp.sum(jnp.where(block_e[:, None] == experts[None, :], next_used[None, :], 0), axis=1)
    return (block_e.astype(i32), next_e.astype(i32), n_used.astype(i32).reshape(1), row_tok.astype(i32),
            dest.astype(i32))


def kernel(x, p, g_mix, w_in, w_pool, pool_scale, w_branch_a, sgu_ln_g, sgu_ln_b, w_spatial, b_spatial, w_branch_b, w_merge_gate, b_merge_gate, w_out, g_ffn, w_router_group, b_router_group, w_router_expert, b_router_expert, w_exp_gate, w_exp_up, w_exp_down, g_ple, w_ple_gate, b_ple_gate, w_ple_up, g_final):
    bsz, seq, d = x.shape
    assert (seq, d) == (SEQ, D_MODEL) and g_mix.shape[0] == 1
    n_tok = bsz * seq
    x2d = x.reshape(n_tok, d)
    row2d = lambda v: v.reshape(1, -1)

    za, sb = _mixer_front(
        x2d, row2d(g_mix[0]), w_in[0].astype(BF16), w_pool[0].astype(BF16), row2d(pool_scale[0]),
        row2d(sgu_ln_g[0]), row2d(sgu_ln_b[0]), w_spatial[0], b_spatial[0][:, :, None])

    w_router = jnp.concatenate(
        [w_router_group[0], jnp.transpose(w_router_expert[0], (1, 0, 2)).reshape(d, N_EXPERTS)], axis=1)
    b_router = jnp.concatenate([b_router_group[0], b_router_expert[0].reshape(N_EXPERTS)])
    pad = LANES - w_router.shape[1]
    w_router = jnp.pad(w_router, ((0, 0), (0, pad)))
    b_router = jnp.pad(b_router, (0, pad))
    w_router_hi = w_router.astype(BF16)
    w_router_lo = (w_router - w_router_hi.astype(F32)).astype(BF16)
    w_router_split = jnp.concatenate([w_router_hi, w_router_lo], axis=1)

    x1, eid, wts = _mixer_back(
        x2d, za, sb, row2d(g_mix[0]), w_merge_gate[0].astype(BF16), row2d(b_merge_gate[0]),
        w_branch_a[0].astype(BF16), w_branch_b[0].astype(BF16), w_out[0].astype(BF16), row2d(g_ffn[0]),
        w_router_split, row2d(b_router))

    block_e, next_e, n_used, row_tok, dest = _dispatch_plan(eid[:, :TOP_K])
    ys = _expert_ffn(block_e, next_e, n_used, row_tok, x1, row2d(g_ffn[0]), w_exp_gate[0], w_exp_up[0],
                     w_exp_down[0])
    out = _tail(dest, x1, wts, ys, p[0].reshape(n_tok, PLE_DIM), row2d(g_ple[0]), w_ple_gate[0].astype(BF16),
                row2d(b_ple_gate[0]), w_ple_up[0].astype(BF16), row2d(g_final))
    return out.reshape(bsz, seq, d)
```

```python
import jax
import jax.numpy as jnp
from jax import lax
from jax.experimental import pallas as pl
from jax.experimental.pallas import tpu as pltpu

F32 = jnp.float32
BF16 = jnp.bfloat16

D_MODEL = 2048
SEQ = 4096
EPS = 1e-6
PLE_DIM = 256
POOL_WINDOWS = (2, 4, 8, 16)
POOL_WIDTH = D_MODEL // 2
POOL_GROUP_DIM = POOL_WIDTH // len(POOL_WINDOWS)
POOL_HISTORY = max(POOL_WINDOWS)
SGU_BLOCK = 128
SGU_CHUNK = 64
SGU_GROUPS = 8
SGU_WIDTH = D_MODEL // 2
SGU_GROUP_DIM = SGU_WIDTH // SGU_GROUPS
N_IN = POOL_WIDTH + 2 * SGU_WIDTH
N_EXPERT_GROUPS = 4
EXPERTS_PER_GROUP = 8
N_EXPERTS = N_EXPERT_GROUPS * EXPERTS_PER_GROUP
TOP_K = 2
D_EXPERT = D_MODEL // 4
MOE_BLOCK_ROWS = 256

LANES = 128
MXU_COLS = 256
ROUTER_EXPERT_LANE0 = N_EXPERT_GROUPS
VMEM_LIMIT_BYTES = 56 * 1024 * 1024

TM_FRONT = 256
TM_BACK = 256
TM_TAIL = 256
MERGE_CHUNK = 512
N_DMA_QUEUES = 2
WEIGHT_DMA_PRIORITY = 1
GATHER_UNROLL = 8


def _rms_scale(x):
    return x * lax.rsqrt(jnp.mean(x * x, axis=-1, keepdims=True) + EPS)


def _resident(shape):
    zeros = (0,) * len(shape)
    return pl.BlockSpec(shape, lambda *_: zeros, pipeline_mode=pl.Buffered(1))


def _mixer_front_kernel(x_ref, gmix_ref, win_ref, wpool_ref, pscale_ref, lng_ref, lnb_ref, ws_ref,
                        bsp_ref, za_ref, sb_ref, hist_ref):
    tm = x_ref.shape[0]
    tiles_per_seq = SEQ // tm
    seq_tile = lax.rem(pl.program_id(0), tiles_per_seq)

    @pl.when(seq_tile == 0)
    def _():
        hist_ref[...] = jnp.zeros_like(hist_ref)

    h = (_rms_scale(x_ref[...]) * gmix_ref[...]).astype(BF16)
    z = jnp.dot(h, win_ref[...], preferred_element_type=F32)

    a = z[:, :POOL_WIDTH]
    ext = jnp.concatenate([hist_ref[...], a], axis=0)
    hist_ref[...] = a[tm - POOL_HISTORY:, :]
    frames = (seq_tile * tm + 1 + lax.broadcasted_iota(jnp.int32, (tm, 1), 0)).astype(F32)
    for gi, w in enumerate(POOL_WINDOWS):
        cols = slice(gi * POOL_GROUP_DIM, (gi + 1) * POOL_GROUP_DIM)
        s = ext[:, cols]
        k = 1
        while k < w:
            s = s + pltpu.roll(s, k, 0)
            k *= 2
        wsum = s[POOL_HISTORY:, :]
        zg = wsum / jnp.minimum(frames, float(w)) - a[:, cols]
        yg = jnp.dot(zg.astype(BF16), wpool_ref[gi], preferred_element_type=F32)
        za_ref[:, cols] = (yg * pscale_ref[:, cols]).astype(BF16)

    u = jax.nn.gelu(z[:, POOL_WIDTH:POOL_WIDTH + SGU_WIDTH])
    v = jax.nn.gelu(z[:, POOL_WIDTH + SGU_WIDTH:])
    vc = v - jnp.mean(v, axis=-1, keepdims=True)
    var = jnp.mean(vc * vc, axis=-1, keepdims=True)
    vn = (vc * lax.rsqrt(var + EPS) * lng_ref[...] + lnb_ref[...]).astype(BF16)
    t_chunk = lax.broadcasted_iota(jnp.int32, (SGU_BLOCK, SGU_BLOCK), 0) // SGU_CHUNK
    s_chunk = lax.broadcasted_iota(jnp.int32, (SGU_BLOCK, SGU_BLOCK), 1) // SGU_CHUNK
    causal = s_chunk <= t_chunk
    nblk = tm // SGU_BLOCK
    for g in range(SGU_GROUPS):
        cols = slice(g * SGU_GROUP_DIM, (g + 1) * SGU_GROUP_DIM)
        wsg = jnp.where(causal, ws_ref[g], 0.0).astype(BF16)
        vg = jnp.concatenate([vn[j * SGU_BLOCK:(j + 1) * SGU_BLOCK, cols] for j in range(nblk)], axis=1)
        vm = jnp.dot(wsg, vg, preferred_element_type=F32) + bsp_ref[g]
        for j in range(nblk):
            rows = slice(j * SGU_BLOCK, (j + 1) * SGU_BLOCK)
            sb_ref[rows, cols] = (u[rows, cols] * vm[:, j * SGU_GROUP_DIM:(j + 1) * SGU_GROUP_DIM]).astype(BF16)


def _mixer_front(x2d, g_mix, w_in, w_pool, pool_scale, ln_g, ln_b, w_spatial, b_spatial):
    n_tok = x2d.shape[0]
    tm = TM_FRONT
    row = lambda i: (i, 0)
    return pl.pallas_call(
        _mixer_front_kernel,
        grid=(n_tok // tm,),
        in_specs=[
            pl.BlockSpec((tm, D_MODEL), row),
            _resident((1, D_MODEL)),
            _resident((D_MODEL, N_IN)),
            _resident((len(POOL_WINDOWS), POOL_GROUP_DIM, POOL_GROUP_DIM)),
            _resident((1, POOL_WIDTH)),
            _resident((1, SGU_WIDTH)),
            _resident((1, SGU_WIDTH)),
            _resident((SGU_GROUPS, SGU_BLOCK, SGU_BLOCK)),
            _resident((SGU_GROUPS, SGU_BLOCK, 1)),
        ],
        out_specs=[pl.BlockSpec((tm, POOL_WIDTH), row), pl.BlockSpec((tm, SGU_WIDTH), row)],
        out_shape=[jax.ShapeDtypeStruct((n_tok, POOL_WIDTH), BF16),
                   jax.ShapeDtypeStruct((n_tok, SGU_WIDTH), BF16)],
        scratch_shapes=[pltpu.VMEM((POOL_HISTORY, POOL_WIDTH), F32)],
        compiler_params=pltpu.CompilerParams(dimension_semantics=("arbitrary",),
                                             vmem_limit_bytes=VMEM_LIMIT_BYTES),
        name="mixer_front",
    )(x2d, g_mix, w_in, w_pool, pool_scale, ln_g, ln_b, w_spatial, b_spatial)


def _route(logits):
    lane = lax.broadcasted_iota(jnp.int32, logits.shape, 1).astype(F32)
    neg = -jnp.inf
    far = float(LANES)

    def first_argmax(vals):
        top = jnp.max(vals, axis=-1, keepdims=True)
        return top, jnp.min(jnp.where(vals == top, lane, far), axis=-1, keepdims=True)

    is_grp = lane < float(N_EXPERT_GROUPS)
    g_top, g_idx = first_argmax(jnp.where(is_grp, logits, neg))
    g_den = jnp.sum(jnp.where(is_grp, jnp.exp(logits - g_top), 0.0), axis=-1, keepdims=True)
    grp_p = 1.0 / g_den
    lo = float(ROUTER_EXPERT_LANE0) + g_idx * float(EXPERTS_PER_GROUP)
    e_log = jnp.where(lane >= lo, jnp.where(lane < lo + float(EXPERTS_PER_GROUP), logits, neg), neg)
    t1, i1 = first_argmax(e_log)
    t2, i2 = first_argmax(jnp.where(lane == i1, neg, e_log))
    r = jnp.exp(t2 - t1)
    w1 = grp_p / (1.0 + r)
    w2 = grp_p * r / (1.0 + r)
    e1 = i1 - float(ROUTER_EXPERT_LANE0)
    e2 = i2 - float(ROUTER_EXPERT_LANE0)
    eid = jnp.where(lane == 0.0, e1, jnp.where(lane == 1.0, e2, 0.0)).astype(jnp.int32)
    wts = jnp.where(lane == 0.0, w1, jnp.where(lane == 1.0, w2, 0.0))
    return eid, wts


def _mixer_back_kernel(x_ref, za_ref, sb_ref, gmix_ref, wm_ref, bm_ref, wa_ref, wb_ref, wo_ref,
                       gffn_ref, wr_ref, br_ref, x1_ref, eid_ref, wts_ref):
    x = x_ref[...]
    h = (_rms_scale(x) * gmix_ref[...]).astype(BF16)
    za = za_ref[...]
    sb = sb_ref[...]
    acc = jnp.zeros(x.shape, F32)
    for c in range(D_MODEL // MERGE_CHUNK):
        ca = slice(c * MERGE_CHUNK, (c + 1) * MERGE_CHUNK)
        cb = slice(D_MODEL + c * MERGE_CHUNK, D_MODEL + (c + 1) * MERGE_CHUNK)
        ga = jax.nn.sigmoid(jnp.dot(h, wm_ref[:, ca], preferred_element_type=F32) + bm_ref[:, ca])
        gb = jax.nn.sigmoid(jnp.dot(h, wm_ref[:, cb], preferred_element_type=F32) + bm_ref[:, cb])
        ya = jnp.dot(za, wa_ref[:, ca], preferred_element_type=F32)
        yb = jnp.dot(sb, wb_ref[:, ca], preferred_element_type=F32)
        merged = (ga * ya + gb * yb).astype(BF16)
        acc = acc + jnp.dot(merged, wo_ref[ca, :], preferred_element_type=F32)
    x1 = x + acc
    x1_ref[...] = x1
    h2 = _rms_scale(x1) * gffn_ref[...]
    h2_hi = h2.astype(BF16)
    h2_lo = (h2 - h2_hi.astype(F32)).astype(BF16)
    hi_terms = jnp.dot(h2_hi, wr_ref[...], preferred_element_type=F32)
    lo_term = jnp.dot(h2_lo, wr_ref[:, :LANES], preferred_element_type=F32)
    logits = hi_terms[:, :LANES] + hi_terms[:, LANES:] + lo_term + br_ref[...]
    eid, wts = _route(logits)
    eid_ref[...] = eid
    wts_ref[...] = wts


def _mixer_back(x2d, za, sb, g_mix, w_merge, b_merge, w_a, w_b, w_out, g_ffn, w_router, b_router):
    n_tok = x2d.shape[0]
    tm = TM_BACK
    row = lambda i: (i, 0)
    return pl.pallas_call(
        _mixer_back_kernel,
        grid=(n_tok // tm,),
        in_specs=[
            pl.BlockSpec((tm, D_MODEL), row),
            pl.BlockSpec((tm, POOL_WIDTH), row),
            pl.BlockSpec((tm, SGU_WIDTH), row),
            _resident((1, D_MODEL)),
            _resident((D_MODEL, 2 * D_MODEL)),
            _resident((1, 2 * D_MODEL)),
            _resident((POOL_WIDTH, D_MODEL)),
            _resident((SGU_WIDTH, D_MODEL)),
            _resident((D_MODEL, D_MODEL)),
            _resident((1, D_MODEL)),
            _resident((D_MODEL, 2 * LANES)),
            _resident((1, LANES)),
        ],
        out_specs=[pl.BlockSpec((tm, D_MODEL), row), pl.BlockSpec((tm, LANES), row),
                   pl.BlockSpec((tm, LANES), row)],
        out_shape=[jax.ShapeDtypeStruct((n_tok, D_MODEL), F32),
                   jax.ShapeDtypeStruct((n_tok, LANES), jnp.int32),
                   jax.ShapeDtypeStruct((n_tok, LANES), F32)],
        compiler_params=pltpu.CompilerParams(dimension_semantics=("arbitrary",),
                                             vmem_limit_bytes=VMEM_LIMIT_BYTES),
        name="mixer_back",
    )(x2d, za, sb, g_mix, w_merge, b_merge, w_a, w_b, w_out, g_ffn, w_router, b_router)


def _row_gather_copy(src_hbm, src_row, dst_vmem, dst_row, sem):
    return pltpu.make_async_copy(src_hbm.at[pl.ds(src_row, 1), :], dst_vmem.at[pl.ds(dst_row, 1), :], sem)


def _expert_kernel(be_ref, nexte_ref, nused_ref, rowtok_ref, x1_hbm, gffn_ref, wg_hbm, wu_hbm, wd_hbm,
                   ys_ref, xbuf, sems, wg_st, wu_st, wd_st, wsems, wg_bf, wu_bf, wd_bf):
    rows = MOE_BLOCK_ROWS
    b = pl.program_id(0)
    n_used = nused_ref[0]
    slot = lax.rem(b, 2)

    def row_copy(blk, slt, r):
        return _row_gather_copy(x1_hbm, rowtok_ref[blk * rows + r], xbuf.at[slt], r, sems.at[slt])

    def wait_block(slt):
        pltpu.make_async_copy(x1_hbm.at[pl.ds(0, rows), :], xbuf.at[slt], sems.at[slt]).wait()

    def weight_copies(e):
        return [pltpu.make_async_copy(src.at[e], dst, wsems.at[j])
                for j, (src, dst) in enumerate(((wg_hbm, wg_st), (wu_hbm, wu_st), (wd_hbm, wd_st)))]

    @pl.when(b == 0)
    def _():
        def body(r, carry):
            row_copy(0, 0, r).start()
            return carry
        lax.fori_loop(0, rows, body, 0, unroll=GATHER_UNROLL)

    @pl.when(b < n_used)
    def _():
        @pl.when(b == 0)
        def _():
            for cp in weight_copies(be_ref[0]):
                cp.start()

        @pl.when(jnp.logical_or(b == 0, be_ref[b] != be_ref[jnp.maximum(b - 1, 0)]))
        def _():
            for cp in weight_copies(be_ref[b]):
                cp.wait()
            wg_bf[...] = wg_st[...].astype(BF16)
            wu_bf[...] = wu_st[...].astype(BF16)
            wd_bf[...] = wd_st[...].astype(BF16)
            nxt_e = nexte_ref[b]

            @pl.when(nxt_e >= 0)
            def _():
                for cp in weight_copies(nxt_e):
                    cp.start(priority=WEIGHT_DMA_PRIORITY)

        nxt = jnp.minimum(b + 1, n_used - 1)
        other = 1 - slot
        for r in range(rows):
            row_copy(nxt, other, r).start(priority=r % N_DMA_QUEUES)
        wait_block(slot)

        h2 = (_rms_scale(xbuf[slot]) * gffn_ref[...]).astype(BF16)
        gate = jnp.dot(h2, wg_bf[...], preferred_element_type=F32)
        up = jnp.dot(h2, wu_bf[...], preferred_element_type=F32)
        hid = (jax.nn.silu(gate) * up).astype(BF16)
        ys_ref[...] = jnp.dot(hid, wd_bf[...], preferred_element_type=F32)

        @pl.when(b == n_used - 1)
        def _():
            wait_block(other)

    @pl.when(b >= n_used)
    def _():
        ys_ref[...] = jnp.zeros_like(ys_ref)


def _expert_ffn(block_e, next_e, n_used, row_tok, x1, g_ffn, w_g, w_u, w_d):
    n_rows = row_tok.shape[0]
    n_blocks = n_rows // MOE_BLOCK_ROWS
    grid_spec = pltpu.PrefetchScalarGridSpec(
        num_scalar_prefetch=4,
        grid=(n_blocks,),
        in_specs=[
            pl.BlockSpec(memory_space=pl.ANY),
            pl.BlockSpec((1, D_MODEL), lambda b, *_: (0, 0)),
            pl.BlockSpec(memory_space=pl.ANY),
            pl.BlockSpec(memory_space=pl.ANY),
            pl.BlockSpec(memory_space=pl.ANY),
        ],
        out_specs=pl.BlockSpec((MOE_BLOCK_ROWS, D_MODEL), lambda b, *_: (b, 0)),
        scratch_shapes=[
            pltpu.VMEM((2, MOE_BLOCK_ROWS, D_MODEL), F32),
            pltpu.SemaphoreType.DMA((2,)),
            pltpu.VMEM((D_MODEL, D_EXPERT), F32),
            pltpu.VMEM((D_MODEL, D_EXPERT), F32),
            pltpu.VMEM((D_EXPERT, D_MODEL), F32),
            pltpu.SemaphoreType.DMA((3,)),
            pltpu.VMEM((D_MODEL, D_EXPERT), BF16),
            pltpu.VMEM((D_MODEL, D_EXPERT), BF16),
            pltpu.VMEM((D_EXPERT, D_MODEL), BF16),
        ],
    )
    return pl.pallas_call(
        _expert_kernel,
        grid_spec=grid_spec,
        out_shape=jax.ShapeDtypeStruct((n_rows, D_MODEL), F32),
        compiler_params=pltpu.CompilerParams(dimension_semantics=("arbitrary",),
                                             vmem_limit_bytes=VMEM_LIMIT_BYTES),
        name="expert_ffn",
    )(block_e, next_e, n_used, row_tok, x1, g_ffn, w_g, w_u, w_d)


def _tail_kernel(dest_ref, x1_ref, wts_ref, ys_hbm, p_ref, gple_ref, wpg_ref, bpg_ref, wpu_ref, gfin_ref,
                 out_ref, ybuf, sems):
    tm = x1_ref.shape[0]
    i = pl.program_id(0)
    last = pl.num_programs(0) - 1
    slot = lax.rem(i, 2)

    def start_token(step, slt, r):
        for k in range(TOP_K):
            _row_gather_copy(ys_hbm, dest_ref[TOP_K * (step * tm + r) + k], ybuf.at[slt], k * tm + r,
                             sems.at[slt]).start(priority=k % N_DMA_QUEUES)

    def wait_tile(slt):
        pltpu.make_async_copy(ys_hbm.at[pl.ds(0, TOP_K * tm), :], ybuf.at[slt], sems.at[slt]).wait()

    @pl.when(i == 0)
    def _():
        def body(r, carry):
            start_token(0, 0, r)
            return carry
        lax.fori_loop(0, tm, body, 0, unroll=GATHER_UNROLL)

    wait_tile(slot)
    nxt = jnp.minimum(i + 1, last)
    other = 1 - slot
    wts = wts_ref[...]
    x2 = x1_ref[...]
    for k in range(TOP_K):
        x2 = x2 + wts[:, k:k + 1] * ybuf[slot, k * tm:(k + 1) * tm, :]
    hn = (_rms_scale(x2) * gple_ref[...]).astype(BF16)
    up = jnp.dot(p_ref[...].astype(BF16), wpu_ref[...], preferred_element_type=F32)
    n_slabs = D_MODEL // MXU_COLS
    per = tm // n_slabs
    x3 = []
    for n in range(n_slabs):
        cols = slice(n * MXU_COLS, (n + 1) * MXU_COLS)
        for r in range(n * per, (n + 1) * per):
            start_token(nxt, other, r)
        gate = jax.nn.sigmoid(jnp.dot(hn, wpg_ref[:, cols], preferred_element_type=F32) + bpg_ref[:, cols])
        x3.append(x2[:, cols] + gate * up[:, cols])
    x3 = jnp.concatenate(x3, axis=1)
    out_ref[...] = _rms_scale(x3) * gfin_ref[...]

    @pl.when(i == last)
    def _():
        wait_tile(other)


def _tail(dest, x1, wts, ys, p2d, g_ple, w_pg, b_pg, w_pu, g_final):
    n_tok = x1.shape[0]
    tm = TM_TAIL
    row = lambda i, *_: (i, 0)
    const = lambda i, *_: (0, 0)
    grid_spec = pltpu.PrefetchScalarGridSpec(
        num_scalar_prefetch=1,
        grid=(n_tok // tm,),
        in_specs=[
            pl.BlockSpec((tm, D_MODEL), row),
            pl.BlockSpec((tm, LANES), row),
            pl.BlockSpec(memory_space=pl.ANY),
            pl.BlockSpec((tm, PLE_DIM), row),
            pl.BlockSpec((1, D_MODEL), const),
            pl.BlockSpec((D_MODEL, D_MODEL), const, pipeline_mode=pl.Buffered(1)),
            pl.BlockSpec((1, D_MODEL), const),
            pl.BlockSpec((PLE_DIM, D_MODEL), const),
            pl.BlockSpec((1, D_MODEL), const),
        ],
        out_specs=pl.BlockSpec((tm, D_MODEL), row),
        scratch_shapes=[pltpu.VMEM((2, TOP_K * tm, D_MODEL), F32), pltpu.SemaphoreType.DMA((2,))],
    )
    return pl.pallas_call(
        _tail_kernel,
        grid_spec=grid_spec,
        out_shape=jax.ShapeDtypeStruct((n_tok, D_MODEL), F32),
        compiler_params=pltpu.CompilerParams(dimension_semantics=("arbitrary",),
                                             vmem_limit_bytes=VMEM_LIMIT_BYTES),
        name="tail",
    )(dest, x1, wts, ys, p2d, g_ple, w_pg, b_pg, w_pu, g_final)


def _dispatch_plan(expert_id):
    n_tok = expert_id.shape[0]
    n_assign = n_tok * TOP_K
    n_blocks = -(-n_assign // MOE_BLOCK_ROWS) + N_EXPERTS
    n_rows = n_blocks * MOE_BLOCK_ROWS
    i32 = jnp.int32
    flat_e = expert_id.reshape(-1)
    experts = jnp.arange(N_EXPERTS, dtype=i32)
    assign = jnp.arange(n_assign, dtype=i32)
    se, order = lax.sort((flat_e, assign), num_keys=1)
    onehot_sorted = se[:, None] == experts[None, :]
    counts = jnp.sum(onehot_sorted.astype(i32), axis=0)
    padded = (counts + MOE_BLOCK_ROWS - 1) // MOE_BLOCK_ROWS * MOE_BLOCK_ROWS
    pad_end = jnp.cumsum(padded)
    pad_start = pad_end - padded
    start = jnp.cumsum(counts) - counts
    row_of_sorted = assign + jnp.sum(jnp.where(onehot_sorted, (pad_start - start)[None, :], 0), axis=1)
    _, dest = lax.sort((order, row_of_sorted), num_keys=1)
    n_used = pad_end[-1] // MOE_BLOCK_ROWS
    rows = jnp.arange(n_rows, dtype=i32)
    row_e = jnp.minimum(jnp.sum((pad_end[None, :] <= rows[:, None]).astype(i32), axis=1), N_EXPERTS - 1)
    onehot_row = row_e[:, None] == experts[None, :]
    pick = lambda table: jnp.sum(jnp.where(onehot_row, table[None, :], 0), axis=1)
    offset = rows - pick(pad_start)
    valid = offset < pick(counts)
    src = order[jnp.clip(pick(start) + offset, 0, n_assign - 1)]
    row_tok = jnp.where(valid, src // TOP_K, 0)
    block_e = row_e.reshape(n_blocks, MOE_BLOCK_ROWS)[:, 0]
    block_e = jnp.where(jnp.arange(n_blocks) < n_used, block_e, block_e[jnp.maximum(n_used - 1, 0)])
    later_used = jnp.logical_and(experts[None, :] > experts[:, None], (counts > 0)[None, :])
    next_used = jnp.min(jnp.where(later_used, experts[None, :], N_EXPERTS), axis=1)
    next_used = jnp.where(next_used < N_EXPERTS, next_used, -1)
    next_e = jnp.sum(jnp.where(block_e[:, None] == experts[None, :], next_used[None, :], 0), axis=1)
    return (block_e.astype(i32), next_e.astype(i32), n_used.astype(i32).reshape(1), row_tok.astype(i32),
            dest.astype(i32))


def kernel(x, p, g_mix, w_in, w_pool, pool_scale, w_branch_a, sgu_ln_g, sgu_ln_b, w_spatial, b_spatial, w_branch_b, w_merge_gate, b_merge_gate, w_out, g_ffn, w_router_group, b_router_group, w_router_expert, b_router_expert, w_exp_gate, w_exp_up, w_exp_down, g_ple, w_ple_gate, b_ple_gate, w_ple_up, g_final):
    bsz, seq, d = x.shape
    assert (seq, d) == (SEQ, D_MODEL) and g_mix.shape[0] == 1
    n_tok = bsz * seq
    x2d = x.reshape(n_tok, d)
    row2d = lambda v: v.reshape(1, -1)

    za, sb = _mixer_front(
        x2d, row2d(g_mix[0]), w_in[0].astype(BF16), w_pool[0].astype(BF16), row2d(pool_scale[0]),
        row2d(sgu_ln_g[0]), row2d(sgu_ln_b[0]), w_spatial[0], b_spatial[0][:, :, None])

    w_router = jnp.concatenate(
        [w_router_group[0], jnp.transpose(w_router_expert[0], (1, 0, 2)).reshape(d, N_EXPERTS)], axis=1)
    b_router = jnp.concatenate([b_router_group[0], b_router_expert[0].reshape(N_EXPERTS)])
    pad = LANES - w_router.shape[1]
    w_router = jnp.pad(w_router, ((0, 0), (0, pad)))
    b_router = jnp.pad(b_router, (0, pad))
    w_router_hi = w_router.astype(BF16)
    w_router_lo = (w_router - w_router_hi.astype(F32)).astype(BF16)
    w_router_split = jnp.concatenate([w_router_hi, w_router_lo], axis=1)

    x1, eid, wts = _mixer_back(
        x2d, za, sb, row2d(g_mix[0]), w_merge_gate[0].astype(BF16), row2d(b_merge_gate[0]),
        w_branch_a[0].astype(BF16), w_branch_b[0].astype(BF16), w_out[0].astype(BF16), row2d(g_ffn[0]),
        w_router_split, row2d(b_router))

    block_e, next_e, n_used, row_tok, dest = _dispatch_plan(eid[:, :TOP_K])
    ys = _expert_ffn(block_e, next_e, n_used, row_tok, x1, row2d(g_ffn[0]), w_exp_gate[0], w_exp_up[0],
                     w_exp_down[0])
    out = _tail(dest, x1, wts, ys, p[0].reshape(n_tok, PLE_DIM), row2d(g_ple[0]), w_ple_gate[0].astype(BF16),
                row2d(b_ple_gate[0]), w_ple_up[0].astype(BF16), row2d(g_final))
    return out.reshape(bsz, seq, d)
```

```python
import jax
import jax.numpy as jnp
from jax import lax
from jax.experimental import pallas as pl
from jax.experimental.pallas import tpu as pltpu

F32 = jnp.float32
BF16 = jnp.bfloat16

D_MODEL = 2048
SEQ = 4096
EPS = 1e-6
PLE_DIM = 256
POOL_WINDOWS = (2, 4, 8, 16)
POOL_WIDTH = D_MODEL // 2
POOL_GROUP_DIM = POOL_WIDTH // len(POOL_WINDOWS)
POOL_HISTORY = max(POOL_WINDOWS)
SGU_BLOCK = 128
SGU_CHUNK = 64
SGU_GROUPS = 8
SGU_WIDTH = D_MODEL // 2
SGU_GROUP_DIM = SGU_WIDTH // SGU_GROUPS
N_IN = POOL_WIDTH + 2 * SGU_WIDTH
N_EXPERT_GROUPS = 4
EXPERTS_PER_GROUP = 8
N_EXPERTS = N_EXPERT_GROUPS * EXPERTS_PER_GROUP
TOP_K = 2
D_EXPERT = D_MODEL // 4
MOE_BLOCK_ROWS = 256

LANES = 128
MXU_COLS = 256
ROUTER_EXPERT_LANE0 = N_EXPERT_GROUPS
VMEM_LIMIT_BYTES = 56 * 1024 * 1024

TM_FRONT = 256
TM_BACK = 256
TM_TAIL = 256
MERGE_CHUNK = 512
WEIGHT_DMA_PRIORITY = 1
PACKED_COLS = D_MODEL // 2
GATHER_UNROLL = 8


def _rms_scale(x):
    return x * lax.rsqrt(jnp.mean(x * x, axis=-1, keepdims=True) + EPS)


def _resident(shape):
    zeros = (0,) * len(shape)
    return pl.BlockSpec(shape, lambda *_: zeros, pipeline_mode=pl.Buffered(1))


def _mixer_front_kernel(x_ref, gmix_ref, win_ref, wpool_ref, pscale_ref, lng_ref, lnb_ref, ws_ref,
                        bsp_ref, za_ref, sb_ref, hist_ref):
    tm = x_ref.shape[0]
    tiles_per_seq = SEQ // tm
    seq_tile = lax.rem(pl.program_id(0), tiles_per_seq)

    @pl.when(seq_tile == 0)
    def _():
        hist_ref[...] = jnp.zeros_like(hist_ref)

    h = (_rms_scale(x_ref[...]) * gmix_ref[...]).astype(BF16)
    z = jnp.dot(h, win_ref[...], preferred_element_type=F32)

    a = z[:, :POOL_WIDTH]
    ext = jnp.concatenate([hist_ref[...], a], axis=0)
    hist_ref[...] = a[tm - POOL_HISTORY:, :]
    frames = (seq_tile * tm + 1 + lax.broadcasted_iota(jnp.int32, (tm, 1), 0)).astype(F32)
    for gi, w in enumerate(POOL_WINDOWS):
        cols = slice(gi * POOL_GROUP_DIM, (gi + 1) * POOL_GROUP_DIM)
        s = ext[:, cols]
        k = 1
        while k < w:
            s = s + pltpu.roll(s, k, 0)
            k *= 2
        wsum = s[POOL_HISTORY:, :]
        zg = wsum / jnp.minimum(frames, float(w)) - a[:, cols]
        yg = jnp.dot(zg.astype(BF16), wpool_ref[gi], preferred_element_type=F32)
        za_ref[:, cols] = (yg * pscale_ref[:, cols]).astype(BF16)

    u = jax.nn.gelu(z[:, POOL_WIDTH:POOL_WIDTH + SGU_WIDTH])
    v = jax.nn.gelu(z[:, POOL_WIDTH + SGU_WIDTH:])
    vc = v - jnp.mean(v, axis=-1, keepdims=True)
    var = jnp.mean(vc * vc, axis=-1, keepdims=True)
    vn = (vc * lax.rsqrt(var + EPS) * lng_ref[...] + lnb_ref[...]).astype(BF16)
    t_chunk = lax.broadcasted_iota(jnp.int32, (SGU_BLOCK, SGU_BLOCK), 0) // SGU_CHUNK
    s_chunk = lax.broadcasted_iota(jnp.int32, (SGU_BLOCK, SGU_BLOCK), 1) // SGU_CHUNK
    causal = s_chunk <= t_chunk
    nblk = tm // SGU_BLOCK
    for g in range(SGU_GROUPS):
        cols = slice(g * SGU_GROUP_DIM, (g + 1) * SGU_GROUP_DIM)
        wsg = jnp.where(causal, ws_ref[g], 0.0).astype(BF16)
        vg = jnp.concatenate([vn[j * SGU_BLOCK:(j + 1) * SGU_BLOCK, cols] for j in range(nblk)], axis=1)
        vm = jnp.dot(wsg, vg, preferred_element_type=F32) + bsp_ref[g]
        for j in range(nblk):
            rows = slice(j * SGU_BLOCK, (j + 1) * SGU_BLOCK)
            sb_ref[rows, cols] = (u[rows, cols] * vm[:, j * SGU_GROUP_DIM:(j + 1) * SGU_GROUP_DIM]).astype(BF16)


def _mixer_front(x2d, g_mix, w_in, w_pool, pool_scale, ln_g, ln_b, w_spatial, b_spatial):
    n_tok = x2d.shape[0]
    tm = TM_FRONT
    row = lambda i: (i, 0)
    return pl.pallas_call(
        _mixer_front_kernel,
        grid=(n_tok // tm,),
        in_specs=[
            pl.BlockSpec((tm, D_MODEL), row),
            _resident((1, D_MODEL)),
            _resident((D_MODEL, N_IN)),
            _resident((len(POOL_WINDOWS), POOL_GROUP_DIM, POOL_GROUP_DIM)),
            _resident((1, POOL_WIDTH)),
            _resident((1, SGU_WIDTH)),
            _resident((1, SGU_WIDTH)),
            _resident((SGU_GROUPS, SGU_BLOCK, SGU_BLOCK)),
            _resident((SGU_GROUPS, SGU_BLOCK, 1)),
        ],
        out_specs=[pl.BlockSpec((tm, POOL_WIDTH), row), pl.BlockSpec((tm, SGU_WIDTH), row)],
        out_shape=[jax.ShapeDtypeStruct((n_tok, POOL_WIDTH), BF16),
                   jax.ShapeDtypeStruct((n_tok, SGU_WIDTH), BF16)],
        scratch_shapes=[pltpu.VMEM((POOL_HISTORY, POOL_WIDTH), F32)],
        compiler_params=pltpu.CompilerParams(dimension_semantics=("arbitrary",),
                                             vmem_limit_bytes=VMEM_LIMIT_BYTES),
        name="mixer_front",
    )(x2d, g_mix, w_in, w_pool, pool_scale, ln_g, ln_b, w_spatial, b_spatial)


def _route(logits):
    lane = lax.broadcasted_iota(jnp.int32, logits.shape, 1).astype(F32)
    neg = -jnp.inf
    far = float(LANES)

    def first_argmax(vals):
        top = jnp.max(vals, axis=-1, keepdims=True)
        return top, jnp.min(jnp.where(vals == top, lane, far), axis=-1, keepdims=True)

    is_grp = lane < float(N_EXPERT_GROUPS)
    g_top, g_idx = first_argmax(jnp.where(is_grp, logits, neg))
    g_den = jnp.sum(jnp.where(is_grp, jnp.exp(logits - g_top), 0.0), axis=-1, keepdims=True)
    grp_p = 1.0 / g_den
    lo = float(ROUTER_EXPERT_LANE0) + g_idx * float(EXPERTS_PER_GROUP)
    e_log = jnp.where(lane >= lo, jnp.where(lane < lo + float(EXPERTS_PER_GROUP), logits, neg), neg)
    t1, i1 = first_argmax(e_log)
    t2, i2 = first_argmax(jnp.where(lane == i1, neg, e_log))
    r = jnp.exp(t2 - t1)
    w1 = grp_p / (1.0 + r)
    w2 = grp_p * r / (1.0 + r)
    e1 = i1 - float(ROUTER_EXPERT_LANE0)
    e2 = i2 - float(ROUTER_EXPERT_LANE0)
    eid = jnp.where(lane == 0.0, e1, jnp.where(lane == 1.0, e2, 0.0)).astype(jnp.int32)
    wts = jnp.where(lane == 0.0, w1, jnp.where(lane == 1.0, w2, 0.0))
    return eid, wts


def _mixer_back_kernel(x_ref, za_ref, sb_ref, gmix_ref, wm_ref, bm_ref, wa_ref, wb_ref, wo_ref,
                       gffn_ref, wr_ref, br_ref, x1_ref, h2p_ref, eid_ref, wts_ref):
    x = x_ref[...]
    h = (_rms_scale(x) * gmix_ref[...]).astype(BF16)
    za = za_ref[...]
    sb = sb_ref[...]
    acc = jnp.zeros(x.shape, F32)
    for c in range(D_MODEL // MERGE_CHUNK):
        ca = slice(c * MERGE_CHUNK, (c + 1) * MERGE_CHUNK)
        cb = slice(D_MODEL + c * MERGE_CHUNK, D_MODEL + (c + 1) * MERGE_CHUNK)
        ga = jax.nn.sigmoid(jnp.dot(h, wm_ref[:, ca], preferred_element_type=F32) + bm_ref[:, ca])
        gb = jax.nn.sigmoid(jnp.dot(h, wm_ref[:, cb], preferred_element_type=F32) + bm_ref[:, cb])
        ya = jnp.dot(za, wa_ref[:, ca], preferred_element_type=F32)
        yb = jnp.dot(sb, wb_ref[:, ca], preferred_element_type=F32)
        merged = (ga * ya + gb * yb).astype(BF16)
        acc = acc + jnp.dot(merged, wo_ref[ca, :], preferred_element_type=F32)
    x1 = x + acc
    x1_ref[...] = x1
    h2 = _rms_scale(x1) * gffn_ref[...]
    h2_hi = h2.astype(BF16)
    h2_lo = (h2 - h2_hi.astype(F32)).astype(BF16)
    as_bits = lambda v: lax.bitcast_convert_type(v.astype(F32), jnp.uint32)
    h2p_ref[...] = (as_bits(h2_hi[:, :PACKED_COLS]) >> 16) | as_bits(h2_hi[:, PACKED_COLS:])
    hi_terms = jnp.dot(h2_hi, wr_ref[...], preferred_element_type=F32)
    lo_term = jnp.dot(h2_lo, wr_ref[:, :LANES], preferred_element_type=F32)
    logits = hi_terms[:, :LANES] + hi_terms[:, LANES:] + lo_term + br_ref[...]
    eid, wts = _route(logits)
    eid_ref[...] = eid
    wts_ref[...] = wts


def _mixer_back(x2d, za, sb, g_mix, w_merge, b_merge, w_a, w_b, w_out, g_ffn, w_router, b_router):
    n_tok = x2d.shape[0]
    tm = TM_BACK
    row = lambda i: (i, 0)
    return pl.pallas_call(
        _mixer_back_kernel,
        grid=(n_tok // tm,),
        in_specs=[
            pl.BlockSpec((tm, D_MODEL), row),
            pl.BlockSpec((tm, POOL_WIDTH), row),
            pl.BlockSpec((tm, SGU_WIDTH), row),
            _resident((1, D_MODEL)),
            _resident((D_MODEL, 2 * D_MODEL)),
            _resident((1, 2 * D_MODEL)),
            _resident((POOL_WIDTH, D_MODEL)),
            _resident((SGU_WIDTH, D_MODEL)),
            _resident((D_MODEL, D_MODEL)),
            _resident((1, D_MODEL)),
            _resident((D_MODEL, 2 * LANES)),
            _resident((1, LANES)),
        ],
        out_specs=[pl.BlockSpec((tm, D_MODEL), row), pl.BlockSpec((tm, PACKED_COLS), row),
                   pl.BlockSpec((tm, LANES), row), pl.BlockSpec((tm, LANES), row)],
        out_shape=[jax.ShapeDtypeStruct((n_tok, D_MODEL), F32),
                   jax.ShapeDtypeStruct((n_tok, PACKED_COLS), jnp.uint32),
                   jax.ShapeDtypeStruct((n_tok, LANES), jnp.int32),
                   jax.ShapeDtypeStruct((n_tok, LANES), F32)],
        compiler_params=pltpu.CompilerParams(dimension_semantics=("arbitrary",),
                                             vmem_limit_bytes=VMEM_LIMIT_BYTES),
        name="mixer_back",
    )(x2d, za, sb, g_mix, w_merge, b_merge, w_a, w_b, w_out, g_ffn, w_router, b_router)


def _row_gather_copy(src_hbm, src_row, dst_vmem, dst_row, sem):
    return pltpu.make_async_copy(src_hbm.at[pl.ds(src_row, 1), :], dst_vmem.at[pl.ds(dst_row, 1), :], sem)


def _expert_kernel(be_ref, nexte_ref, nused_ref, rowtok_ref, h2p_hbm, wg_hbm, wu_hbm, wd_hbm,
                   ys_ref, xbuf, sems, wg_st, wu_st, wd_st, wsems, wg_bf, wu_bf, wd_bf):
    rows = MOE_BLOCK_ROWS
    b = pl.program_id(0)
    n_used = nused_ref[0]
    slot = lax.rem(b, 2)

    def row_copy(blk, slt, r):
        return _row_gather_copy(h2p_hbm, rowtok_ref[blk * rows + r], xbuf.at[slt], r, sems.at[slt])

    def wait_block(slt):
        pltpu.make_async_copy(h2p_hbm.at[pl.ds(0, rows), :], xbuf.at[slt], sems.at[slt]).wait()

    def weight_copies(e):
        return [pltpu.make_async_copy(src.at[e], dst, wsems.at[j])
                for j, (src, dst) in enumerate(((wg_hbm, wg_st), (wu_hbm, wu_st), (wd_hbm, wd_st)))]

    @pl.when(b == 0)
    def _():
        def body(r, carry):
            row_copy(0, 0, r).start()
            return carry
        lax.fori_loop(0, rows, body, 0, unroll=GATHER_UNROLL)

    @pl.when(b < n_used)
    def _():
        @pl.when(b == 0)
        def _():
            for cp in weight_copies(be_ref[0]):
                cp.start()

        @pl.when(jnp.logical_or(b == 0, be_ref[b] != be_ref[jnp.maximum(b - 1, 0)]))
        def _():
            for cp in weight_copies(be_ref[b]):
                cp.wait()
            wg_bf[...] = wg_st[...].astype(BF16)
            wu_bf[...] = wu_st[...].astype(BF16)
            wd_bf[...] = wd_st[...].astype(BF16)
            nxt_e = nexte_ref[b]

            @pl.when(nxt_e >= 0)
            def _():
                for cp in weight_copies(nxt_e):
                    cp.start(priority=WEIGHT_DMA_PRIORITY)

        nxt = jnp.minimum(b + 1, n_used - 1)
        other = 1 - slot
        for r in range(rows):
            row_copy(nxt, other, r).start()
        wait_block(slot)

        words = xbuf[slot]
        from_bits = lambda v: lax.bitcast_convert_type(v, F32).astype(BF16)
        h2 = jnp.concatenate([from_bits(words << 16), from_bits(words & jnp.uint32(0xFFFF0000))], axis=1)
        gate = jnp.dot(h2, wg_bf[...], preferred_element_type=F32)
        up = jnp.dot(h2, wu_bf[...], preferred_element_type=F32)
        hid = (jax.nn.silu(gate) * up).astype(BF16)
        ys_ref[...] = jnp.dot(hid, wd_bf[...], preferred_element_type=F32)

        @pl.when(b == n_used - 1)
        def _():
            wait_block(other)

    @pl.when(b >= n_used)
    def _():
        ys_ref[...] = jnp.zeros_like(ys_ref)


def _expert_ffn(block_e, next_e, n_used, row_tok, h2p, w_g, w_u, w_d):
    n_rows = row_tok.shape[0]
    n_blocks = n_rows // MOE_BLOCK_ROWS
    grid_spec = pltpu.PrefetchScalarGridSpec(
        num_scalar_prefetch=4,
        grid=(n_blocks,),
        in_specs=[
            pl.BlockSpec(memory_space=pl.ANY),
            pl.BlockSpec(memory_space=pl.ANY),
            pl.BlockSpec(memory_space=pl.ANY),
            pl.BlockSpec(memory_space=pl.ANY),
        ],
        out_specs=pl.BlockSpec((MOE_BLOCK_ROWS, D_MODEL), lambda b, *_: (b, 0)),
        scratch_shapes=[
            pltpu.VMEM((2, MOE_BLOCK_ROWS, PACKED_COLS), jnp.uint32),
            pltpu.SemaphoreType.DMA((2,)),
            pltpu.VMEM((D_MODEL, D_EXPERT), F32),
            pltpu.VMEM((D_MODEL, D_EXPERT), F32),
            pltpu.VMEM((D_EXPERT, D_MODEL), F32),
            pltpu.SemaphoreType.DMA((3,)),
            pltpu.VMEM((D_MODEL, D_EXPERT), BF16),
            pltpu.VMEM((D_MODEL, D_EXPERT), BF16),
            pltpu.VMEM((D_EXPERT, D_MODEL), BF16),
        ],
    )
    return pl.pallas_call(
        _expert_kernel,
        grid_spec=grid_spec,
        out_shape=jax.ShapeDtypeStruct((n_rows, D_MODEL), F32),
        compiler_params=pltpu.CompilerParams(dimension_semantics=("arbitrary",),
                                             vmem_limit_bytes=VMEM_LIMIT_BYTES),
        name="expert_ffn",
    )(block_e, next_e, n_used, row_tok, h2p, w_g, w_u, w_d)


def _tail_kernel(dest_ref, x1_ref, wts_ref, ys_hbm, p_ref, gple_ref, wpg_ref, bpg_ref, wpu_ref, gfin_ref,
                 out_ref, ybuf, sems):
    tm = x1_ref.shape[0]
    i = pl.program_id(0)
    last = pl.num_programs(0) - 1
    slot = lax.rem(i, 2)

    def start_token(step, slt, r):
        for k in range(TOP_K):
            _row_gather_copy(ys_hbm, dest_ref[TOP_K * (step * tm + r) + k], ybuf.at[slt], k * tm + r,
                             sems.at[slt]).start()

    def wait_tile(slt):
        pltpu.make_async_copy(ys_hbm.at[pl.ds(0, TOP_K * tm), :], ybuf.at[slt], sems.at[slt]).wait()

    @pl.when(i == 0)
    def _():
        def body(r, carry):
            start_token(0, 0, r)
            return carry
        lax.fori_loop(0, tm, body, 0, unroll=GATHER_UNROLL)

    wait_tile(slot)
    nxt = jnp.minimum(i + 1, last)
    other = 1 - slot
    wts = wts_ref[...]
    x2 = x1_ref[...]
    for k in range(TOP_K):
        x2 = x2 + wts[:, k:k + 1] * ybuf[slot, k * tm:(k + 1) * tm, :]
    hn = (_rms_scale(x2) * gple_ref[...]).astype(BF16)
    up = jnp.dot(p_ref[...].astype(BF16), wpu_ref[...], preferred_element_type=F32)
    n_slabs = D_MODEL // MXU_COLS
    per = tm // n_slabs
    x3 = []
    for n in range(n_slabs):
        cols = slice(n * MXU_COLS, (n + 1) * MXU_COLS)
        for r in range(n * per, (n + 1) * per):
            start_token(nxt, other, r)
        gate = jax.nn.sigmoid(jnp.dot(hn, wpg_ref[:, cols], preferred_element_type=F32) + bpg_ref[:, cols])
        x3.append(x2[:, cols] + gate * up[:, cols])
    x3 = jnp.concatenate(x3, axis=1)
    out_ref[...] = _rms_scale(x3) * gfin_ref[...]

    @pl.when(i == last)
    def _():
        wait_tile(other)


def _tail(dest, x1, wts, ys, p2d, g_ple, w_pg, b_pg, w_pu, g_final):
    n_tok = x1.shape[0]
    tm = TM_TAIL
    row = lambda i, *_: (i, 0)
    const = lambda i, *_: (0, 0)
    grid_spec = pltpu.PrefetchScalarGridSpec(
        num_scalar_prefetch=1,
        grid=(n_tok // tm,),
        in_specs=[
            pl.BlockSpec((tm, D_MODEL), row),
            pl.BlockSpec((tm, LANES), row),
            pl.BlockSpec(memory_space=pl.ANY),
            pl.BlockSpec((tm, PLE_DIM), row),
            pl.BlockSpec((1, D_MODEL), const),
            pl.BlockSpec((D_MODEL, D_MODEL), const, pipeline_mode=pl.Buffered(1)),
            pl.BlockSpec((1, D_MODEL), const),
            pl.BlockSpec((PLE_DIM, D_MODEL), const),
            pl.BlockSpec((1, D_MODEL), const),
        ],
        out_specs=pl.BlockSpec((tm, D_MODEL), row),
        scratch_shapes=[pltpu.VMEM((2, TOP_K * tm, D_MODEL), F32), pltpu.SemaphoreType.DMA((2,))],
    )
    return pl.pallas_call(
        _tail_kernel,
        grid_spec=grid_spec,
        out_shape=jax.ShapeDtypeStruct((n_tok, D_MODEL), F32),
        compiler_params=pltpu.CompilerParams(dimension_semantics=("arbitrary",),
                                             vmem_limit_bytes=VMEM_LIMIT_BYTES),
        name="tail",
    )(dest, x1, wts, ys, p2d, g_ple, w_pg, b_pg, w_pu, g_final)


def _dispatch_plan(expert_id):
    n_tok = expert_id.shape[0]
    n_assign = n_tok * TOP_K
    n_blocks = -(-n_assign // MOE_BLOCK_ROWS) + N_EXPERTS
    n_rows = n_blocks * MOE_BLOCK_ROWS
    i32 = jnp.int32
    flat_e = expert_id.reshape(-1)
    experts = jnp.arange(N_EXPERTS, dtype=i32)
    assign = jnp.arange(n_assign, dtype=i32)
    se, order = lax.sort((flat_e, assign), num_keys=1)
    onehot_sorted = se[:, None] == experts[None, :]
    counts = jnp.sum(onehot_sorted.astype(i32), axis=0)
    padded = (counts + MOE_BLOCK_ROWS - 1) // MOE_BLOCK_ROWS * MOE_BLOCK_ROWS
    pad_end = jnp.cumsum(padded)
    pad_start = pad_end - padded
    start = jnp.cumsum(counts) - counts
    row_of_sorted = assign + jnp.sum(jnp.where(onehot_sorted, (pad_start - start)[None, :], 0), axis=1)
    _, dest = lax.sort((order, row_of_sorted), num_keys=1)
    n_used = pad_end[-1] // MOE_BLOCK_ROWS
    rows = jnp.arange(n_rows, dtype=i32)
    row_e = jnp.minimum(jnp.sum((pad_end[None, :] <= rows[:, None]).astype(i32), axis=1), N_EXPERTS - 1)
    onehot_row = row_e[:, None] == experts[None, :]
    pick = lambda table: jnp.sum(jnp.where(onehot_row, table[None, :], 0), axis=1)
    offset = rows - pick(pad_start)
    valid = offset < pick(counts)
    src = order[jnp.clip(pick(start) + offset, 0, n_assign - 1)]
    row_tok = jnp.where(valid, src // TOP_K, 0)
    block_e = row_e.reshape(n_blocks, MOE_BLOCK_ROWS)[:, 0]
    block_e = jnp.where(jnp.arange(n_blocks) < n_used, block_e, block_e[jnp.maximum(n_used - 1, 0)])
    later_used = jnp.logical_and(experts[None, :] > experts[:, None], (counts > 0)[None, :])
    next_used = jnp.min(jnp.where(later_used, experts[None, :], N_EXPERTS), axis=1)
    next_used = jnp.where(next_used < N_EXPERTS, next_used, -1)
    next_e = jnp.sum(jnp.where(block_e[:, None] == experts[None, :], next_used[None, :], 0), axis=1)
    return (block_e.astype(i32), next_e.astype(i32), n_used.astype(i32).reshape(1), row_tok.astype(i32),
            dest.astype(i32))


def kernel(x, p, g_mix, w_in, w_pool, pool_scale, w_branch_a, sgu_ln_g, sgu_ln_b, w_spatial, b_spatial, w_branch_b, w_merge_gate, b_merge_gate, w_out, g_ffn, w_router_group, b_router_group, w_router_expert, b_router_expert, w_exp_gate, w_exp_up, w_exp_down, g_ple, w_ple_gate, b_ple_gate, w_ple_up, g_final):
    bsz, seq, d = x.shape
    assert (seq, d) == (SEQ, D_MODEL) and g_mix.shape[0] == 1
    n_tok = bsz * seq
    x2d = x.reshape(n_tok, d)
    row2d = lambda v: v.reshape(1, -1)

    za, sb = _mixer_front(
        x2d, row2d(g_mix[0]), w_in[0].astype(BF16), w_pool[0].astype(BF16), row2d(pool_scale[0]),
        row2d(sgu_ln_g[0]), row2d(sgu_ln_b[0]), w_spatial[0], b_spatial[0][:, :, None])

    w_router = jnp.concatenate(
        [w_router_group[0], jnp.transpose(w_router_expert[0], (1, 0, 2)).reshape(d, N_EXPERTS)], axis=1)
    b_router = jnp.concatenate([b_router_group[0], b_router_expert[0].reshape(N_EXPERTS)])
    pad = LANES - w_router.shape[1]
    w_router = jnp.pad(w_router, ((0, 0), (0, pad)))
    b_router = jnp.pad(b_router, (0, pad))
    w_router_hi = w_router.astype(BF16)
    w_router_lo = (w_router - w_router_hi.astype(F32)).astype(BF16)
    w_router_split = jnp.concatenate([w_router_hi, w_router_lo], axis=1)

    x1, h2p, eid, wts = _mixer_back(
        x2d, za, sb, row2d(g_mix[0]), w_merge_gate[0].astype(BF16), row2d(b_merge_gate[0]),
        w_branch_a[0].astype(BF16), w_branch_b[0].astype(BF16), w_out[0].astype(BF16), row2d(g_ffn[0]),
        w_router_split, row2d(b_router))

    block_e, next_e, n_used, row_tok, dest = _dispatch_plan(eid[:, :TOP_K])
    ys = _expert_ffn(block_e, next_e, n_used, row_tok, h2p, w_exp_gate[0], w_exp_up[0], w_exp_down[0])
    out = _tail(dest, x1, wts, ys, p[0].reshape(n_tok, PLE_DIM), row2d(g_ple[0]), w_ple_gate[0].astype(BF16),
                row2d(b_ple_gate[0]), w_ple_up[0].astype(BF16), row2d(g_final))
    return out.reshape(bsz, seq, d)
```

```python
import jax
import jax.numpy as jnp
from jax import lax
from jax.experimental import pallas as pl
from jax.experimental.pallas import tpu as pltpu

F32 = jnp.float32
BF16 = jnp.bfloat16

D_MODEL = 2048
SEQ = 4096
EPS = 1e-6
PLE_DIM = 256
POOL_WINDOWS = (2, 4, 8, 16)
POOL_WIDTH = D_MODEL // 2
POOL_GROUP_DIM = POOL_WIDTH // len(POOL_WINDOWS)
POOL_HISTORY = max(POOL_WINDOWS)
SGU_BLOCK = 128
SGU_CHUNK = 64
SGU_GROUPS = 8
SGU_WIDTH = D_MODEL // 2
SGU_GROUP_DIM = SGU_WIDTH // SGU_GROUPS
N_IN = POOL_WIDTH + 2 * SGU_WIDTH
N_EXPERT_GROUPS = 4
EXPERTS_PER_GROUP = 8
N_EXPERTS = N_EXPERT_GROUPS * EXPERTS_PER_GROUP
TOP_K = 2
D_EXPERT = D_MODEL // 4
MOE_BLOCK_ROWS = 256

LANES = 128
MXU_COLS = 256
ROUTER_EXPERT_LANE0 = N_EXPERT_GROUPS
VMEM_LIMIT_BYTES = 56 * 1024 * 1024

TM_FRONT = 256
TM_BACK = 256
TM_TAIL = 256
MERGE_CHUNK = 512
WEIGHT_DMA_PRIORITY = 1
GATHER_AHEAD = 2
GATHER_SLOTS = GATHER_AHEAD + 1
PACKED_COLS = D_MODEL // 2
GATHER_UNROLL = 8


def _rms_scale(x):
    return x * lax.rsqrt(jnp.mean(x * x, axis=-1, keepdims=True) + EPS)


def _resident(shape):
    zeros = (0,) * len(shape)
    return pl.BlockSpec(shape, lambda *_: zeros, pipeline_mode=pl.Buffered(1))


def _mixer_front_kernel(x_ref, gmix_ref, win_ref, wpool_ref, pscale_ref, lng_ref, lnb_ref, ws_ref,
                        bsp_ref, za_ref, sb_ref, hist_ref):
    tm = x_ref.shape[0]
    tiles_per_seq = SEQ // tm
    seq_tile = lax.rem(pl.program_id(0), tiles_per_seq)

    @pl.when(seq_tile == 0)
    def _():
        hist_ref[...] = jnp.zeros_like(hist_ref)

    h = (_rms_scale(x_ref[...]) * gmix_ref[...]).astype(BF16)
    z = jnp.dot(h, win_ref[...], preferred_element_type=F32)

    a = z[:, :POOL_WIDTH]
    ext = jnp.concatenate([hist_ref[...], a], axis=0)
    hist_ref[...] = a[tm - POOL_HISTORY:, :]
    frames = (seq_tile * tm + 1 + lax.broadcasted_iota(jnp.int32, (tm, 1), 0)).astype(F32)
    for gi, w in enumerate(POOL_WINDOWS):
        cols = slice(gi * POOL_GROUP_DIM, (gi + 1) * POOL_GROUP_DIM)
        s = ext[:, cols]
        k = 1
        while k < w:
            s = s + pltpu.roll(s, k, 0)
            k *= 2
        wsum = s[POOL_HISTORY:, :]
        zg = wsum / jnp.minimum(frames, float(w)) - a[:, cols]
        yg = jnp.dot(zg.astype(BF16), wpool_ref[gi], preferred_element_type=F32)
        za_ref[:, cols] = (yg * pscale_ref[:, cols]).astype(BF16)

    u = jax.nn.gelu(z[:, POOL_WIDTH:POOL_WIDTH + SGU_WIDTH])
    v = jax.nn.gelu(z[:, POOL_WIDTH + SGU_WIDTH:])
    vc = v - jnp.mean(v, axis=-1, keepdims=True)
    var = jnp.mean(vc * vc, axis=-1, keepdims=True)
    vn = (vc * lax.rsqrt(var + EPS) * lng_ref[...] + lnb_ref[...]).astype(BF16)
    t_chunk = lax.broadcasted_iota(jnp.int32, (SGU_BLOCK, SGU_BLOCK), 0) // SGU_CHUNK
    s_chunk = lax.broadcasted_iota(jnp.int32, (SGU_BLOCK, SGU_BLOCK), 1) // SGU_CHUNK
    causal = s_chunk <= t_chunk
    nblk = tm // SGU_BLOCK
    for g in range(SGU_GROUPS):
        cols = slice(g * SGU_GROUP_DIM, (g + 1) * SGU_GROUP_DIM)
        wsg = jnp.where(causal, ws_ref[g], 0.0).astype(BF16)
        vg = jnp.concatenate([vn[j * SGU_BLOCK:(j + 1) * SGU_BLOCK, cols] for j in range(nblk)], axis=1)
        vm = jnp.dot(wsg, vg, preferred_element_type=F32) + bsp_ref[g]
        for j in range(nblk):
            rows = slice(j * SGU_BLOCK, (j + 1) * SGU_BLOCK)
            sb_ref[rows, cols] = (u[rows, cols] * vm[:, j * SGU_GROUP_DIM:(j + 1) * SGU_GROUP_DIM]).astype(BF16)


def _mixer_front(x2d, g_mix, w_in, w_pool, pool_scale, ln_g, ln_b, w_spatial, b_spatial):
    n_tok = x2d.shape[0]
    tm = TM_FRONT
    row = lambda i: (i, 0)
    return pl.pallas_call(
        _mixer_front_kernel,
        grid=(n_tok // tm,),
        in_specs=[
            pl.BlockSpec((tm, D_MODEL), row),
            _resident((1, D_MODEL)),
            _resident((D_MODEL, N_IN)),
            _resident((len(POOL_WINDOWS), POOL_GROUP_DIM, POOL_GROUP_DIM)),
            _resident((1, POOL_WIDTH)),
            _resident((1, SGU_WIDTH)),
            _resident((1, SGU_WIDTH)),
            _resident((SGU_GROUPS, SGU_BLOCK, SGU_BLOCK)),
            _resident((SGU_GROUPS, SGU_BLOCK, 1)),
        ],
        out_specs=[pl.BlockSpec((tm, POOL_WIDTH), row), pl.BlockSpec((tm, SGU_WIDTH), row)],
        out_shape=[jax.ShapeDtypeStruct((n_tok, POOL_WIDTH), BF16),
                   jax.ShapeDtypeStruct((n_tok, SGU_WIDTH), BF16)],
        scratch_shapes=[pltpu.VMEM((POOL_HISTORY, POOL_WIDTH), F32)],
        compiler_params=pltpu.CompilerParams(dimension_semantics=("arbitrary",),
                                             vmem_limit_bytes=VMEM_LIMIT_BYTES),
        name="mixer_front",
    )(x2d, g_mix, w_in, w_pool, pool_scale, ln_g, ln_b, w_spatial, b_spatial)


def _route(logits):
    lane = lax.broadcasted_iota(jnp.int32, logits.shape, 1).astype(F32)
    neg = -jnp.inf
    far = float(LANES)

    def first_argmax(vals):
        top = jnp.max(vals, axis=-1, keepdims=True)
        return top, jnp.min(jnp.where(vals == top, lane, far), axis=-1, keepdims=True)

    is_grp = lane < float(N_EXPERT_GROUPS)
    g_top, g_idx = first_argmax(jnp.where(is_grp, logits, neg))
    g_den = jnp.sum(jnp.where(is_grp, jnp.exp(logits - g_top), 0.0), axis=-1, keepdims=True)
    grp_p = 1.0 / g_den
    lo = float(ROUTER_EXPERT_LANE0) + g_idx * float(EXPERTS_PER_GROUP)
    e_log = jnp.where(lane >= lo, jnp.where(lane < lo + float(EXPERTS_PER_GROUP), logits, neg), neg)
    t1, i1 = first_argmax(e_log)
    t2, i2 = first_argmax(jnp.where(lane == i1, neg, e_log))
    r = jnp.exp(t2 - t1)
    w1 = grp_p / (1.0 + r)
    w2 = grp_p * r / (1.0 + r)
    e1 = i1 - float(ROUTER_EXPERT_LANE0)
    e2 = i2 - float(ROUTER_EXPERT_LANE0)
    eid = jnp.where(lane == 0.0, e1, jnp.where(lane == 1.0, e2, 0.0)).astype(jnp.int32)
    wts = jnp.where(lane == 0.0, w1, jnp.where(lane == 1.0, w2, 0.0))
    return eid, wts


def _mixer_back_kernel(x_ref, za_ref, sb_ref, gmix_ref, wm_ref, bm_ref, wa_ref, wb_ref, wo_ref,
                       gffn_ref, wr_ref, br_ref, x1_ref, h2p_ref, eid_ref, wts_ref):
    x = x_ref[...]
    h = (_rms_scale(x) * gmix_ref[...]).astype(BF16)
    za = za_ref[...]
    sb = sb_ref[...]
    acc = jnp.zeros(x.shape, F32)
    for c in range(D_MODEL // MERGE_CHUNK):
        ca = slice(c * MERGE_CHUNK, (c + 1) * MERGE_CHUNK)
        cb = slice(D_MODEL + c * MERGE_CHUNK, D_MODEL + (c + 1) * MERGE_CHUNK)
        ga = jax.nn.sigmoid(jnp.dot(h, wm_ref[:, ca], preferred_element_type=F32) + bm_ref[:, ca])
        gb = jax.nn.sigmoid(jnp.dot(h, wm_ref[:, cb], preferred_element_type=F32) + bm_ref[:, cb])
        ya = jnp.dot(za, wa_ref[:, ca], preferred_element_type=F32)
        yb = jnp.dot(sb, wb_ref[:, ca], preferred_element_type=F32)
        merged = (ga * ya + gb * yb).astype(BF16)
        acc = acc + jnp.dot(merged, wo_ref[ca, :], preferred_element_type=F32)
    x1 = x + acc
    x1_ref[...] = x1
    h2 = _rms_scale(x1) * gffn_ref[...]
    h2_hi = h2.astype(BF16)
    h2_lo = (h2 - h2_hi.astype(F32)).astype(BF16)
    as_bits = lambda v: lax.bitcast_convert_type(v.astype(F32), jnp.uint32)
    h2p_ref[...] = (as_bits(h2_hi[:, :PACKED_COLS]) >> 16) | as_bits(h2_hi[:, PACKED_COLS:])
    hi_terms = jnp.dot(h2_hi, wr_ref[...], preferred_element_type=F32)
    lo_term = jnp.dot(h2_lo, wr_ref[:, :LANES], preferred_element_type=F32)
    logits = hi_terms[:, :LANES] + hi_terms[:, LANES:] + lo_term + br_ref[...]
    eid, wts = _route(logits)
    eid_ref[...] = eid
    wts_ref[...] = wts


def _mixer_back(x2d, za, sb, g_mix, w_merge, b_merge, w_a, w_b, w_out, g_ffn, w_router, b_router):
    n_tok = x2d.shape[0]
    tm = TM_BACK
    row = lambda i: (i, 0)
    return pl.pallas_call(
        _mixer_back_kernel,
        grid=(n_tok // tm,),
        in_specs=[
            pl.BlockSpec((tm, D_MODEL), row),
            pl.BlockSpec((tm, POOL_WIDTH), row),
            pl.BlockSpec((tm, SGU_WIDTH), row),
            _resident((1, D_MODEL)),
            _resident((D_MODEL, 2 * D_MODEL)),
            _resident((1, 2 * D_MODEL)),
            _resident((POOL_WIDTH, D_MODEL)),
            _resident((SGU_WIDTH, D_MODEL)),
            _resident((D_MODEL, D_MODEL)),
            _resident((1, D_MODEL)),
            _resident((D_MODEL, 2 * LANES)),
            _resident((1, LANES)),
        ],
        out_specs=[pl.BlockSpec((tm, D_MODEL), row), pl.BlockSpec((tm, PACKED_COLS), row),
                   pl.BlockSpec((tm, LANES), row), pl.BlockSpec((tm, LANES), row)],
        out_shape=[jax.ShapeDtypeStruct((n_tok, D_MODEL), F32),
                   jax.ShapeDtypeStruct((n_tok, PACKED_COLS), jnp.uint32),
                   jax.ShapeDtypeStruct((n_tok, LANES), jnp.int32),
                   jax.ShapeDtypeStruct((n_tok, LANES), F32)],
        compiler_params=pltpu.CompilerParams(dimension_semantics=("arbitrary",),
                                             vmem_limit_bytes=VMEM_LIMIT_BYTES),
        name="mixer_back",
    )(x2d, za, sb, g_mix, w_merge, b_merge, w_a, w_b, w_out, g_ffn, w_router, b_router)


def _row_gather_copy(src_hbm, src_row, dst_vmem, dst_row, sem):
    return pltpu.make_async_copy(src_hbm.at[pl.ds(src_row, 1), :], dst_vmem.at[pl.ds(dst_row, 1), :], sem)


def _expert_kernel(be_ref, nexte_ref, nused_ref, rowtok_ref, h2p_hbm, wg_hbm, wu_hbm, wd_hbm,
                   ys_ref, xbuf, sems, wg_st, wu_st, wd_st, wsems, wg_bf, wu_bf, wd_bf):
    rows = MOE_BLOCK_ROWS
    b = pl.program_id(0)
    n_used = nused_ref[0]
    slot = lax.rem(b, GATHER_SLOTS)

    def row_copy(blk, slt, r):
        return _row_gather_copy(h2p_hbm, rowtok_ref[blk * rows + r], xbuf.at[slt], r, sems.at[slt])

    def wait_block(slt):
        pltpu.make_async_copy(h2p_hbm.at[pl.ds(0, rows), :], xbuf.at[slt], sems.at[slt]).wait()

    def weight_copies(e):
        return [pltpu.make_async_copy(src.at[e], dst, wsems.at[j])
                for j, (src, dst) in enumerate(((wg_hbm, wg_st), (wu_hbm, wu_st), (wd_hbm, wd_st)))]

    last_used = n_used - 1

    @pl.when(b == 0)
    def _():
        for ahead in range(GATHER_AHEAD):
            def body(r, carry, ahead=ahead):
                row_copy(jnp.minimum(ahead, last_used), ahead, r).start()
                return carry
            lax.fori_loop(0, rows, body, 0, unroll=GATHER_UNROLL)

    @pl.when(b < n_used)
    def _():
        @pl.when(b == 0)
        def _():
            for cp in weight_copies(be_ref[0]):
                cp.start()

        @pl.when(jnp.logical_or(b == 0, be_ref[b] != be_ref[jnp.maximum(b - 1, 0)]))
        def _():
            for cp in weight_copies(be_ref[b]):
                cp.wait()
            wg_bf[...] = wg_st[...].astype(BF16)
            wu_bf[...] = wu_st[...].astype(BF16)
            wd_bf[...] = wd_st[...].astype(BF16)
            nxt_e = nexte_ref[b]

            @pl.when(nxt_e >= 0)
            def _():
                for cp in weight_copies(nxt_e):
                    cp.start(priority=WEIGHT_DMA_PRIORITY)

        wait_block(slot)
        words = xbuf[slot]
        from_bits = lambda v: lax.bitcast_convert_type(v, F32).astype(BF16)
        h2 = jnp.concatenate([from_bits(words << 16), from_bits(words & jnp.uint32(0xFFFF0000))], axis=1)
        ahead_blk = jnp.minimum(b + GATHER_AHEAD, last_used)
        ahead_slot = lax.rem(b + GATHER_AHEAD, GATHER_SLOTS)
        for r in range(rows):
            row_copy(ahead_blk, ahead_slot, r).start()
        gate = jnp.dot(h2, wg_bf[...], preferred_element_type=F32)
        up = jnp.dot(h2, wu_bf[...], preferred_element_type=F32)
        hid = (jax.nn.silu(gate) * up).astype(BF16)
        ys_ref[...] = jnp.dot(hid, wd_bf[...], preferred_element_type=F32)

        @pl.when(b == last_used)
        def _():
            for ahead in range(1, GATHER_SLOTS):
                wait_block(lax.rem(b + ahead, GATHER_SLOTS))

    @pl.when(b >= n_used)
    def _():
        ys_ref[...] = jnp.zeros_like(ys_ref)


def _expert_ffn(block_e, next_e, n_used, row_tok, h2p, w_g, w_u, w_d):
    n_rows = row_tok.shape[0]
    n_blocks = n_rows // MOE_BLOCK_ROWS
    grid_spec = pltpu.PrefetchScalarGridSpec(
        num_scalar_prefetch=4,
        grid=(n_blocks,),
        in_specs=[
            pl.BlockSpec(memory_space=pl.ANY),
            pl.BlockSpec(memory_space=pl.ANY),
            pl.BlockSpec(memory_space=pl.ANY),
            pl.BlockSpec(memory_space=pl.ANY),
        ],
        out_specs=pl.BlockSpec((MOE_BLOCK_ROWS, D_MODEL), lambda b, *_: (b, 0)),
        scratch_shapes=[
            pltpu.VMEM((GATHER_SLOTS, MOE_BLOCK_ROWS, PACKED_COLS), jnp.uint32),
            pltpu.SemaphoreType.DMA((GATHER_SLOTS,)),
            pltpu.VMEM((D_MODEL, D_EXPERT), F32),
            pltpu.VMEM((D_MODEL, D_EXPERT), F32),
            pltpu.VMEM((D_EXPERT, D_MODEL), F32),
            pltpu.SemaphoreType.DMA((3,)),
            pltpu.VMEM((D_MODEL, D_EXPERT), BF16),
            pltpu.VMEM((D_MODEL, D_EXPERT), BF16),
            pltpu.VMEM((D_EXPERT, D_MODEL), BF16),
        ],
    )
    return pl.pallas_call(
        _expert_kernel,
        grid_spec=grid_spec,
        out_shape=jax.ShapeDtypeStruct((n_rows, D_MODEL), F32),
        compiler_params=pltpu.CompilerParams(dimension_semantics=("arbitrary",),
                                             vmem_limit_bytes=VMEM_LIMIT_BYTES),
        name="expert_ffn",
    )(block_e, next_e, n_used, row_tok, h2p, w_g, w_u, w_d)


def _tail_kernel(dest_ref, x1_ref, wts_ref, ys_hbm, p_ref, gple_ref, wpg_ref, bpg_ref, wpu_ref, gfin_ref,
                 out_ref, ybuf, sems):
    tm = x1_ref.shape[0]
    i = pl.program_id(0)
    last = pl.num_programs(0) - 1
    slot = lax.rem(i, 2)

    def start_token(step, slt, r):
        for k in range(TOP_K):
            _row_gather_copy(ys_hbm, dest_ref[TOP_K * (step * tm + r) + k], ybuf.at[slt], k * tm + r,
                             sems.at[slt]).start()

    def wait_tile(slt):
        pltpu.make_async_copy(ys_hbm.at[pl.ds(0, TOP_K * tm), :], ybuf.at[slt], sems.at[slt]).wait()

    @pl.when(i == 0)
    def _():
        def body(r, carry):
            start_token(0, 0, r)
            return carry
        lax.fori_loop(0, tm, body, 0, unroll=GATHER_UNROLL)

    wait_tile(slot)
    nxt = jnp.minimum(i + 1, last)
    other = 1 - slot
    wts = wts_ref[...]
    x2 = x1_ref[...]
    for k in range(TOP_K):
        x2 = x2 + wts[:, k:k + 1] * ybuf[slot, k * tm:(k + 1) * tm, :]
    hn = (_rms_scale(x2) * gple_ref[...]).astype(BF16)
    up = jnp.dot(p_ref[...].astype(BF16), wpu_ref[...], preferred_element_type=F32)
    n_slabs = D_MODEL // MXU_COLS
    per = tm // n_slabs
    x3 = []
    for n in range(n_slabs):
        cols = slice(n * MXU_COLS, (n + 1) * MXU_COLS)
        for r in range(n * per, (n + 1) * per):
            start_token(nxt, other, r)
        gate = jax.nn.sigmoid(jnp.dot(hn, wpg_ref[:, cols], preferred_element_type=F32) + bpg_ref[:, cols])
        x3.append(x2[:, cols] + gate * up[:, cols])
    x3 = jnp.concatenate(x3, axis=1)
    out_ref[...] = _rms_scale(x3) * gfin_ref[...]

    @pl.when(i == last)
    def _():
        wait_tile(other)


def _tail(dest, x1, wts, ys, p2d, g_ple, w_pg, b_pg, w_pu, g_final):
    n_tok = x1.shape[0]
    tm = TM_TAIL
    row = lambda i, *_: (i, 0)
    const = lambda i, *_: (0, 0)
    grid_spec = pltpu.PrefetchScalarGridSpec(
        num_scalar_prefetch=1,
        grid=(n_tok // tm,),
        in_specs=[
            pl.BlockSpec((tm, D_MODEL), row),
            pl.BlockSpec((tm, LANES), row),
            pl.BlockSpec(memory_space=pl.ANY),
            pl.BlockSpec((tm, PLE_DIM), row),
            pl.BlockSpec((1, D_MODEL), const),
            pl.BlockSpec((D_MODEL, D_MODEL), const, pipeline_mode=pl.Buffered(1)),
            pl.BlockSpec((1, D_MODEL), const),
            pl.BlockSpec((PLE_DIM, D_MODEL), const),
            pl.BlockSpec((1, D_MODEL), const),
        ],
        out_specs=pl.BlockSpec((tm, D_MODEL), row),
        scratch_shapes=[pltpu.VMEM((2, TOP_K * tm, D_MODEL), F32), pltpu.SemaphoreType.DMA((2,))],
    )
    return pl.pallas_call(
        _tail_kernel,
        grid_spec=grid_spec,
        out_shape=jax.ShapeDtypeStruct((n_tok, D_MODEL), F32),
        compiler_params=pltpu.CompilerParams(dimension_semantics=("arbitrary",),
                                             vmem_limit_bytes=VMEM_LIMIT_BYTES),
        name="tail",
    )(dest, x1, wts, ys, p2d, g_ple, w_pg, b_pg, w_pu, g_final)


def _dispatch_plan(expert_id):
    n_tok = expert_id.shape[0]
    n_assign = n_tok * TOP_K
    n_blocks = -(-n_assign // MOE_BLOCK_ROWS) + N_EXPERTS
    n_rows = n_blocks * MOE_BLOCK_ROWS
    i32 = jnp.int32
    flat_e = expert_id.reshape(-1)
    experts = jnp.arange(N_EXPERTS, dtype=i32)
    assign = jnp.arange(n_assign, dtype=i32)
    se, order = lax.sort((flat_e, assign), num_keys=1)
    onehot_sorted = se[:, None] == experts[None, :]
    counts = jnp.sum(onehot_sorted.astype(i32), axis=0)
    padded = (counts + MOE_BLOCK_ROWS - 1) // MOE_BLOCK_ROWS * MOE_BLOCK_ROWS
    pad_end = jnp.cumsum(padded)
    pad_start = pad_end - padded
    start = jnp.cumsum(counts) - counts
    row_of_sorted = assign + jnp.sum(jnp.where(onehot_sorted, (pad_start - start)[None, :], 0), axis=1)
    _, dest = lax.sort((order, row_of_sorted), num_keys=1)
    n_used = pad_end[-1] // MOE_BLOCK_ROWS
    rows = jnp.arange(n_rows, dtype=i32)
    row_e = jnp.minimum(jnp.sum((pad_end[None, :] <= rows[:, None]).astype(i32), axis=1), N_EXPERTS - 1)
    onehot_row = row_e[:, None] == experts[None, :]
    pick = lambda table: jnp.sum(jnp.where(onehot_row, table[None, :], 0), axis=1)
    offset = rows - pick(pad_start)
    valid = offset < pick(counts)
    src = order[jnp.clip(pick(start) + offset, 0, n_assign - 1)]
    row_tok = jnp.where(valid, src // TOP_K, 0)
    block_e = row_e.reshape(n_blocks, MOE_BLOCK_ROWS)[:, 0]
    block_e = jnp.where(jnp.arange(n_blocks) < n_used, block_e, block_e[jnp.maximum(n_used - 1, 0)])
    later_used = jnp.logical_and(experts[None, :] > experts[:, None], (counts > 0)[None, :])
    next_used = jnp.min(jnp.where(later_used, experts[None, :], N_EXPERTS), axis=1)
    next_used = jnp.where(next_used < N_EXPERTS, next_used, -1)
    next_e = jnp.sum(jnp.where(block_e[:, None] == experts[None, :], next_used[None, :], 0), axis=1)
    return (block_e.astype(i32), next_e.astype(i32), n_used.astype(i32).reshape(1), row_tok.astype(i32),
            dest.astype(i32))


def kernel(x, p, g_mix, w_in, w_pool, pool_scale, w_branch_a, sgu_ln_g, sgu_ln_b, w_spatial, b_spatial, w_branch_b, w_merge_gate, b_merge_gate, w_out, g_ffn, w_router_group, b_router_group, w_router_expert, b_router_expert, w_exp_gate, w_exp_up, w_exp_down, g_ple, w_ple_gate, b_ple_gate, w_ple_up, g_final):
    bsz, seq, d = x.shape
    assert (seq, d) == (SEQ, D_MODEL) and g_mix.shape[0] == 1
    n_tok = bsz * seq
    x2d = x.reshape(n_tok, d)
    row2d = lambda v: v.reshape(1, -1)

    za, sb = _mixer_front(
        x2d, row2d(g_mix[0]), w_in[0].astype(BF16), w_pool[0].astype(BF16), row2d(pool_scale[0]),
        row2d(sgu_ln_g[0]), row2d(sgu_ln_b[0]), w_spatial[0], b_spatial[0][:, :, None])

    w_router = jnp.concatenate(
        [w_router_group[0], jnp.transpose(w_router_expert[0], (1, 0, 2)).reshape(d, N_EXPERTS)], axis=1)
    b_router = jnp.concatenate([b_router_group[0], b_router_expert[0].reshape(N_EXPERTS)])
    pad = LANES - w_router.shape[1]
    w_router = jnp.pad(w_router, ((0, 0), (0, pad)))
    b_router = jnp.pad(b_router, (0, pad))
    w_router_hi = w_router.astype(BF16)
    w_router_lo = (w_router - w_router_hi.astype(F32)).astype(BF16)
    w_router_split = jnp.concatenate([w_router_hi, w_router_lo], axis=1)

    x1, h2p, eid, wts = _mixer_back(
        x2d, za, sb, row2d(g_mix[0]), w_merge_gate[0].astype(BF16), row2d(b_merge_gate[0]),
        w_branch_a[0].astype(BF16), w_branch_b[0].astype(BF16), w_out[0].astype(BF16), row2d(g_ffn[0]),
        w_router_split, row2d(b_router))

    block_e, next_e, n_used, row_tok, dest = _dispatch_plan(eid[:, :TOP_K])
    ys = _expert_ffn(block_e, next_e, n_used, row_tok, h2p, w_exp_gate[0], w_exp_up[0], w_exp_down[0])
    out = _tail(dest, x1, wts, ys, p[0].reshape(n_tok, PLE_DIM), row2d(g_ple[0]), w_ple_gate[0].astype(BF16),
                row2d(b_ple_gate[0]), w_ple_up[0].astype(BF16), row2d(g_final))
    return out.reshape(bsz, seq, d)
```

```python
import jax
import jax.numpy as jnp
from jax import lax
from jax.experimental import pallas as pl
from jax.experimental.pallas import tpu as pltpu

F32 = jnp.float32
BF16 = jnp.bfloat16

D_MODEL = 2048
SEQ = 4096
EPS = 1e-6
PLE_DIM = 256
POOL_WINDOWS = (2, 4, 8, 16)
POOL_WIDTH = D_MODEL // 2
POOL_GROUP_DIM = POOL_WIDTH // len(POOL_WINDOWS)
POOL_HISTORY = max(POOL_WINDOWS)
SGU_BLOCK = 128
SGU_CHUNK = 64
SGU_GROUPS = 8
SGU_WIDTH = D_MODEL // 2
SGU_GROUP_DIM = SGU_WIDTH // SGU_GROUPS
N_IN = POOL_WIDTH + 2 * SGU_WIDTH
N_EXPERT_GROUPS = 4
EXPERTS_PER_GROUP = 8
N_EXPERTS = N_EXPERT_GROUPS * EXPERTS_PER_GROUP
TOP_K = 2
D_EXPERT = D_MODEL // 4
MOE_BLOCK_ROWS = 256

LANES = 128
MXU_COLS = 256
ROUTER_EXPERT_LANE0 = N_EXPERT_GROUPS
VMEM_LIMIT_BYTES = 56 * 1024 * 1024

TM_FRONT = 256
TM_BACK = 256
TM_TAIL = 256
MERGE_CHUNK = 512
N_DMA_QUEUES = 2
WEIGHT_DMA_CHUNKS = 4
GATHER_AHEAD = 2
GATHER_SLOTS = GATHER_AHEAD + 1
PACKED_COLS = D_MODEL // 2
GATHER_UNROLL = 8


def _rms_scale(x):
    return x * lax.rsqrt(jnp.mean(x * x, axis=-1, keepdims=True) + EPS)


def _resident(shape):
    zeros = (0,) * len(shape)
    return pl.BlockSpec(shape, lambda *_: zeros, pipeline_mode=pl.Buffered(1))


def _mixer_front_kernel(x_ref, gmix_ref, win_ref, wpool_ref, pscale_ref, lng_ref, lnb_ref, ws_ref,
                        bsp_ref, za_ref, sb_ref, hist_ref):
    tm = x_ref.shape[0]
    tiles_per_seq = SEQ // tm
    seq_tile = lax.rem(pl.program_id(0), tiles_per_seq)

    @pl.when(seq_tile == 0)
    def _():
        hist_ref[...] = jnp.zeros_like(hist_ref)

    h = (_rms_scale(x_ref[...]) * gmix_ref[...]).astype(BF16)
    z = jnp.dot(h, win_ref[...], preferred_element_type=F32)

    a = z[:, :POOL_WIDTH]
    ext = jnp.concatenate([hist_ref[...], a], axis=0)
    hist_ref[...] = a[tm - POOL_HISTORY:, :]
    frames = (seq_tile * tm + 1 + lax.broadcasted_iota(jnp.int32, (tm, 1), 0)).astype(F32)
    for gi, w in enumerate(POOL_WINDOWS):
        cols = slice(gi * POOL_GROUP_DIM, (gi + 1) * POOL_GROUP_DIM)
        s = ext[:, cols]
        k = 1
        while k < w:
            s = s + pltpu.roll(s, k, 0)
            k *= 2
        wsum = s[POOL_HISTORY:, :]
        zg = wsum / jnp.minimum(frames, float(w)) - a[:, cols]
        yg = jnp.dot(zg.astype(BF16), wpool_ref[gi], preferred_element_type=F32)
        za_ref[:, cols] = (yg * pscale_ref[:, cols]).astype(BF16)

    u = jax.nn.gelu(z[:, POOL_WIDTH:POOL_WIDTH + SGU_WIDTH])
    v = jax.nn.gelu(z[:, POOL_WIDTH + SGU_WIDTH:])
    vc = v - jnp.mean(v, axis=-1, keepdims=True)
    var = jnp.mean(vc * vc, axis=-1, keepdims=True)
    vn = (vc * lax.rsqrt(var + EPS) * lng_ref[...] + lnb_ref[...]).astype(BF16)
    t_chunk = lax.broadcasted_iota(jnp.int32, (SGU_BLOCK, SGU_BLOCK), 0) // SGU_CHUNK
    s_chunk = lax.broadcasted_iota(jnp.int32, (SGU_BLOCK, SGU_BLOCK), 1) // SGU_CHUNK
    causal = s_chunk <= t_chunk
    nblk = tm // SGU_BLOCK
    for g in range(SGU_GROUPS):
        cols = slice(g * SGU_GROUP_DIM, (g + 1) * SGU_GROUP_DIM)
        wsg = jnp.where(causal, ws_ref[g], 0.0).astype(BF16)
        vg = jnp.concatenate([vn[j * SGU_BLOCK:(j + 1) * SGU_BLOCK, cols] for j in range(nblk)], axis=1)
        vm = jnp.dot(wsg, vg, preferred_element_type=F32) + bsp_ref[g]
        for j in range(nblk):
            rows = slice(j * SGU_BLOCK, (j + 1) * SGU_BLOCK)
            sb_ref[rows, cols] = (u[rows, cols] * vm[:, j * SGU_GROUP_DIM:(j + 1) * SGU_GROUP_DIM]).astype(BF16)


def _mixer_front(x2d, g_mix, w_in, w_pool, pool_scale, ln_g, ln_b, w_spatial, b_spatial):
    n_tok = x2d.shape[0]
    tm = TM_FRONT
    row = lambda i: (i, 0)
    return pl.pallas_call(
        _mixer_front_kernel,
        grid=(n_tok // tm,),
        in_specs=[
            pl.BlockSpec((tm, D_MODEL), row),
            _resident((1, D_MODEL)),
            _resident((D_MODEL, N_IN)),
            _resident((len(POOL_WINDOWS), POOL_GROUP_DIM, POOL_GROUP_DIM)),
            _resident((1, POOL_WIDTH)),
            _resident((1, SGU_WIDTH)),
            _resident((1, SGU_WIDTH)),
            _resident((SGU_GROUPS, SGU_BLOCK, SGU_BLOCK)),
            _resident((SGU_GROUPS, SGU_BLOCK, 1)),
        ],
        out_specs=[pl.BlockSpec((tm, POOL_WIDTH), row), pl.BlockSpec((tm, SGU_WIDTH), row)],
        out_shape=[jax.ShapeDtypeStruct((n_tok, POOL_WIDTH), BF16),
                   jax.ShapeDtypeStruct((n_tok, SGU_WIDTH), BF16)],
        scratch_shapes=[pltpu.VMEM((POOL_HISTORY, POOL_WIDTH), F32)],
        compiler_params=pltpu.CompilerParams(dimension_semantics=("arbitrary",),
                                             vmem_limit_bytes=VMEM_LIMIT_BYTES),
        name="mixer_front",
    )(x2d, g_mix, w_in, w_pool, pool_scale, ln_g, ln_b, w_spatial, b_spatial)


def _route(logits):
    lane = lax.broadcasted_iota(jnp.int32, logits.shape, 1).astype(F32)
    neg = -jnp.inf
    far = float(LANES)

    def first_argmax(vals):
        top = jnp.max(vals, axis=-1, keepdims=True)
        return top, jnp.min(jnp.where(vals == top, lane, far), axis=-1, keepdims=True)

    is_grp = lane < float(N_EXPERT_GROUPS)
    g_top, g_idx = first_argmax(jnp.where(is_grp, logits, neg))
    g_den = jnp.sum(jnp.where(is_grp, jnp.exp(logits - g_top), 0.0), axis=-1, keepdims=True)
    grp_p = 1.0 / g_den
    lo = float(ROUTER_EXPERT_LANE0) + g_idx * float(EXPERTS_PER_GROUP)
    e_log = jnp.where(lane >= lo, jnp.where(lane < lo + float(EXPERTS_PER_GROUP), logits, neg), neg)
    t1, i1 = first_argmax(e_log)
    t2, i2 = first_argmax(jnp.where(lane == i1, neg, e_log))
    r = jnp.exp(t2 - t1)
    w1 = grp_p / (1.0 + r)
    w2 = grp_p * r / (1.0 + r)
    e1 = i1 - float(ROUTER_EXPERT_LANE0)
    e2 = i2 - float(ROUTER_EXPERT_LANE0)
    eid = jnp.where(lane == 0.0, e1, jnp.where(lane == 1.0, e2, 0.0)).astype(jnp.int32)
    wts = jnp.where(lane == 0.0, w1, jnp.where(lane == 1.0, w2, 0.0))
    return eid, wts


def _mixer_back_kernel(x_ref, za_ref, sb_ref, gmix_ref, wm_ref, bm_ref, wa_ref, wb_ref, wo_ref,
                       gffn_ref, wr_ref, br_ref, x1_ref, h2p_ref, eid_ref, wts_ref):
    x = x_ref[...]
    h = (_rms_scale(x) * gmix_ref[...]).astype(BF16)
    za = za_ref[...]
    sb = sb_ref[...]
    acc = jnp.zeros(x.shape, F32)
    for c in range(D_MODEL // MERGE_CHUNK):
        ca = slice(c * MERGE_CHUNK, (c + 1) * MERGE_CHUNK)
        cb = slice(D_MODEL + c * MERGE_CHUNK, D_MODEL + (c + 1) * MERGE_CHUNK)
        ga = jax.nn.sigmoid(jnp.dot(h, wm_ref[:, ca], preferred_element_type=F32) + bm_ref[:, ca])
        gb = jax.nn.sigmoid(jnp.dot(h, wm_ref[:, cb], preferred_element_type=F32) + bm_ref[:, cb])
        ya = jnp.dot(za, wa_ref[:, ca], preferred_element_type=F32)
        yb = jnp.dot(sb, wb_ref[:, ca], preferred_element_type=F32)
        merged = (ga * ya + gb * yb).astype(BF16)
        acc = acc + jnp.dot(merged, wo_ref[ca, :], preferred_element_type=F32)
    x1 = x + acc
    x1_ref[...] = x1
    h2 = _rms_scale(x1) * gffn_ref[...]
    h2_hi = h2.astype(BF16)
    h2_lo = (h2 - h2_hi.astype(F32)).astype(BF16)
    as_bits = lambda v: lax.bitcast_convert_type(v.astype(F32), jnp.uint32)
    h2p_ref[...] = (as_bits(h2_hi[:, :PACKED_COLS]) >> 16) | as_bits(h2_hi[:, PACKED_COLS:])
    hi_terms = jnp.dot(h2_hi, wr_ref[...], preferred_element_type=F32)
    lo_term = jnp.dot(h2_lo, wr_ref[:, :LANES], preferred_element_type=F32)
    logits = hi_terms[:, :LANES] + hi_terms[:, LANES:] + lo_term + br_ref[...]
    eid, wts = _route(logits)
    eid_ref[...] = eid
    wts_ref[...] = wts


def _mixer_back(x2d, za, sb, g_mix, w_merge, b_merge, w_a, w_b, w_out, g_ffn, w_router, b_router):
    n_tok = x2d.shape[0]
    tm = TM_BACK
    row = lambda i: (i, 0)
    return pl.pallas_call(
        _mixer_back_kernel,
        grid=(n_tok // tm,),
        in_specs=[
            pl.BlockSpec((tm, D_MODEL), row),
            pl.BlockSpec((tm, POOL_WIDTH), row),
            pl.BlockSpec((tm, SGU_WIDTH), row),
            _resident((1, D_MODEL)),
            _resident((D_MODEL, 2 * D_MODEL)),
            _resident((1, 2 * D_MODEL)),
            _resident((POOL_WIDTH, D_MODEL)),
            _resident((SGU_WIDTH, D_MODEL)),
            _resident((D_MODEL, D_MODEL)),
            _resident((1, D_MODEL)),
            _resident((D_MODEL, 2 * LANES)),
            _resident((1, LANES)),
        ],
        out_specs=[pl.BlockSpec((tm, D_MODEL), row), pl.BlockSpec((tm, PACKED_COLS), row),
                   pl.BlockSpec((tm, LANES), row), pl.BlockSpec((tm, LANES), row)],
        out_shape=[jax.ShapeDtypeStruct((n_tok, D_MODEL), F32),
                   jax.ShapeDtypeStruct((n_tok, PACKED_COLS), jnp.uint32),
                   jax.ShapeDtypeStruct((n_tok, LANES), jnp.int32),
                   jax.ShapeDtypeStruct((n_tok, LANES), F32)],
        compiler_params=pltpu.CompilerParams(dimension_semantics=("arbitrary",),
                                             vmem_limit_bytes=VMEM_LIMIT_BYTES),
        name="mixer_back",
    )(x2d, za, sb, g_mix, w_merge, b_merge, w_a, w_b, w_out, g_ffn, w_router, b_router)


def _row_gather_copy(src_hbm, src_row, dst_vmem, dst_row, sem):
    return pltpu.make_async_copy(src_hbm.at[pl.ds(src_row, 1), :], dst_vmem.at[pl.ds(dst_row, 1), :], sem)


def _expert_kernel(be_ref, nexte_ref, nused_ref, rowtok_ref, h2p_hbm, wg_hbm, wu_hbm, wd_hbm,
                   ys_ref, xbuf, sems, wg_st, wu_st, wd_st, wsems, wg_bf, wu_bf, wd_bf):
    rows = MOE_BLOCK_ROWS
    b = pl.program_id(0)
    n_used = nused_ref[0]
    slot = lax.rem(b, GATHER_SLOTS)

    def row_copy(blk, slt, r):
        return _row_gather_copy(h2p_hbm, rowtok_ref[blk * rows + r], xbuf.at[slt], r, sems.at[slt])

    def wait_block(slt):
        pltpu.make_async_copy(h2p_hbm.at[pl.ds(0, rows), :], xbuf.at[slt], sems.at[slt]).wait()

    def weight_copies(e):
        copies = []
        for j, (src, dst) in enumerate(((wg_hbm, wg_st), (wu_hbm, wu_st), (wd_hbm, wd_st))):
            chunk = dst.shape[0] // WEIGHT_DMA_CHUNKS
            for c in range(WEIGHT_DMA_CHUNKS):
                part = pl.ds(c * chunk, chunk)
                copies.append((pltpu.make_async_copy(src.at[e, part, :], dst.at[part, :], wsems.at[j]),
                               c % N_DMA_QUEUES))
        return copies

    last_used = n_used - 1

    @pl.when(b == 0)
    def _():
        for ahead in range(GATHER_AHEAD):
            def body(r, carry, ahead=ahead):
                row_copy(jnp.minimum(ahead, last_used), ahead, r).start()
                return carry
            lax.fori_loop(0, rows, body, 0, unroll=GATHER_UNROLL)

    @pl.when(b < n_used)
    def _():
        @pl.when(b == 0)
        def _():
            for cp, queue in weight_copies(be_ref[0]):
                cp.start(priority=queue)

        @pl.when(jnp.logical_or(b == 0, be_ref[b] != be_ref[jnp.maximum(b - 1, 0)]))
        def _():
            for cp, _ in weight_copies(be_ref[b]):
                cp.wait()
            wg_bf[...] = wg_st[...].astype(BF16)
            wu_bf[...] = wu_st[...].astype(BF16)
            wd_bf[...] = wd_st[...].astype(BF16)
            nxt_e = nexte_ref[b]

            @pl.when(nxt_e >= 0)
            def _():
                for cp, queue in weight_copies(nxt_e):
                    cp.start(priority=queue)

        wait_block(slot)
        words = xbuf[slot]
        from_bits = lambda v: lax.bitcast_convert_type(v, F32).astype(BF16)
        h2 = jnp.concatenate([from_bits(words << 16), from_bits(words & jnp.uint32(0xFFFF0000))], axis=1)
        ahead_blk = jnp.minimum(b + GATHER_AHEAD, last_used)
        ahead_slot = lax.rem(b + GATHER_AHEAD, GATHER_SLOTS)
        for r in range(rows):
            row_copy(ahead_blk, ahead_slot, r).start()
        gate = jnp.dot(h2, wg_bf[...], preferred_element_type=F32)
        up = jnp.dot(h2, wu_bf[...], preferred_element_type=F32)
        hid = (jax.nn.silu(gate) * up).astype(BF16)
        ys_ref[...] = jnp.dot(hid, wd_bf[...], preferred_element_type=F32)

        @pl.when(b == last_used)
        def _():
            for ahead in range(1, GATHER_SLOTS):
                wait_block(lax.rem(b + ahead, GATHER_SLOTS))

    @pl.when(b >= n_used)
    def _():
        ys_ref[...] = jnp.zeros_like(ys_ref)


def _expert_ffn(block_e, next_e, n_used, row_tok, h2p, w_g, w_u, w_d):
    n_rows = row_tok.shape[0]
    n_blocks = n_rows // MOE_BLOCK_ROWS
    grid_spec = pltpu.PrefetchScalarGridSpec(
        num_scalar_prefetch=4,
        grid=(n_blocks,),
        in_specs=[
            pl.BlockSpec(memory_space=pl.ANY),
            pl.BlockSpec(memory_space=pl.ANY),
            pl.BlockSpec(memory_space=pl.ANY),
            pl.BlockSpec(memory_space=pl.ANY),
        ],
        out_specs=pl.BlockSpec((MOE_BLOCK_ROWS, D_MODEL), lambda b, *_: (b, 0)),
        scratch_shapes=[
            pltpu.VMEM((GATHER_SLOTS, MOE_BLOCK_ROWS, PACKED_COLS), jnp.uint32),
            pltpu.SemaphoreType.DMA((GATHER_SLOTS,)),
            pltpu.VMEM((D_MODEL, D_EXPERT), F32),
            pltpu.VMEM((D_MODEL, D_EXPERT), F32),
            pltpu.VMEM((D_EXPERT, D_MODEL), F32),
            pltpu.SemaphoreType.DMA((3,)),
            pltpu.VMEM((D_MODEL, D_EXPERT), BF16),
            pltpu.VMEM((D_MODEL, D_EXPERT), BF16),
            pltpu.VMEM((D_EXPERT, D_MODEL), BF16),
        ],
    )
    return pl.pallas_call(
        _expert_kernel,
        grid_spec=grid_spec,
        out_shape=jax.ShapeDtypeStruct((n_rows, D_MODEL), F32),
        compiler_params=pltpu.CompilerParams(dimension_semantics=("arbitrary",),
                                             vmem_limit_bytes=VMEM_LIMIT_BYTES),
        name="expert_ffn",
    )(block_e, next_e, n_used, row_tok, h2p, w_g, w_u, w_d)


def _tail_kernel(dest_ref, x1_ref, wts_ref, ys_hbm, p_ref, gple_ref, wpg_ref, bpg_ref, wpu_ref, gfin_ref,
                 out_ref, ybuf, sems):
    tm = x1_ref.shape[0]
    i = pl.program_id(0)
    last = pl.num_programs(0) - 1
    slot = lax.rem(i, 2)

    def start_token(step, slt, r):
        for k in range(TOP_K):
            _row_gather_copy(ys_hbm, dest_ref[TOP_K * (step * tm + r) + k], ybuf.at[slt], k * tm + r,
                             sems.at[slt]).start()

    def wait_tile(slt):
        pltpu.make_async_copy(ys_hbm.at[pl.ds(0, TOP_K * tm), :], ybuf.at[slt], sems.at[slt]).wait()

    @pl.when(i == 0)
    def _():
        def body(r, carry):
            start_token(0, 0, r)
            return carry
        lax.fori_loop(0, tm, body, 0, unroll=GATHER_UNROLL)

    wait_tile(slot)
    nxt = jnp.minimum(i + 1, last)
    other = 1 - slot
    wts = wts_ref[...]
    x2 = x1_ref[...]
    for k in range(TOP_K):
        x2 = x2 + wts[:, k:k + 1] * ybuf[slot, k * tm:(k + 1) * tm, :]
    hn = (_rms_scale(x2) * gple_ref[...]).astype(BF16)
    up = jnp.dot(p_ref[...].astype(BF16), wpu_ref[...], preferred_element_type=F32)
    n_slabs = D_MODEL // MXU_COLS
    per = tm // n_slabs
    x3 = []
    for n in range(n_slabs):
        cols = slice(n * MXU_COLS, (n + 1) * MXU_COLS)
        for r in range(n * per, (n + 1) * per):
            start_token(nxt, other, r)
        gate = jax.nn.sigmoid(jnp.dot(hn, wpg_ref[:, cols], preferred_element_type=F32) + bpg_ref[:, cols])
        x3.append(x2[:, cols] + gate * up[:, cols])
    x3 = jnp.concatenate(x3, axis=1)
    out_ref[...] = _rms_scale(x3) * gfin_ref[...]

    @pl.when(i == last)
    def _():
        wait_tile(other)


def _tail(dest, x1, wts, ys, p2d, g_ple, w_pg, b_pg, w_pu, g_final):
    n_tok = x1.shape[0]
    tm = TM_TAIL
    row = lambda i, *_: (i, 0)
    const = lambda i, *_: (0, 0)
    grid_spec = pltpu.PrefetchScalarGridSpec(
        num_scalar_prefetch=1,
        grid=(n_tok // tm,),
        in_specs=[
            pl.BlockSpec((tm, D_MODEL), row),
            pl.BlockSpec((tm, LANES), row),
            pl.BlockSpec(memory_space=pl.ANY),
            pl.BlockSpec((tm, PLE_DIM), row),
            pl.BlockSpec((1, D_MODEL), const),
            pl.BlockSpec((D_MODEL, D_MODEL), const, pipeline_mode=pl.Buffered(1)),
            pl.BlockSpec((1, D_MODEL), const),
            pl.BlockSpec((PLE_DIM, D_MODEL), const),
            pl.BlockSpec((1, D_MODEL), const),
        ],
        out_specs=pl.BlockSpec((tm, D_MODEL), row),
        scratch_shapes=[pltpu.VMEM((2, TOP_K * tm, D_MODEL), F32), pltpu.SemaphoreType.DMA((2,))],
    )
    return pl.pallas_call(
        _tail_kernel,
        grid_spec=grid_spec,
        out_shape=jax.ShapeDtypeStruct((n_tok, D_MODEL), F32),
        compiler_params=pltpu.CompilerParams(dimension_semantics=("arbitrary",),
                                             vmem_limit_bytes=VMEM_LIMIT_BYTES),
        name="tail",
    )(dest, x1, wts, ys, p2d, g_ple, w_pg, b_pg, w_pu, g_final)


def _dispatch_plan(expert_id):
    n_tok = expert_id.shape[0]
    n_assign = n_tok * TOP_K
    n_blocks = -(-n_assign // MOE_BLOCK_ROWS) + N_EXPERTS
    n_rows = n_blocks * MOE_BLOCK_ROWS
    i32 = jnp.int32
    flat_e = expert_id.reshape(-1)
    experts = jnp.arange(N_EXPERTS, dtype=i32)
    assign = jnp.arange(n_assign, dtype=i32)
    se, order = lax.sort((flat_e, assign), num_keys=1)
    onehot_sorted = se[:, None] == experts[None, :]
    counts = jnp.sum(onehot_sorted.astype(i32), axis=0)
    padded = (counts + MOE_BLOCK_ROWS - 1) // MOE_BLOCK_ROWS * MOE_BLOCK_ROWS
    pad_end = jnp.cumsum(padded)
    pad_start = pad_end - padded
    start = jnp.cumsum(counts) - counts
    row_of_sorted = assign + jnp.sum(jnp.where(onehot_sorted, (pad_start - start)[None, :], 0), axis=1)
    _, dest = lax.sort((order, row_of_sorted), num_keys=1)
    n_used = pad_end[-1] // MOE_BLOCK_ROWS
    rows = jnp.arange(n_rows, dtype=i32)
    row_e = jnp.minimum(jnp.sum((pad_end[None, :] <= rows[:, None]).astype(i32), axis=1), N_EXPERTS - 1)
    onehot_row = row_e[:, None] == experts[None, :]
    pick = lambda table: jnp.sum(jnp.where(onehot_row, table[None, :], 0), axis=1)
    offset = rows - pick(pad_start)
    valid = offset < pick(counts)
    src = order[jnp.clip(pick(start) + offset, 0, n_assign - 1)]
    row_tok = jnp.where(valid, src // TOP_K, 0)
    block_e = row_e.reshape(n_blocks, MOE_BLOCK_ROWS)[:, 0]
    block_e = jnp.where(jnp.arange(n_blocks) < n_used, block_e, block_e[jnp.maximum(n_used - 1, 0)])
    later_used = jnp.logical_and(experts[None, :] > experts[:, None], (counts > 0)[None, :])
    next_used = jnp.min(jnp.where(later_used, experts[None, :], N_EXPERTS), axis=1)
    next_used = jnp.where(next_used < N_EXPERTS, next_used, -1)
    next_e = jnp.sum(jnp.where(block_e[:, None] == experts[None, :], next_used[None, :], 0), axis=1)
    return (block_e.astype(i32), next_e.astype(i32), n_used.astype(i32).reshape(1), row_tok.astype(i32),
            dest.astype(i32))


def kernel(x, p, g_mix, w_in, w_pool, pool_scale, w_branch_a, sgu_ln_g, sgu_ln_b, w_spatial, b_spatial, w_branch_b, w_merge_gate, b_merge_gate, w_out, g_ffn, w_router_group, b_router_group, w_router_expert, b_router_expert, w_exp_gate, w_exp_up, w_exp_down, g_ple, w_ple_gate, b_ple_gate, w_ple_up, g_final):
    bsz, seq, d = x.shape
    assert (seq, d) == (SEQ, D_MODEL) and g_mix.shape[0] == 1
    n_tok = bsz * seq
    x2d = x.reshape(n_tok, d)
    row2d = lambda v: v.reshape(1, -1)

    za, sb = _mixer_front(
        x2d, row2d(g_mix[0]), w_in[0].astype(BF16), w_pool[0].astype(BF16), row2d(pool_scale[0]),
        row2d(sgu_ln_g[0]), row2d(sgu_ln_b[0]), w_spatial[0], b_spatial[0][:, :, None])

    w_router = jnp.concatenate(
        [w_router_group[0], jnp.transpose(w_router_expert[0], (1, 0, 2)).reshape(d, N_EXPERTS)], axis=1)
    b_router = jnp.concatenate([b_router_group[0], b_router_expert[0].reshape(N_EXPERTS)])
    pad = LANES - w_router.shape[1]
    w_router = jnp.pad(w_router, ((0, 0), (0, pad)))
    b_router = jnp.pad(b_router, (0, pad))
    w_router_hi = w_router.astype(BF16)
    w_router_lo = (w_router - w_router_hi.astype(F32)).astype(BF16)
    w_router_split = jnp.concatenate([w_router_hi, w_router_lo], axis=1)

    x1, h2p, eid, wts = _mixer_back(
        x2d, za, sb, row2d(g_mix[0]), w_merge_gate[0].astype(BF16), row2d(b_merge_gate[0]),
        w_branch_a[0].astype(BF16), w_branch_b[0].astype(BF16), w_out[0].astype(BF16), row2d(g_ffn[0]),
        w_router_split, row2d(b_router))

    block_e, next_e, n_used, row_tok, dest = _dispatch_plan(eid[:, :TOP_K])
    ys = _expert_ffn(block_e, next_e, n_used, row_tok, h2p, w_exp_gate[0], w_exp_up[0], w_exp_down[0])
    out = _tail(dest, x1, wts, ys, p[0].reshape(n_tok, PLE_DIM), row2d(g_ple[0]), w_ple_gate[0].astype(BF16),
                row2d(b_ple_gate[0]), w_ple_up[0].astype(BF16), row2d(g_final))
    return out.reshape(bsz, seq, d)
```

```python
import jax
import jax.numpy as jnp
from jax import lax
from jax.experimental import pallas as pl
from jax.experimental.pallas import tpu as pltpu

F32 = jnp.float32
BF16 = jnp.bfloat16

D_MODEL = 2048
SEQ = 4096
EPS = 1e-6
PLE_DIM = 256
POOL_WINDOWS = (2, 4, 8, 16)
POOL_WIDTH = D_MODEL // 2
POOL_GROUP_DIM = POOL_WIDTH // len(POOL_WINDOWS)
POOL_HISTORY = max(POOL_WINDOWS)
SGU_BLOCK = 128
SGU_CHUNK = 64
SGU_GROUPS = 8
SGU_WIDTH = D_MODEL // 2
SGU_GROUP_DIM = SGU_WIDTH // SGU_GROUPS
N_IN = POOL_WIDTH + 2 * SGU_WIDTH
N_EXPERT_GROUPS = 4
EXPERTS_PER_GROUP = 8
N_EXPERTS = N_EXPERT_GROUPS * EXPERTS_PER_GROUP
TOP_K = 2
D_EXPERT = D_MODEL // 4
MOE_BLOCK_ROWS = 256

LANES = 128
MXU_COLS = 256
ROUTER_EXPERT_LANE0 = N_EXPERT_GROUPS
VMEM_LIMIT_BYTES = 56 * 1024 * 1024

TM_FRONT = 256
TM_BACK = 256
TM_TAIL = 256
MERGE_CHUNK = 512
WEIGHT_DMA_PRIORITY = 1
GATHER_AHEAD = 2
GATHER_SLOTS = GATHER_AHEAD + 1
PACKED_COLS = D_MODEL // 2
GATHER_UNROLL = 8


def _rms_scale(x):
    return x * lax.rsqrt(jnp.mean(x * x, axis=-1, keepdims=True) + EPS)


def _pack_bf16_halves(v):
    bits = lambda part: lax.bitcast_convert_type(part.astype(BF16).astype(F32), jnp.uint32)
    return (bits(v[:, :PACKED_COLS]) >> 16) | bits(v[:, PACKED_COLS:])


def _unpack_bf16_halves(words):
    return (lax.bitcast_convert_type(words << 16, F32),
            lax.bitcast_convert_type(words & jnp.uint32(0xFFFF0000), F32))


def _resident(shape):
    zeros = (0,) * len(shape)
    return pl.BlockSpec(shape, lambda *_: zeros, pipeline_mode=pl.Buffered(1))


def _mixer_front_kernel(x_ref, gmix_ref, win_ref, wpool_ref, pscale_ref, lng_ref, lnb_ref, ws_ref,
                        bsp_ref, za_ref, sb_ref, hist_ref):
    tm = x_ref.shape[0]
    tiles_per_seq = SEQ // tm
    seq_tile = lax.rem(pl.program_id(0), tiles_per_seq)

    @pl.when(seq_tile == 0)
    def _():
        hist_ref[...] = jnp.zeros_like(hist_ref)

    h = (_rms_scale(x_ref[...]) * gmix_ref[...]).astype(BF16)
    z = jnp.dot(h, win_ref[...], preferred_element_type=F32)

    a = z[:, :POOL_WIDTH]
    ext = jnp.concatenate([hist_ref[...], a], axis=0)
    hist_ref[...] = a[tm - POOL_HISTORY:, :]
    frames = (seq_tile * tm + 1 + lax.broadcasted_iota(jnp.int32, (tm, 1), 0)).astype(F32)
    for gi, w in enumerate(POOL_WINDOWS):
        cols = slice(gi * POOL_GROUP_DIM, (gi + 1) * POOL_GROUP_DIM)
        s = ext[:, cols]
        k = 1
        while k < w:
            s = s + pltpu.roll(s, k, 0)
            k *= 2
        wsum = s[POOL_HISTORY:, :]
        zg = wsum / jnp.minimum(frames, float(w)) - a[:, cols]
        yg = jnp.dot(zg.astype(BF16), wpool_ref[gi], preferred_element_type=F32)
        za_ref[:, cols] = (yg * pscale_ref[:, cols]).astype(BF16)

    u = jax.nn.gelu(z[:, POOL_WIDTH:POOL_WIDTH + SGU_WIDTH])
    v = jax.nn.gelu(z[:, POOL_WIDTH + SGU_WIDTH:])
    vc = v - jnp.mean(v, axis=-1, keepdims=True)
    var = jnp.mean(vc * vc, axis=-1, keepdims=True)
    vn = (vc * lax.rsqrt(var + EPS) * lng_ref[...] + lnb_ref[...]).astype(BF16)
    t_chunk = lax.broadcasted_iota(jnp.int32, (SGU_BLOCK, SGU_BLOCK), 0) // SGU_CHUNK
    s_chunk = lax.broadcasted_iota(jnp.int32, (SGU_BLOCK, SGU_BLOCK), 1) // SGU_CHUNK
    causal = s_chunk <= t_chunk
    nblk = tm // SGU_BLOCK
    for g in range(SGU_GROUPS):
        cols = slice(g * SGU_GROUP_DIM, (g + 1) * SGU_GROUP_DIM)
        wsg = jnp.where(causal, ws_ref[g], 0.0).astype(BF16)
        vg = jnp.concatenate([vn[j * SGU_BLOCK:(j + 1) * SGU_BLOCK, cols] for j in range(nblk)], axis=1)
        vm = jnp.dot(wsg, vg, preferred_element_type=F32) + bsp_ref[g]
        for j in range(nblk):
            rows = slice(j * SGU_BLOCK, (j + 1) * SGU_BLOCK)
            sb_ref[rows, cols] = (u[rows, cols] * vm[:, j * SGU_GROUP_DIM:(j + 1) * SGU_GROUP_DIM]).astype(BF16)


def _mixer_front(x2d, g_mix, w_in, w_pool, pool_scale, ln_g, ln_b, w_spatial, b_spatial):
    n_tok = x2d.shape[0]
    tm = TM_FRONT
    row = lambda i: (i, 0)
    return pl.pallas_call(
        _mixer_front_kernel,
        grid=(n_tok // tm,),
        in_specs=[
            pl.BlockSpec((tm, D_MODEL), row),
            _resident((1, D_MODEL)),
            _resident((D_MODEL, N_IN)),
            _resident((len(POOL_WINDOWS), POOL_GROUP_DIM, POOL_GROUP_DIM)),
            _resident((1, POOL_WIDTH)),
            _resident((1, SGU_WIDTH)),
            _resident((1, SGU_WIDTH)),
            _resident((SGU_GROUPS, SGU_BLOCK, SGU_BLOCK)),
            _resident((SGU_GROUPS, SGU_BLOCK, 1)),
        ],
        out_specs=[pl.BlockSpec((tm, POOL_WIDTH), row), pl.BlockSpec((tm, SGU_WIDTH), row)],
        out_shape=[jax.ShapeDtypeStruct((n_tok, POOL_WIDTH), BF16),
                   jax.ShapeDtypeStruct((n_tok, SGU_WIDTH), BF16)],
        scratch_shapes=[pltpu.VMEM((POOL_HISTORY, POOL_WIDTH), F32)],
        compiler_params=pltpu.CompilerParams(dimension_semantics=("arbitrary",),
                                             vmem_limit_bytes=VMEM_LIMIT_BYTES),
        name="mixer_front",
    )(x2d, g_mix, w_in, w_pool, pool_scale, ln_g, ln_b, w_spatial, b_spatial)


def _route(logits):
    lane = lax.broadcasted_iota(jnp.int32, logits.shape, 1).astype(F32)
    neg = -jnp.inf
    far = float(LANES)

    def first_argmax(vals):
        top = jnp.max(vals, axis=-1, keepdims=True)
        return top, jnp.min(jnp.where(vals == top, lane, far), axis=-1, keepdims=True)

    is_grp = lane < float(N_EXPERT_GROUPS)
    g_top, g_idx = first_argmax(jnp.where(is_grp, logits, neg))
    g_den = jnp.sum(jnp.where(is_grp, jnp.exp(logits - g_top), 0.0), axis=-1, keepdims=True)
    grp_p = 1.0 / g_den
    lo = float(ROUTER_EXPERT_LANE0) + g_idx * float(EXPERTS_PER_GROUP)
    e_log = jnp.where(lane >= lo, jnp.where(lane < lo + float(EXPERTS_PER_GROUP), logits, neg), neg)
    t1, i1 = first_argmax(e_log)
    t2, i2 = first_argmax(jnp.where(lane == i1, neg, e_log))
    r = jnp.exp(t2 - t1)
    w1 = grp_p / (1.0 + r)
    w2 = grp_p * r / (1.0 + r)
    e1 = i1 - float(ROUTER_EXPERT_LANE0)
    e2 = i2 - float(ROUTER_EXPERT_LANE0)
    eid = jnp.where(lane == 0.0, e1, jnp.where(lane == 1.0, e2, 0.0)).astype(jnp.int32)
    wts = jnp.where(lane == 0.0, w1, jnp.where(lane == 1.0, w2, 0.0))
    return eid, wts


def _mixer_back_kernel(x_ref, za_ref, sb_ref, gmix_ref, wm_ref, bm_ref, wa_ref, wb_ref, wo_ref,
                       gffn_ref, wr_ref, br_ref, x1_ref, h2p_ref, eid_ref, wts_ref):
    x = x_ref[...]
    h = (_rms_scale(x) * gmix_ref[...]).astype(BF16)
    za = za_ref[...]
    sb = sb_ref[...]
    acc = jnp.zeros(x.shape, F32)
    for c in range(D_MODEL // MERGE_CHUNK):
        ca = slice(c * MERGE_CHUNK, (c + 1) * MERGE_CHUNK)
        cb = slice(D_MODEL + c * MERGE_CHUNK, D_MODEL + (c + 1) * MERGE_CHUNK)
        ga = jax.nn.sigmoid(jnp.dot(h, wm_ref[:, ca], preferred_element_type=F32) + bm_ref[:, ca])
        gb = jax.nn.sigmoid(jnp.dot(h, wm_ref[:, cb], preferred_element_type=F32) + bm_ref[:, cb])
        ya = jnp.dot(za, wa_ref[:, ca], preferred_element_type=F32)
        yb = jnp.dot(sb, wb_ref[:, ca], preferred_element_type=F32)
        merged = (ga * ya + gb * yb).astype(BF16)
        acc = acc + jnp.dot(merged, wo_ref[ca, :], preferred_element_type=F32)
    x1 = x + acc
    x1_ref[...] = x1
    h2 = _rms_scale(x1) * gffn_ref[...]
    h2_hi = h2.astype(BF16)
    h2_lo = (h2 - h2_hi.astype(F32)).astype(BF16)
    h2p_ref[...] = _pack_bf16_halves(h2)
    hi_terms = jnp.dot(h2_hi, wr_ref[...], preferred_element_type=F32)
    lo_term = jnp.dot(h2_lo, wr_ref[:, :LANES], preferred_element_type=F32)
    logits = hi_terms[:, :LANES] + hi_terms[:, LANES:] + lo_term + br_ref[...]
    eid, wts = _route(logits)
    eid_ref[...] = eid
    wts_ref[...] = wts


def _mixer_back(x2d, za, sb, g_mix, w_merge, b_merge, w_a, w_b, w_out, g_ffn, w_router, b_router):
    n_tok = x2d.shape[0]
    tm = TM_BACK
    row = lambda i: (i, 0)
    return pl.pallas_call(
        _mixer_back_kernel,
        grid=(n_tok // tm,),
        in_specs=[
            pl.BlockSpec((tm, D_MODEL), row),
            pl.BlockSpec((tm, POOL_WIDTH), row),
            pl.BlockSpec((tm, SGU_WIDTH), row),
            _resident((1, D_MODEL)),
            _resident((D_MODEL, 2 * D_MODEL)),
            _resident((1, 2 * D_MODEL)),
            _resident((POOL_WIDTH, D_MODEL)),
            _resident((SGU_WIDTH, D_MODEL)),
            _resident((D_MODEL, D_MODEL)),
            _resident((1, D_MODEL)),
            _resident((D_MODEL, 2 * LANES)),
            _resident((1, LANES)),
        ],
        out_specs=[pl.BlockSpec((tm, D_MODEL), row), pl.BlockSpec((tm, PACKED_COLS), row),
                   pl.BlockSpec((tm, LANES), row), pl.BlockSpec((tm, LANES), row)],
        out_shape=[jax.ShapeDtypeStruct((n_tok, D_MODEL), F32),
                   jax.ShapeDtypeStruct((n_tok, PACKED_COLS), jnp.uint32),
                   jax.ShapeDtypeStruct((n_tok, LANES), jnp.int32),
                   jax.ShapeDtypeStruct((n_tok, LANES), F32)],
        compiler_params=pltpu.CompilerParams(dimension_semantics=("arbitrary",),
                                             vmem_limit_bytes=VMEM_LIMIT_BYTES),
        name="mixer_back",
    )(x2d, za, sb, g_mix, w_merge, b_merge, w_a, w_b, w_out, g_ffn, w_router, b_router)


def _row_gather_copy(src_hbm, src_row, dst_vmem, dst_row, sem):
    return pltpu.make_async_copy(src_hbm.at[pl.ds(src_row, 1), :], dst_vmem.at[pl.ds(dst_row, 1), :], sem)


def _expert_kernel(be_ref, nexte_ref, nused_ref, rowtok_ref, h2p_hbm, wg_hbm, wu_hbm, wd_hbm,
                   ys_ref, xbuf, sems, wg_st, wu_st, wd_st, wsems, wg_bf, wu_bf, wd_bf):
    rows = MOE_BLOCK_ROWS
    b = pl.program_id(0)
    n_used = nused_ref[0]
    slot = lax.rem(b, GATHER_SLOTS)

    def row_copy(blk, slt, r):
        return _row_gather_copy(h2p_hbm, rowtok_ref[blk * rows + r], xbuf.at[slt], r, sems.at[slt])

    def wait_block(slt):
        pltpu.make_async_copy(h2p_hbm.at[pl.ds(0, rows), :], xbuf.at[slt], sems.at[slt]).wait()

    def weight_copies(e):
        return [pltpu.make_async_copy(src.at[e], dst, wsems.at[j])
                for j, (src, dst) in enumerate(((wg_hbm, wg_st), (wu_hbm, wu_st), (wd_hbm, wd_st)))]

    last_used = n_used - 1

    @pl.when(b == 0)
    def _():
        for ahead in range(GATHER_AHEAD):
            def body(r, carry, ahead=ahead):
                row_copy(jnp.minimum(ahead, last_used), ahead, r).start()
                return carry
            lax.fori_loop(0, rows, body, 0, unroll=GATHER_UNROLL)

    @pl.when(b < n_used)
    def _():
        @pl.when(b == 0)
        def _():
            for cp in weight_copies(be_ref[0]):
                cp.start()

        @pl.when(jnp.logical_or(b == 0, be_ref[b] != be_ref[jnp.maximum(b - 1, 0)]))
        def _():
            for cp in weight_copies(be_ref[b]):
                cp.wait()
            wg_bf[...] = wg_st[...].astype(BF16)
            wu_bf[...] = wu_st[...].astype(BF16)
            wd_bf[...] = wd_st[...].astype(BF16)
            nxt_e = nexte_ref[b]

            @pl.when(nxt_e >= 0)
            def _():
                for cp in weight_copies(nxt_e):
                    cp.start(priority=WEIGHT_DMA_PRIORITY)

        wait_block(slot)
        h2 = jnp.concatenate([half.astype(BF16) for half in _unpack_bf16_halves(xbuf[slot])], axis=1)
        ahead_blk = jnp.minimum(b + GATHER_AHEAD, last_used)
        ahead_slot = lax.rem(b + GATHER_AHEAD, GATHER_SLOTS)
        for r in range(rows):
            row_copy(ahead_blk, ahead_slot, r).start()
        gate = jnp.dot(h2, wg_bf[...], preferred_element_type=F32)
        up = jnp.dot(h2, wu_bf[...], preferred_element_type=F32)
        hid = (jax.nn.silu(gate) * up).astype(BF16)
        y = jnp.dot(hid, wd_bf[...], preferred_element_type=F32)
        ys_ref[...] = _pack_bf16_halves(y)

        @pl.when(b == last_used)
        def _():
            for ahead in range(1, GATHER_SLOTS):
                wait_block(lax.rem(b + ahead, GATHER_SLOTS))

    @pl.when(b >= n_used)
    def _():
        ys_ref[...] = jnp.zeros_like(ys_ref)


def _expert_ffn(block_e, next_e, n_used, row_tok, h2p, w_g, w_u, w_d):
    n_rows = row_tok.shape[0]
    n_blocks = n_rows // MOE_BLOCK_ROWS
    grid_spec = pltpu.PrefetchScalarGridSpec(
        num_scalar_prefetch=4,
        grid=(n_blocks,),
        in_specs=[
            pl.BlockSpec(memory_space=pl.ANY),
            pl.BlockSpec(memory_space=pl.ANY),
            pl.BlockSpec(memory_space=pl.ANY),
            pl.BlockSpec(memory_space=pl.ANY),
        ],
        out_specs=pl.BlockSpec((MOE_BLOCK_ROWS, PACKED_COLS), lambda b, *_: (b, 0)),
        scratch_shapes=[
            pltpu.VMEM((GATHER_SLOTS, MOE_BLOCK_ROWS, PACKED_COLS), jnp.uint32),
            pltpu.SemaphoreType.DMA((GATHER_SLOTS,)),
            pltpu.VMEM((D_MODEL, D_EXPERT), F32),
            pltpu.VMEM((D_MODEL, D_EXPERT), F32),
            pltpu.VMEM((D_EXPERT, D_MODEL), F32),
            pltpu.SemaphoreType.DMA((3,)),
            pltpu.VMEM((D_MODEL, D_EXPERT), BF16),
            pltpu.VMEM((D_MODEL, D_EXPERT), BF16),
            pltpu.VMEM((D_EXPERT, D_MODEL), BF16),
        ],
    )
    return pl.pallas_call(
        _expert_kernel,
        grid_spec=grid_spec,
        out_shape=jax.ShapeDtypeStruct((n_rows, PACKED_COLS), jnp.uint32),
        compiler_params=pltpu.CompilerParams(dimension_semantics=("arbitrary",),
                                             vmem_limit_bytes=VMEM_LIMIT_BYTES),
        name="expert_ffn",
    )(block_e, next_e, n_used, row_tok, h2p, w_g, w_u, w_d)


def _tail_kernel(dest_ref, x1_ref, wts_ref, ys_hbm, p_ref, gple_ref, wpg_ref, bpg_ref, wpu_ref, gfin_ref,
                 out_ref, ybuf, sems):
    tm = x1_ref.shape[0]
    i = pl.program_id(0)
    last = pl.num_programs(0) - 1
    slot = lax.rem(i, 2)

    def start_token(step, slt, r):
        for k in range(TOP_K):
            _row_gather_copy(ys_hbm, dest_ref[TOP_K * (step * tm + r) + k], ybuf.at[slt], k * tm + r,
                             sems.at[slt]).start()

    def wait_tile(slt):
        pltpu.make_async_copy(ys_hbm.at[pl.ds(0, TOP_K * tm), :], ybuf.at[slt], sems.at[slt]).wait()

    @pl.when(i == 0)
    def _():
        def body(r, carry):
            start_token(0, 0, r)
            return carry
        lax.fori_loop(0, tm, body, 0, unroll=GATHER_UNROLL)

    wait_tile(slot)
    nxt = jnp.minimum(i + 1, last)
    other = 1 - slot
    wts = wts_ref[...]
    x2 = x1_ref[...]
    for k in range(TOP_K):
        y = jnp.concatenate(_unpack_bf16_halves(ybuf[slot, k * tm:(k + 1) * tm, :]), axis=1)
        x2 = x2 + wts[:, k:k + 1] * y
    hn = (_rms_scale(x2) * gple_ref[...]).astype(BF16)
    up = jnp.dot(p_ref[...].astype(BF16), wpu_ref[...], preferred_element_type=F32)
    n_slabs = D_MODEL // MXU_COLS
    per = tm // n_slabs
    x3 = []
    for n in range(n_slabs):
        cols = slice(n * MXU_COLS, (n + 1) * MXU_COLS)
        for r in range(n * per, (n + 1) * per):
            start_token(nxt, other, r)
        gate = jax.nn.sigmoid(jnp.dot(hn, wpg_ref[:, cols], preferred_element_type=F32) + bpg_ref[:, cols])
        x3.append(x2[:, cols] + gate * up[:, cols])
    x3 = jnp.concatenate(x3, axis=1)
    out_ref[...] = _rms_scale(x3) * gfin_ref[...]

    @pl.when(i == last)
    def _():
        wait_tile(other)


def _tail(dest, x1, wts, ys, p2d, g_ple, w_pg, b_pg, w_pu, g_final):
    n_tok = x1.shape[0]
    tm = TM_TAIL
    row = lambda i, *_: (i, 0)
    const = lambda i, *_: (0, 0)
    grid_spec = pltpu.PrefetchScalarGridSpec(
        num_scalar_prefetch=1,
        grid=(n_tok // tm,),
        in_specs=[
            pl.BlockSpec((tm, D_MODEL), row),
            pl.BlockSpec((tm, LANES), row),
            pl.BlockSpec(memory_space=pl.ANY),
            pl.BlockSpec((tm, PLE_DIM), row),
            pl.BlockSpec((1, D_MODEL), const),
            pl.BlockSpec((D_MODEL, D_MODEL), const, pipeline_mode=pl.Buffered(1)),
            pl.BlockSpec((1, D_MODEL), const),
            pl.BlockSpec((PLE_DIM, D_MODEL), const),
            pl.BlockSpec((1, D_MODEL), const),
        ],
        out_specs=pl.BlockSpec((tm, D_MODEL), row),
        scratch_shapes=[pltpu.VMEM((2, TOP_K * tm, PACKED_COLS), jnp.uint32), pltpu.SemaphoreType.DMA((2,))],
    )
    return pl.pallas_call(
        _tail_kernel,
        grid_spec=grid_spec,
        out_shape=jax.ShapeDtypeStruct((n_tok, D_MODEL), F32),
        compiler_params=pltpu.CompilerParams(dimension_semantics=("arbitrary",),
                                             vmem_limit_bytes=VMEM_LIMIT_BYTES),
        name="tail",
    )(dest, x1, wts, ys, p2d, g_ple, w_pg, b_pg, w_pu, g_final)


def _dispatch_plan(expert_id):
    n_tok = expert_id.shape[0]
    n_assign = n_tok * TOP_K
    n_blocks = -(-n_assign // MOE_BLOCK_ROWS) + N_EXPERTS
    n_rows = n_blocks * MOE_BLOCK_ROWS
    i32 = jnp.int32
    flat_e = expert_id.reshape(-1)
    experts = jnp.arange(N_EXPERTS, dtype=i32)
    assign = jnp.arange(n_assign, dtype=i32)
    se, order = lax.sort((flat_e, assign), num_keys=1)
    onehot_sorted = se[:, None] == experts[None, :]
    counts = jnp.sum(onehot_sorted.astype(i32), axis=0)
    padded = (counts + MOE_BLOCK_ROWS - 1) // MOE_BLOCK_ROWS * MOE_BLOCK_ROWS
    pad_end = jnp.cumsum(padded)
    pad_start = pad_end - padded
    start = jnp.cumsum(counts) - counts
    row_of_sorted = assign + jnp.sum(jnp.where(onehot_sorted, (pad_start - start)[None, :], 0), axis=1)
    _, dest = lax.sort((order, row_of_sorted), num_keys=1)
    n_used = pad_end[-1] // MOE_BLOCK_ROWS
    rows = jnp.arange(n_rows, dtype=i32)
    row_e = jnp.minimum(jnp.sum((pad_end[None, :] <= rows[:, None]).astype(i32), axis=1), N_EXPERTS - 1)
    onehot_row = row_e[:, None] == experts[None, :]
    pick = lambda table: jnp.sum(jnp.where(onehot_row, table[None, :], 0), axis=1)
    offset = rows - pick(pad_start)
    valid = offset < pick(counts)
    src = order[jnp.clip(pick(start) + offset, 0, n_assign - 1)]
    row_tok = jnp.where(valid, src // TOP_K, 0)
    block_e = row_e.reshape(n_blocks, MOE_BLOCK_ROWS)[:, 0]
    block_e = jnp.where(jnp.arange(n_blocks) < n_used, block_e, block_e[jnp.maximum(n_used - 1, 0)])
    later_used = jnp.logical_and(experts[None, :] > experts[:, None], (counts > 0)[None, :])
    next_used = jnp.min(jnp.where(later_used, experts[None, :], N_EXPERTS), axis=1)
    next_used = jnp.where(next_used < N_EXPERTS, next_used, -1)
    next_e = jnp.sum(jnp.where(block_e[:, None] == experts[None, :], next_used[None, :], 0), axis=1)
    return (block_e.astype(i32), next_e.astype(i32), n_used.astype(i32).reshape(1), row_tok.astype(i32),
            dest.astype(i32))


def kernel(x, p, g_mix, w_in, w_pool, pool_scale, w_branch_a, sgu_ln_g, sgu_ln_b, w_spatial, b_spatial, w_branch_b, w_merge_gate, b_merge_gate, w_out, g_ffn, w_router_group, b_router_group, w_router_expert, b_router_expert, w_exp_gate, w_exp_up, w_exp_down, g_ple, w_ple_gate, b_ple_gate, w_ple_up, g_final):
    bsz, seq, d = x.shape
    assert (seq, d) == (SEQ, D_MODEL) and g_mix.shape[0] == 1
    n_tok = bsz * seq
    x2d = x.reshape(n_tok, d)
    row2d = lambda v: v.reshape(1, -1)

    za, sb = _mixer_front(
        x2d, row2d(g_mix[0]), w_in[0].astype(BF16), w_pool[0].astype(BF16), row2d(pool_scale[0]),
        row2d(sgu_ln_g[0]), row2d(sgu_ln_b[0]), w_spatial[0], b_spatial[0][:, :, None])

    w_router = jnp.concatenate(
        [w_router_group[0], jnp.transpose(w_router_expert[0], (1, 0, 2)).reshape(d, N_EXPERTS)], axis=1)
    b_router = jnp.concatenate([b_router_group[0], b_router_expert[0].reshape(N_EXPERTS)])
    pad = LANES - w_router.shape[1]
    w_router = jnp.pad(w_router, ((0, 0), (0, pad)))
    b_router = jnp.pad(b_router, (0, pad))
    w_router_hi = w_router.astype(BF16)
    w_router_lo = (w_router - w_router_hi.astype(F32)).astype(BF16)
    w_router_split = jnp.concatenate([w_router_hi, w_router_lo], axis=1)

    x1, h2p, eid, wts = _mixer_back(
        x2d, za, sb, row2d(g_mix[0]), w_merge_gate[0].astype(BF16), row2d(b_merge_gate[0]),
        w_branch_a[0].astype(BF16), w_branch_b[0].astype(BF16), w_out[0].astype(BF16), row2d(g_ffn[0]),
        w_router_split, row2d(b_router))

    block_e, next_e, n_used, row_tok, dest = _dispatch_plan(eid[:, :TOP_K])
    ys = _expert_ffn(block_e, next_e, n_used, row_tok, h2p, w_exp_gate[0], w_exp_up[0], w_exp_down[0])
    out = _tail(dest, x1, wts, ys, p[0].reshape(n_tok, PLE_DIM), row2d(g_ple[0]), w_ple_gate[0].astype(BF16),
                row2d(b_ple_gate[0]), w_ple_up[0].astype(BF16), row2d(g_final))
    return out.reshape(bsz, seq, d)
```

```python
import jax
import jax.numpy as jnp
from jax import lax
from jax.experimental import pallas as pl
from jax.experimental.pallas import tpu as pltpu

F32 = jnp.float32
BF16 = jnp.bfloat16

D_MODEL = 2048
SEQ = 4096
EPS = 1e-6
PLE_DIM = 256
POOL_WINDOWS = (2, 4, 8, 16)
POOL_WIDTH = D_MODEL // 2
POOL_GROUP_DIM = POOL_WIDTH // len(POOL_WINDOWS)
POOL_HISTORY = max(POOL_WINDOWS)
SGU_BLOCK = 128
SGU_CHUNK = 64
SGU_GROUPS = 8
SGU_WIDTH = D_MODEL // 2
SGU_GROUP_DIM = SGU_WIDTH // SGU_GROUPS
N_IN = POOL_WIDTH + 2 * SGU_WIDTH
N_EXPERT_GROUPS = 4
EXPERTS_PER_GROUP = 8
N_EXPERTS = N_EXPERT_GROUPS * EXPERTS_PER_GROUP
TOP_K = 2
D_EXPERT = D_MODEL // 4
MOE_BLOCK_ROWS = 256

LANES = 128
MXU_COLS = 256
ROUTER_EXPERT_LANE0 = N_EXPERT_GROUPS
VMEM_LIMIT_BYTES = 56 * 1024 * 1024

TM_FRONT = 512
TM_BACK = 256
TM_TAIL = 256
MERGE_CHUNK = 1024
WEIGHT_DMA_PRIORITY = 1
GATHER_GROUP = 32
GATHER_AHEAD = 2
GATHER_SLOTS = GATHER_AHEAD + 1
PACKED_COLS = D_MODEL // 2
GATHER_UNROLL = 8


def _rms_scale(x):
    return x * lax.rsqrt(jnp.mean(x * x, axis=-1, keepdims=True) + EPS)


def _pack_bf16_halves(v):
    bits = lambda part: lax.bitcast_convert_type(part.astype(BF16).astype(F32), jnp.uint32)
    return (bits(v[:, :PACKED_COLS]) >> 16) | bits(v[:, PACKED_COLS:])


def _unpack_bf16_halves(words):
    return (lax.bitcast_convert_type(words << 16, F32),
            lax.bitcast_convert_type(words & jnp.uint32(0xFFFF0000), F32))


def _resident(shape):
    zeros = (0,) * len(shape)
    return pl.BlockSpec(shape, lambda *_: zeros, pipeline_mode=pl.Buffered(1))


def _mixer_front_kernel(x_ref, gmix_ref, win_ref, wpool_ref, pscale_ref, lng_ref, lnb_ref, ws_ref,
                        bsp_ref, za_ref, sb_ref, hist_ref):
    tm = x_ref.shape[0]
    tiles_per_seq = SEQ // tm
    seq_tile = lax.rem(pl.program_id(0), tiles_per_seq)

    @pl.when(seq_tile == 0)
    def _():
        hist_ref[...] = jnp.zeros_like(hist_ref)

    h = (_rms_scale(x_ref[...]) * gmix_ref[...]).astype(BF16)
    project = lambda lo, width: jnp.dot(h, win_ref[:, lo:lo + width], preferred_element_type=F32)
    v = jax.nn.gelu(project(POOL_WIDTH + SGU_WIDTH, SGU_WIDTH))
    vc = v - jnp.mean(v, axis=-1, keepdims=True)
    var = jnp.mean(vc * vc, axis=-1, keepdims=True)
    vn = (vc * lax.rsqrt(var + EPS) * lng_ref[...] + lnb_ref[...]).astype(BF16)
    u = jax.nn.gelu(project(POOL_WIDTH, SGU_WIDTH))
    a = project(0, POOL_WIDTH)

    ext = jnp.concatenate([hist_ref[...], a], axis=0)
    hist_ref[...] = a[tm - POOL_HISTORY:, :]
    frames = (seq_tile * tm + 1 + lax.broadcasted_iota(jnp.int32, (tm, 1), 0)).astype(F32)
    for gi, w in enumerate(POOL_WINDOWS):
        cols = slice(gi * POOL_GROUP_DIM, (gi + 1) * POOL_GROUP_DIM)
        s = ext[:, cols]
        k = 1
        while k < w:
            s = s + pltpu.roll(s, k, 0)
            k *= 2
        wsum = s[POOL_HISTORY:, :]
        zg = wsum / jnp.minimum(frames, float(w)) - a[:, cols]
        yg = jnp.dot(zg.astype(BF16), wpool_ref[gi], preferred_element_type=F32)
        za_ref[:, cols] = (yg * pscale_ref[:, cols]).astype(BF16)

    t_chunk = lax.broadcasted_iota(jnp.int32, (SGU_BLOCK, SGU_BLOCK), 0) // SGU_CHUNK
    s_chunk = lax.broadcasted_iota(jnp.int32, (SGU_BLOCK, SGU_BLOCK), 1) // SGU_CHUNK
    causal = s_chunk <= t_chunk
    nblk = tm // SGU_BLOCK
    for g in range(SGU_GROUPS):
        cols = slice(g * SGU_GROUP_DIM, (g + 1) * SGU_GROUP_DIM)
        wsg = jnp.where(causal, ws_ref[g], 0.0).astype(BF16)
        vg = jnp.concatenate([vn[j * SGU_BLOCK:(j + 1) * SGU_BLOCK, cols] for j in range(nblk)], axis=1)
        vm = jnp.dot(wsg, vg, preferred_element_type=F32) + bsp_ref[g]
        for j in range(nblk):
            rows = slice(j * SGU_BLOCK, (j + 1) * SGU_BLOCK)
            sb_ref[rows, cols] = (u[rows, cols] * vm[:, j * SGU_GROUP_DIM:(j + 1) * SGU_GROUP_DIM]).astype(BF16)


def _mixer_front(x2d, g_mix, w_in, w_pool, pool_scale, ln_g, ln_b, w_spatial, b_spatial):
    n_tok = x2d.shape[0]
    tm = TM_FRONT
    row = lambda i: (i, 0)
    return pl.pallas_call(
        _mixer_front_kernel,
        grid=(n_tok // tm,),
        in_specs=[
            pl.BlockSpec((tm, D_MODEL), row),
            _resident((1, D_MODEL)),
            _resident((D_MODEL, N_IN)),
            _resident((len(POOL_WINDOWS), POOL_GROUP_DIM, POOL_GROUP_DIM)),
            _resident((1, POOL_WIDTH)),
            _resident((1, SGU_WIDTH)),
            _resident((1, SGU_WIDTH)),
            _resident((SGU_GROUPS, SGU_BLOCK, SGU_BLOCK)),
            _resident((SGU_GROUPS, SGU_BLOCK, 1)),
        ],
        out_specs=[pl.BlockSpec((tm, POOL_WIDTH), row), pl.BlockSpec((tm, SGU_WIDTH), row)],
        out_shape=[jax.ShapeDtypeStruct((n_tok, POOL_WIDTH), BF16),
                   jax.ShapeDtypeStruct((n_tok, SGU_WIDTH), BF16)],
        scratch_shapes=[pltpu.VMEM((POOL_HISTORY, POOL_WIDTH), F32)],
        compiler_params=pltpu.CompilerParams(dimension_semantics=("arbitrary",),
                                             vmem_limit_bytes=VMEM_LIMIT_BYTES),
        name="mixer_front",
    )(x2d, g_mix, w_in, w_pool, pool_scale, ln_g, ln_b, w_spatial, b_spatial)


def _route(logits):
    lane = lax.broadcasted_iota(jnp.int32, logits.shape, 1).astype(F32)
    neg = -jnp.inf
    far = float(LANES)

    def first_argmax(vals):
        top = jnp.max(vals, axis=-1, keepdims=True)
        return top, jnp.min(jnp.where(vals == top, lane, far), axis=-1, keepdims=True)

    is_grp = lane < float(N_EXPERT_GROUPS)
    g_top, g_idx = first_argmax(jnp.where(is_grp, logits, neg))
    g_den = jnp.sum(jnp.where(is_grp, jnp.exp(logits - g_top), 0.0), axis=-1, keepdims=True)
    grp_p = 1.0 / g_den
    lo = float(ROUTER_EXPERT_LANE0) + g_idx * float(EXPERTS_PER_GROUP)
    e_log = jnp.where(lane >= lo, jnp.where(lane < lo + float(EXPERTS_PER_GROUP), logits, neg), neg)
    t1, i1 = first_argmax(e_log)
    t2, i2 = first_argmax(jnp.where(lane == i1, neg, e_log))
    r = jnp.exp(t2 - t1)
    w1 = grp_p / (1.0 + r)
    w2 = grp_p * r / (1.0 + r)
    e1 = i1 - float(ROUTER_EXPERT_LANE0)
    e2 = i2 - float(ROUTER_EXPERT_LANE0)
    eid = jnp.where(lane == 0.0, e1, jnp.where(lane == 1.0, e2, 0.0)).astype(jnp.int32)
    wts = jnp.where(lane == 0.0, w1, jnp.where(lane == 1.0, w2, 0.0))
    return eid, wts


def _mixer_back_kernel(x_ref, za_ref, sb_ref, gmix_ref, wm_ref, bm_ref, wa_ref, wb_ref, wo_ref,
                       gffn_ref, wr_ref, br_ref, x1_ref, h2p_ref, eid_ref, wts_ref, carry_ref):
    i = pl.program_id(0)

    @pl.when(i == 0)
    def _():
        carry_ref[1] = jnp.zeros(carry_ref.shape[1:], F32)

    x1_prev = carry_ref[lax.rem(i + 1, 2)]
    h2 = _rms_scale(x1_prev) * gffn_ref[...]
    h2_hi = h2.astype(BF16)
    h2_lo = (h2 - h2_hi.astype(F32)).astype(BF16)
    h2p_ref[...] = _pack_bf16_halves(h2)
    hi_terms = jnp.dot(h2_hi, wr_ref[...], preferred_element_type=F32)
    lo_term = jnp.dot(h2_lo, wr_ref[:, :LANES], preferred_element_type=F32)
    logits = hi_terms[:, :LANES] + hi_terms[:, LANES:] + lo_term + br_ref[...]
    eid, wts = _route(logits)
    eid_ref[...] = eid
    wts_ref[...] = wts

    x = x_ref[...]
    h = (_rms_scale(x) * gmix_ref[...]).astype(BF16)
    za = za_ref[...]
    sb = sb_ref[...]
    acc = jnp.zeros(x.shape, F32)
    for c in range(D_MODEL // MERGE_CHUNK):
        ca = slice(c * MERGE_CHUNK, (c + 1) * MERGE_CHUNK)
        cb = slice(D_MODEL + c * MERGE_CHUNK, D_MODEL + (c + 1) * MERGE_CHUNK)
        ga = jax.nn.sigmoid(jnp.dot(h, wm_ref[:, ca], preferred_element_type=F32) + bm_ref[:, ca])
        gb = jax.nn.sigmoid(jnp.dot(h, wm_ref[:, cb], preferred_element_type=F32) + bm_ref[:, cb])
        ya = jnp.dot(za, wa_ref[:, ca], preferred_element_type=F32)
        yb = jnp.dot(sb, wb_ref[:, ca], preferred_element_type=F32)
        merged = (ga * ya + gb * yb).astype(BF16)
        acc = acc + jnp.dot(merged, wo_ref[ca, :], preferred_element_type=F32)
    x1 = x + acc
    x1_ref[...] = x1
    carry_ref[lax.rem(i, 2)] = x1


def _mixer_back(x2d, za, sb, g_mix, w_merge, b_merge, w_a, w_b, w_out, g_ffn, w_router, b_router):
    n_tok = x2d.shape[0]
    tm = TM_BACK
    n_tiles = n_tok // tm
    row = lambda i: (jnp.minimum(i, n_tiles - 1), 0)
    prev = lambda i: (jnp.maximum(i - 1, 0), 0)
    return pl.pallas_call(
        _mixer_back_kernel,
        grid=(n_tiles + 1,),
        in_specs=[
            pl.BlockSpec((tm, D_MODEL), row),
            pl.BlockSpec((tm, POOL_WIDTH), row),
            pl.BlockSpec((tm, SGU_WIDTH), row),
            _resident((1, D_MODEL)),
            _resident((D_MODEL, 2 * D_MODEL)),
            _resident((1, 2 * D_MODEL)),
            _resident((POOL_WIDTH, D_MODEL)),
            _resident((SGU_WIDTH, D_MODEL)),
            _resident((D_MODEL, D_MODEL)),
            _resident((1, D_MODEL)),
            _resident((D_MODEL, 2 * LANES)),
            _resident((1, LANES)),
        ],
        out_specs=[pl.BlockSpec((tm, D_MODEL), row), pl.BlockSpec((tm, PACKED_COLS), prev),
                   pl.BlockSpec((tm, LANES), prev), pl.BlockSpec((tm, LANES), prev)],
        out_shape=[jax.ShapeDtypeStruct((n_tok, D_MODEL), F32),
                   jax.ShapeDtypeStruct((n_tok, PACKED_COLS), jnp.uint32),
                   jax.ShapeDtypeStruct((n_tok, LANES), jnp.int32),
                   jax.ShapeDtypeStruct((n_tok, LANES), F32)],
        scratch_shapes=[pltpu.VMEM((2, tm, D_MODEL), F32)],
        compiler_params=pltpu.CompilerParams(dimension_semantics=("arbitrary",),
                                             vmem_limit_bytes=VMEM_LIMIT_BYTES),
        name="mixer_back",
    )(x2d, za, sb, g_mix, w_merge, b_merge, w_a, w_b, w_out, g_ffn, w_router, b_router)


def _row_gather_copy(src_hbm, src_row, dst_vmem, dst_row, sem):
    return pltpu.make_async_copy(src_hbm.at[pl.ds(src_row, 1), :], dst_vmem.at[pl.ds(dst_row, 1), :], sem)


def _expert_kernel(be_ref, nexte_ref, nused_ref, cnt_ref, rowtok_ref, h2p_hbm, wg_hbm, wu_hbm, wd_hbm,
                   ys_ref, xbuf, sems, wg_st, wu_st, wd_st, wsems, wg_bf, wu_bf, wd_bf):
    rows = MOE_BLOCK_ROWS
    b = pl.program_id(0)
    n_blocks = pl.num_programs(0)
    n_used = nused_ref[0]
    slot = lax.rem(b, GATHER_SLOTS)
    groups = rows // GATHER_GROUP

    def start_rows(blk, cnt, slt):
        for g in range(groups):
            @pl.when(g * GATHER_GROUP < cnt)
            def _():
                for r in range(g * GATHER_GROUP, (g + 1) * GATHER_GROUP):
                    _row_gather_copy(h2p_hbm, rowtok_ref[blk * rows + r], xbuf.at[slt], r, sems.at[slt]).start()

    def wait_rows(cnt, slt):
        for g in range(groups):
            @pl.when(g * GATHER_GROUP < cnt)
            def _():
                part = pl.ds(g * GATHER_GROUP, GATHER_GROUP)
                pltpu.make_async_copy(h2p_hbm.at[part, :], xbuf.at[slt, part, :], sems.at[slt]).wait()

    def weight_copies(e):
        return [pltpu.make_async_copy(src.at[e], dst, wsems.at[j])
                for j, (src, dst) in enumerate(((wg_hbm, wg_st), (wu_hbm, wu_st), (wd_hbm, wd_st)))]

    @pl.when(b == 0)
    def _():
        xbuf[...] = jnp.zeros_like(xbuf)
        for ahead in range(GATHER_AHEAD):
            start_rows(ahead, cnt_ref[ahead], ahead)

    @pl.when(b < n_used)
    def _():
        @pl.when(b == 0)
        def _():
            for cp in weight_copies(be_ref[0]):
                cp.start()

        @pl.when(jnp.logical_or(b == 0, be_ref[b] != be_ref[jnp.maximum(b - 1, 0)]))
        def _():
            for cp in weight_copies(be_ref[b]):
                cp.wait()
            wg_bf[...] = wg_st[...].astype(BF16)
            wu_bf[...] = wu_st[...].astype(BF16)
            wd_bf[...] = wd_st[...].astype(BF16)
            nxt_e = nexte_ref[b]

            @pl.when(nxt_e >= 0)
            def _():
                for cp in weight_copies(nxt_e):
                    cp.start(priority=WEIGHT_DMA_PRIORITY)

        ahead_blk = jnp.minimum(b + GATHER_AHEAD, n_blocks - 1)
        ahead_cnt = jnp.where(b + GATHER_AHEAD < n_blocks, cnt_ref[ahead_blk], 0)
        start_rows(ahead_blk, ahead_cnt, lax.rem(b + GATHER_AHEAD, GATHER_SLOTS))
        wait_rows(cnt_ref[b], slot)

        h2 = jnp.concatenate([half.astype(BF16) for half in _unpack_bf16_halves(xbuf[slot])], axis=1)
        gate = jnp.dot(h2, wg_bf[...], preferred_element_type=F32)
        up = jnp.dot(h2, wu_bf[...], preferred_element_type=F32)
        hid = (jax.nn.silu(gate) * up).astype(BF16)
        y = jnp.dot(hid, wd_bf[...], preferred_element_type=F32)
        ys_ref[...] = _pack_bf16_halves(y)

    @pl.when(b >= n_used)
    def _():
        ys_ref[...] = jnp.zeros_like(ys_ref)


def _expert_ffn(block_e, next_e, n_used, block_cnt, row_tok, h2p, w_g, w_u, w_d):
    n_rows = row_tok.shape[0]
    n_blocks = n_rows // MOE_BLOCK_ROWS
    grid_spec = pltpu.PrefetchScalarGridSpec(
        num_scalar_prefetch=5,
        grid=(n_blocks,),
        in_specs=[
            pl.BlockSpec(memory_space=pl.ANY),
            pl.BlockSpec(memory_space=pl.ANY),
            pl.BlockSpec(memory_space=pl.ANY),
            pl.BlockSpec(memory_space=pl.ANY),
        ],
        out_specs=pl.BlockSpec((MOE_BLOCK_ROWS, PACKED_COLS), lambda b, *_: (b, 0)),
        scratch_shapes=[
            pltpu.VMEM((GATHER_SLOTS, MOE_BLOCK_ROWS, PACKED_COLS), jnp.uint32),
            pltpu.SemaphoreType.DMA((GATHER_SLOTS,)),
            pltpu.VMEM((D_MODEL, D_EXPERT), F32),
            pltpu.VMEM((D_MODEL, D_EXPERT), F32),
            pltpu.VMEM((D_EXPERT, D_MODEL), F32),
            pltpu.SemaphoreType.DMA((3,)),
            pltpu.VMEM((D_MODEL, D_EXPERT), BF16),
            pltpu.VMEM((D_MODEL, D_EXPERT), BF16),
            pltpu.VMEM((D_EXPERT, D_MODEL), BF16),
        ],
    )
    return pl.pallas_call(
        _expert_kernel,
        grid_spec=grid_spec,
        out_shape=jax.ShapeDtypeStruct((n_rows, PACKED_COLS), jnp.uint32),
        compiler_params=pltpu.CompilerParams(dimension_semantics=("arbitrary",),
                                             vmem_limit_bytes=VMEM_LIMIT_BYTES),
        name="expert_ffn",
    )(block_e, next_e, n_used, block_cnt, row_tok, h2p, w_g, w_u, w_d)


def _tail_kernel(dest_ref, x1_ref, wts_ref, ys_hbm, p_ref, gple_ref, wpg_ref, bpg_ref, wpu_ref, gfin_ref,
                 out_ref, ybuf, sems):
    tm = x1_ref.shape[0]
    i = pl.program_id(0)
    last = pl.num_programs(0) - 1
    slot = lax.rem(i, 2)

    def start_token(step, slt, r):
        for k in range(TOP_K):
            _row_gather_copy(ys_hbm, dest_ref[TOP_K * (step * tm + r) + k], ybuf.at[slt], k * tm + r,
                             sems.at[slt]).start()

    def wait_tile(slt):
        pltpu.make_async_copy(ys_hbm.at[pl.ds(0, TOP_K * tm), :], ybuf.at[slt], sems.at[slt]).wait()

    @pl.when(i == 0)
    def _():
        def body(r, carry):
            start_token(0, 0, r)
            return carry
        lax.fori_loop(0, tm, body, 0, unroll=GATHER_UNROLL)

    wait_tile(slot)
    nxt = jnp.minimum(i + 1, last)
    other = 1 - slot
    wts = wts_ref[...]
    x2 = x1_ref[...]
    for k in range(TOP_K):
        y = jnp.concatenate(_unpack_bf16_halves(ybuf[slot, k * tm:(k + 1) * tm, :]), axis=1)
        x2 = x2 + wts[:, k:k + 1] * y
    hn = (_rms_scale(x2) * gple_ref[...]).astype(BF16)
    up = jnp.dot(p_ref[...].astype(BF16), wpu_ref[...], preferred_element_type=F32)
    n_slabs = D_MODEL // MXU_COLS
    per = tm // n_slabs
    x3 = []
    for n in range(n_slabs):
        cols = slice(n * MXU_COLS, (n + 1) * MXU_COLS)
        for r in range(n * per, (n + 1) * per):
            start_token(nxt, other, r)
        gate = jax.nn.sigmoid(jnp.dot(hn, wpg_ref[:, cols], preferred_element_type=F32) + bpg_ref[:, cols])
        x3.append(x2[:, cols] + gate * up[:, cols])
    x3 = jnp.concatenate(x3, axis=1)
    out_ref[...] = _rms_scale(x3) * gfin_ref[...]

    @pl.when(i == last)
    def _():
        wait_tile(other)


def _tail(dest, x1, wts, ys, p2d, g_ple, w_pg, b_pg, w_pu, g_final):
    n_tok = x1.shape[0]
    tm = TM_TAIL
    row = lambda i, *_: (i, 0)
    const = lambda i, *_: (0, 0)
    grid_spec = pltpu.PrefetchScalarGridSpec(
        num_scalar_prefetch=1,
        grid=(n_tok // tm,),
        in_specs=[
            pl.BlockSpec((tm, D_MODEL), row),
            pl.BlockSpec((tm, LANES), row),
            pl.BlockSpec(memory_space=pl.ANY),
            pl.BlockSpec((tm, PLE_DIM), row),
            pl.BlockSpec((1, D_MODEL), const),
            pl.BlockSpec((D_MODEL, D_MODEL), const, pipeline_mode=pl.Buffered(1)),
            pl.BlockSpec((1, D_MODEL), const),
            pl.BlockSpec((PLE_DIM, D_MODEL), const),
            pl.BlockSpec((1, D_MODEL), const),
        ],
        out_specs=pl.BlockSpec((tm, D_MODEL), row),
        scratch_shapes=[pltpu.VMEM((2, TOP_K * tm, PACKED_COLS), jnp.uint32), pltpu.SemaphoreType.DMA((2,))],
    )
    return pl.pallas_call(
        _tail_kernel,
        grid_spec=grid_spec,
        out_shape=jax.ShapeDtypeStruct((n_tok, D_MODEL), F32),
        compiler_params=pltpu.CompilerParams(dimension_semantics=("arbitrary",),
                                             vmem_limit_bytes=VMEM_LIMIT_BYTES),
        name="tail",
    )(dest, x1, wts, ys, p2d, g_ple, w_pg, b_pg, w_pu, g_final)


def _dispatch_plan(expert_id):
    n_tok = expert_id.shape[0]
    n_assign = n_tok * TOP_K
    n_blocks = -(-n_assign // MOE_BLOCK_ROWS) + N_EXPERTS
    n_rows = n_blocks * MOE_BLOCK_ROWS
    i32 = jnp.int32
    flat_e = expert_id.reshape(-1)
    experts = jnp.arange(N_EXPERTS, dtype=i32)
    assign = jnp.arange(n_assign, dtype=i32)
    se, order = lax.sort((flat_e, assign), num_keys=1)
    onehot_sorted = se[:, None] == experts[None, :]
    counts = jnp.sum(onehot_sorted.astype(i32), axis=0)
    padded = (counts + MOE_BLOCK_ROWS - 1) // MOE_BLOCK_ROWS * MOE_BLOCK_ROWS
    pad_end = jnp.cumsum(padded)
    pad_start = pad_end - padded
    start = jnp.cumsum(counts) - counts
    row_of_sorted = assign + jnp.sum(jnp.where(onehot_sorted, (pad_start - start)[None, :], 0), axis=1)
    _, dest = lax.sort((order, row_of_sorted), num_keys=1)
    n_used = pad_end[-1] // MOE_BLOCK_ROWS
    rows = jnp.arange(n_rows, dtype=i32)
    row_e = jnp.minimum(jnp.sum((pad_end[None, :] <= rows[:, None]).astype(i32), axis=1), N_EXPERTS - 1)
    onehot_row = row_e[:, None] == experts[None, :]
    pick = lambda table: jnp.sum(jnp.where(onehot_row, table[None, :], 0), axis=1)
    offset = rows - pick(pad_start)
    valid = offset < pick(counts)
    src = order[jnp.clip(pick(start) + offset, 0, n_assign - 1)]
    row_tok = jnp.where(valid, src // TOP_K, 0)
    block_e = row_e.reshape(n_blocks, MOE_BLOCK_ROWS)[:, 0]
    block_cnt = jnp.sum(valid.reshape(n_blocks, MOE_BLOCK_ROWS).astype(i32), axis=1)
    block_e = jnp.where(jnp.arange(n_blocks) < n_used, block_e, block_e[jnp.maximum(n_used - 1, 0)])
    later_used = jnp.logical_and(experts[None, :] > experts[:, None], (counts > 0)[None, :])
    next_used = jnp.min(jnp.where(later_used, experts[None, :], N_EXPERTS), axis=1)
    next_used = jnp.where(next_used < N_EXPERTS, next_used, -1)
    next_e = jnp.sum(jnp.where(block_e[:, None] == experts[None, :], next_used[None, :], 0), axis=1)
    return (block_e.astype(i32), next_e.astype(i32), n_used.astype(i32).reshape(1), block_cnt,
            row_tok.astype(i32), dest.astype(i32))


def kernel(x, p, g_mix, w_in, w_pool, pool_scale, w_branch_a, sgu_ln_g, sgu_ln_b, w_spatial, b_spatial, w_branch_b, w_merge_gate, b_merge_gate, w_out, g_ffn, w_router_group, b_router_group, w_router_expert, b_router_expert, w_exp_gate, w_exp_up, w_exp_down, g_ple, w_ple_gate, b_ple_gate, w_ple_up, g_final):
    bsz, seq, d = x.shape
    assert (seq, d) == (SEQ, D_MODEL) and g_mix.shape[0] == 1
    n_tok = bsz * seq
    x2d = x.reshape(n_tok, d)
    row2d = lambda v: v.reshape(1, -1)

    za, sb = _mixer_front(
        x2d, row2d(g_mix[0]), w_in[0].astype(BF16), w_pool[0].astype(BF16), row2d(pool_scale[0]),
        row2d(sgu_ln_g[0]), row2d(sgu_ln_b[0]), w_spatial[0], b_spatial[0][:, :, None])

    w_router = jnp.concatenate(
        [w_router_group[0], jnp.transpose(w_router_expert[0], (1, 0, 2)).reshape(d, N_EXPERTS)], axis=1)
    b_router = jnp.concatenate([b_router_group[0], b_router_expert[0].reshape(N_EXPERTS)])
    pad = LANES - w_router.shape[1]
    w_router = jnp.pad(w_router, ((0, 0), (0, pad)))
    b_router = jnp.pad(b_router, (0, pad))
    w_router_hi = w_router.astype(BF16)
    w_router_lo = (w_router - w_router_hi.astype(F32)).astype(BF16)
    w_router_split = jnp.concatenate([w_router_hi, w_router_lo], axis=1)

    x1, h2p, eid, wts = _mixer_back(
        x2d, za, sb, row2d(g_mix[0]), w_merge_gate[0].astype(BF16), row2d(b_merge_gate[0]),
        w_branch_a[0].astype(BF16), w_branch_b[0].astype(BF16), w_out[0].astype(BF16), row2d(g_ffn[0]),
        w_router_split, row2d(b_router))

    block_e, next_e, n_used, block_cnt, row_tok, dest = _dispatch_plan(eid[:, :TOP_K])
    ys = _expert_ffn(block_e, next_e, n_used, block_cnt, row_tok, h2p, w_exp_gate[0], w_exp_up[0],
                     w_exp_down[0])
    out = _tail(dest, x1, wts, ys, p[0].reshape(n_tok, PLE_DIM), row2d(g_ple[0]), w_ple_gate[0].astype(BF16),
                row2d(b_ple_gate[0]), w_ple_up[0].astype(BF16), row2d(g_final))
    return out.reshape(bsz, seq, d)
```

```python
import jax
import jax.numpy as jnp
from jax import lax
from jax.experimental import pallas as pl
from jax.experimental.pallas import tpu as pltpu

F32 = jnp.float32
BF16 = jnp.bfloat16

D_MODEL = 2048
SEQ = 4096
EPS = 1e-6
PLE_DIM = 256
POOL_WINDOWS = (2, 4, 8, 16)
POOL_WIDTH = D_MODEL // 2
POOL_GROUP_DIM = POOL_WIDTH // len(POOL_WINDOWS)
POOL_HISTORY = max(POOL_WINDOWS)
SGU_BLOCK = 128
SGU_CHUNK = 64
SGU_GROUPS = 8
SGU_WIDTH = D_MODEL // 2
SGU_GROUP_DIM = SGU_WIDTH // SGU_GROUPS
N_IN = POOL_WIDTH + 2 * SGU_WIDTH
N_EXPERT_GROUPS = 4
EXPERTS_PER_GROUP = 8
N_EXPERTS = N_EXPERT_GROUPS * EXPERTS_PER_GROUP
TOP_K = 2
D_EXPERT = D_MODEL // 4
MOE_BLOCK_ROWS = 256

LANES = 128
BF16_SUBLANES = 16
MXU_COLS = 256
ROUTER_EXPERT_LANE0 = N_EXPERT_GROUPS
VMEM_LIMIT_BYTES = 56 * 1024 * 1024

TM_FRONT = 512
TM_BACK = 256
TM_TAIL = 256
MERGE_CHUNK = 1024
WEIGHT_DMA_PRIORITY = 1
GATHER_GROUP = 32
GATHER_AHEAD = 2
GATHER_SLOTS = GATHER_AHEAD + 1
PACKED_COLS = D_MODEL // 2
GATHER_UNROLL = 8


def _rms_scale(x):
    return x * lax.rsqrt(jnp.mean(x * x, axis=-1, keepdims=True) + EPS)


def _pack_bf16_halves(v):
    bits = lambda part: lax.bitcast_convert_type(part.astype(BF16).astype(F32), jnp.uint32)
    return (bits(v[:, :PACKED_COLS]) >> 16) | bits(v[:, PACKED_COLS:])


def _unpack_bf16_halves(words):
    return (lax.bitcast_convert_type(words << 16, F32),
            lax.bitcast_convert_type(words & jnp.uint32(0xFFFF0000), F32))


def _resident(shape):
    zeros = (0,) * len(shape)
    return pl.BlockSpec(shape, lambda *_: zeros, pipeline_mode=pl.Buffered(1))


def _mixer_front_kernel(x_ref, gmix_ref, win_ref, wpool_ref, pscale_ref, lng_ref, lnb_ref, ws_ref,
                        bsp_ref, *rest):
    n_later = len(rest) // 2 - 1
    later_f32, (za_ref, sb_ref), later_bf16, hist_ref = (
        rest[:n_later], rest[n_later:n_later + 2], rest[n_later + 2:-1], rest[-1])
    tm = x_ref.shape[0]
    tiles_per_seq = SEQ // tm
    seq_tile = lax.rem(pl.program_id(0), tiles_per_seq)

    @pl.when(seq_tile == 0)
    def _():
        hist_ref[...] = jnp.zeros_like(hist_ref)

    for src, dst in zip(later_f32, later_bf16):
        dst[...] = src[...].astype(BF16)

    h = (_rms_scale(x_ref[...]) * gmix_ref[...]).astype(BF16)
    project = lambda lo, width: jnp.dot(h, win_ref[:, lo:lo + width], preferred_element_type=F32)
    v = jax.nn.gelu(project(POOL_WIDTH + SGU_WIDTH, SGU_WIDTH))
    vc = v - jnp.mean(v, axis=-1, keepdims=True)
    var = jnp.mean(vc * vc, axis=-1, keepdims=True)
    vn = (vc * lax.rsqrt(var + EPS) * lng_ref[...] + lnb_ref[...]).astype(BF16)
    u = jax.nn.gelu(project(POOL_WIDTH, SGU_WIDTH))
    a = project(0, POOL_WIDTH)

    ext = jnp.concatenate([hist_ref[...], a], axis=0)
    hist_ref[...] = a[tm - POOL_HISTORY:, :]
    frames = (seq_tile * tm + 1 + lax.broadcasted_iota(jnp.int32, (tm, 1), 0)).astype(F32)
    for gi, w in enumerate(POOL_WINDOWS):
        cols = slice(gi * POOL_GROUP_DIM, (gi + 1) * POOL_GROUP_DIM)
        s = ext[:, cols]
        k = 1
        while k < w:
            s = s + pltpu.roll(s, k, 0)
            k *= 2
        wsum = s[POOL_HISTORY:, :]
        zg = wsum / jnp.minimum(frames, float(w)) - a[:, cols]
        yg = jnp.dot(zg.astype(BF16), wpool_ref[gi], preferred_element_type=F32)
        za_ref[:, cols] = (yg * pscale_ref[:, cols]).astype(BF16)

    t_chunk = lax.broadcasted_iota(jnp.int32, (SGU_BLOCK, SGU_BLOCK), 0) // SGU_CHUNK
    s_chunk = lax.broadcasted_iota(jnp.int32, (SGU_BLOCK, SGU_BLOCK), 1) // SGU_CHUNK
    causal = s_chunk <= t_chunk
    nblk = tm // SGU_BLOCK
    for g in range(SGU_GROUPS):
        cols = slice(g * SGU_GROUP_DIM, (g + 1) * SGU_GROUP_DIM)
        wsg = jnp.where(causal, ws_ref[g], 0.0).astype(BF16)
        vg = jnp.concatenate([vn[j * SGU_BLOCK:(j + 1) * SGU_BLOCK, cols] for j in range(nblk)], axis=1)
        vm = jnp.dot(wsg, vg, preferred_element_type=F32) + bsp_ref[g]
        for j in range(nblk):
            rows = slice(j * SGU_BLOCK, (j + 1) * SGU_BLOCK)
            sb_ref[rows, cols] = (u[rows, cols] * vm[:, j * SGU_GROUP_DIM:(j + 1) * SGU_GROUP_DIM]).astype(BF16)


def _mixer_front(x2d, g_mix, w_in, w_pool, pool_scale, ln_g, ln_b, w_spatial, b_spatial, later_weights):
    n_tok = x2d.shape[0]
    tm = TM_FRONT
    n_steps = n_tok // tm
    row = lambda i: (i, 0)
    later_specs = [pl.BlockSpec((w.shape[0] // n_steps, w.shape[1]), row) for w in later_weights]
    assert all(w.shape[0] % (n_steps * BF16_SUBLANES) == 0 for w in later_weights)
    return pl.pallas_call(
        _mixer_front_kernel,
        grid=(n_steps,),
        in_specs=[
            pl.BlockSpec((tm, D_MODEL), row),
            _resident((1, D_MODEL)),
            _resident((D_MODEL, N_IN)),
            _resident((len(POOL_WINDOWS), POOL_GROUP_DIM, POOL_GROUP_DIM)),
            _resident((1, POOL_WIDTH)),
            _resident((1, SGU_WIDTH)),
            _resident((1, SGU_WIDTH)),
            _resident((SGU_GROUPS, SGU_BLOCK, SGU_BLOCK)),
            _resident((SGU_GROUPS, SGU_BLOCK, 1)),
        ] + later_specs,
        out_specs=[pl.BlockSpec((tm, POOL_WIDTH), row), pl.BlockSpec((tm, SGU_WIDTH), row)] + later_specs,
        out_shape=[jax.ShapeDtypeStruct((n_tok, POOL_WIDTH), BF16),
                   jax.ShapeDtypeStruct((n_tok, SGU_WIDTH), BF16)]
                  + [jax.ShapeDtypeStruct(w.shape, BF16) for w in later_weights],
        scratch_shapes=[pltpu.VMEM((POOL_HISTORY, POOL_WIDTH), F32)],
        compiler_params=pltpu.CompilerParams(dimension_semantics=("arbitrary",),
                                             vmem_limit_bytes=VMEM_LIMIT_BYTES),
        name="mixer_front",
    )(x2d, g_mix, w_in, w_pool, pool_scale, ln_g, ln_b, w_spatial, b_spatial, *later_weights)


def _route(logits):
    lane = lax.broadcasted_iota(jnp.int32, logits.shape, 1).astype(F32)
    neg = -jnp.inf
    far = float(LANES)

    def first_argmax(vals):
        top = jnp.max(vals, axis=-1, keepdims=True)
        return top, jnp.min(jnp.where(vals == top, lane, far), axis=-1, keepdims=True)

    is_grp = lane < float(N_EXPERT_GROUPS)
    g_top, g_idx = first_argmax(jnp.where(is_grp, logits, neg))
    g_den = jnp.sum(jnp.where(is_grp, jnp.exp(logits - g_top), 0.0), axis=-1, keepdims=True)
    grp_p = 1.0 / g_den
    lo = float(ROUTER_EXPERT_LANE0) + g_idx * float(EXPERTS_PER_GROUP)
    e_log = jnp.where(lane >= lo, jnp.where(lane < lo + float(EXPERTS_PER_GROUP), logits, neg), neg)
    t1, i1 = first_argmax(e_log)
    t2, i2 = first_argmax(jnp.where(lane == i1, neg, e_log))
    r = jnp.exp(t2 - t1)
    w1 = grp_p / (1.0 + r)
    w2 = grp_p * r / (1.0 + r)
    e1 = i1 - float(ROUTER_EXPERT_LANE0)
    e2 = i2 - float(ROUTER_EXPERT_LANE0)
    eid = jnp.where(lane == 0.0, e1, jnp.where(lane == 1.0, e2, 0.0)).astype(jnp.int32)
    wts = jnp.where(lane == 0.0, w1, jnp.where(lane == 1.0, w2, 0.0))
    return eid, wts


def _mixer_back_kernel(x_ref, za_ref, sb_ref, gmix_ref, wm_ref, bm_ref, wa_ref, wb_ref, wo_ref,
                       gffn_ref, wr_ref, br_ref, x1_ref, h2p_ref, eid_ref, wts_ref, carry_ref):
    i = pl.program_id(0)

    @pl.when(i == 0)
    def _():
        carry_ref[1] = jnp.zeros(carry_ref.shape[1:], F32)

    x1_prev = carry_ref[lax.rem(i + 1, 2)]
    h2 = _rms_scale(x1_prev) * gffn_ref[...]
    h2_hi = h2.astype(BF16)
    h2_lo = (h2 - h2_hi.astype(F32)).astype(BF16)
    h2p_ref[...] = _pack_bf16_halves(h2)
    hi_terms = jnp.dot(h2_hi, wr_ref[...], preferred_element_type=F32)
    lo_term = jnp.dot(h2_lo, wr_ref[:, :LANES], preferred_element_type=F32)
    logits = hi_terms[:, :LANES] + hi_terms[:, LANES:] + lo_term + br_ref[...]
    eid, wts = _route(logits)
    eid_ref[...] = eid
    wts_ref[...] = wts

    x = x_ref[...]
    h = (_rms_scale(x) * gmix_ref[...]).astype(BF16)
    za = za_ref[...]
    sb = sb_ref[...]
    acc = jnp.zeros(x.shape, F32)
    for c in range(D_MODEL // MERGE_CHUNK):
        ca = slice(c * MERGE_CHUNK, (c + 1) * MERGE_CHUNK)
        cb = slice(D_MODEL + c * MERGE_CHUNK, D_MODEL + (c + 1) * MERGE_CHUNK)
        ga = jax.nn.sigmoid(jnp.dot(h, wm_ref[:, ca], preferred_element_type=F32) + bm_ref[:, ca])
        gb = jax.nn.sigmoid(jnp.dot(h, wm_ref[:, cb], preferred_element_type=F32) + bm_ref[:, cb])
        ya = jnp.dot(za, wa_ref[:, ca], preferred_element_type=F32)
        yb = jnp.dot(sb, wb_ref[:, ca], preferred_element_type=F32)
        merged = (ga * ya + gb * yb).astype(BF16)
        acc = acc + jnp.dot(merged, wo_ref[ca, :], preferred_element_type=F32)
    x1 = x + acc
    x1_ref[...] = x1
    carry_ref[lax.rem(i, 2)] = x1


def _mixer_back(x2d, za, sb, g_mix, w_merge, b_merge, w_a, w_b, w_out, g_ffn, w_router, b_router):
    n_tok = x2d.shape[0]
    tm = TM_BACK
    n_tiles = n_tok // tm
    row = lambda i: (jnp.minimum(i, n_tiles - 1), 0)
    prev = lambda i: (jnp.maximum(i - 1, 0), 0)
    return pl.pallas_call(
        _mixer_back_kernel,
        grid=(n_tiles + 1,),
        in_specs=[
            pl.BlockSpec((tm, D_MODEL), row),
            pl.BlockSpec((tm, POOL_WIDTH), row),
            pl.BlockSpec((tm, SGU_WIDTH), row),
            _resident((1, D_MODEL)),
            _resident((D_MODEL, 2 * D_MODEL)),
            _resident((1, 2 * D_MODEL)),
            _resident((POOL_WIDTH, D_MODEL)),
            _resident((SGU_WIDTH, D_MODEL)),
            _resident((D_MODEL, D_MODEL)),
            _resident((1, D_MODEL)),
            _resident((D_MODEL, 2 * LANES)),
            _resident((1, LANES)),
        ],
        out_specs=[pl.BlockSpec((tm, D_MODEL), row), pl.BlockSpec((tm, PACKED_COLS), prev),
                   pl.BlockSpec((tm, LANES), prev), pl.BlockSpec((tm, LANES), prev)],
        out_shape=[jax.ShapeDtypeStruct((n_tok, D_MODEL), F32),
                   jax.ShapeDtypeStruct((n_tok, PACKED_COLS), jnp.uint32),
                   jax.ShapeDtypeStruct((n_tok, LANES), jnp.int32),
                   jax.ShapeDtypeStruct((n_tok, LANES), F32)],
        scratch_shapes=[pltpu.VMEM((2, tm, D_MODEL), F32)],
        compiler_params=pltpu.CompilerParams(dimension_semantics=("arbitrary",),
                                             vmem_limit_bytes=VMEM_LIMIT_BYTES),
        name="mixer_back",
    )(x2d, za, sb, g_mix, w_merge, b_merge, w_a, w_b, w_out, g_ffn, w_router, b_router)


def _row_gather_copy(src_hbm, src_row, dst_vmem, dst_row, sem):
    return pltpu.make_async_copy(src_hbm.at[pl.ds(src_row, 1), :], dst_vmem.at[pl.ds(dst_row, 1), :], sem)


def _expert_kernel(be_ref, nexte_ref, nused_ref, cnt_ref, base_ref, tok_ref, h2p_hbm, wg_hbm, wu_hbm, wd_hbm,
                   ys_ref, xbuf, sems, wg_st, wu_st, wd_st, wsems, wg_bf, wu_bf, wd_bf):
    rows = MOE_BLOCK_ROWS
    b = pl.program_id(0)
    n_blocks = pl.num_programs(0)
    n_used = nused_ref[0]
    slot = lax.rem(b, GATHER_SLOTS)
    groups = rows // GATHER_GROUP

    def start_rows(blk, cnt, slt):
        base = base_ref[blk]
        for g in range(groups):
            @pl.when(g * GATHER_GROUP < cnt)
            def _():
                for r in range(g * GATHER_GROUP, (g + 1) * GATHER_GROUP):
                    _row_gather_copy(h2p_hbm, tok_ref[base + r], xbuf.at[slt], r, sems.at[slt]).start()

    def wait_rows(cnt, slt):
        for g in range(groups):
            @pl.when(g * GATHER_GROUP < cnt)
            def _():
                part = pl.ds(g * GATHER_GROUP, GATHER_GROUP)
                pltpu.make_async_copy(h2p_hbm.at[part, :], xbuf.at[slt, part, :], sems.at[slt]).wait()

    def weight_copies(e):
        return [pltpu.make_async_copy(src.at[e], dst, wsems.at[j])
                for j, (src, dst) in enumerate(((wg_hbm, wg_st), (wu_hbm, wu_st), (wd_hbm, wd_st)))]

    @pl.when(b == 0)
    def _():
        xbuf[...] = jnp.zeros_like(xbuf)
        for ahead in range(GATHER_AHEAD):
            start_rows(ahead, cnt_ref[ahead], ahead)

    @pl.when(b < n_used)
    def _():
        @pl.when(b == 0)
        def _():
            for cp in weight_copies(be_ref[0]):
                cp.start()

        @pl.when(jnp.logical_or(b == 0, be_ref[b] != be_ref[jnp.maximum(b - 1, 0)]))
        def _():
            for cp in weight_copies(be_ref[b]):
                cp.wait()
            wg_bf[...] = wg_st[...].astype(BF16)
            wu_bf[...] = wu_st[...].astype(BF16)
            wd_bf[...] = wd_st[...].astype(BF16)
            nxt_e = nexte_ref[b]

            @pl.when(nxt_e >= 0)
            def _():
                for cp in weight_copies(nxt_e):
                    cp.start(priority=WEIGHT_DMA_PRIORITY)

        ahead_blk = jnp.minimum(b + GATHER_AHEAD, n_blocks - 1)
        ahead_cnt = jnp.where(b + GATHER_AHEAD < n_blocks, cnt_ref[ahead_blk], 0)
        start_rows(ahead_blk, ahead_cnt, lax.rem(b + GATHER_AHEAD, GATHER_SLOTS))
        wait_rows(cnt_ref[b], slot)

        h2 = jnp.concatenate([half.astype(BF16) for half in _unpack_bf16_halves(xbuf[slot])], axis=1)
        gate = jnp.dot(h2, wg_bf[...], preferred_element_type=F32)
        up = jnp.dot(h2, wu_bf[...], preferred_element_type=F32)
        hid = (jax.nn.silu(gate) * up).astype(BF16)
        y = jnp.dot(hid, wd_bf[...], preferred_element_type=F32)
        ys_ref[...] = _pack_bf16_halves(y)

    @pl.when(b >= n_used)
    def _():
        ys_ref[...] = jnp.zeros_like(ys_ref)


def _expert_ffn(block_e, next_e, n_used, block_cnt, block_base, sorted_tok, h2p, w_g, w_u, w_d):
    n_blocks = block_e.shape[0]
    n_rows = n_blocks * MOE_BLOCK_ROWS
    grid_spec = pltpu.PrefetchScalarGridSpec(
        num_scalar_prefetch=6,
        grid=(n_blocks,),
        in_specs=[
            pl.BlockSpec(memory_space=pl.ANY),
            pl.BlockSpec(memory_space=pl.ANY),
            pl.BlockSpec(memory_space=pl.ANY),
            pl.BlockSpec(memory_space=pl.ANY),
        ],
        out_specs=pl.BlockSpec((MOE_BLOCK_ROWS, PACKED_COLS), lambda b, *_: (b, 0)),
        scratch_shapes=[
            pltpu.VMEM((GATHER_SLOTS, MOE_BLOCK_ROWS, PACKED_COLS), jnp.uint32),
            pltpu.SemaphoreType.DMA((GATHER_SLOTS,)),
            pltpu.VMEM((D_MODEL, D_EXPERT), F32),
            pltpu.VMEM((D_MODEL, D_EXPERT), F32),
            pltpu.VMEM((D_EXPERT, D_MODEL), F32),
            pltpu.SemaphoreType.DMA((3,)),
            pltpu.VMEM((D_MODEL, D_EXPERT), BF16),
            pltpu.VMEM((D_MODEL, D_EXPERT), BF16),
            pltpu.VMEM((D_EXPERT, D_MODEL), BF16),
        ],
    )
    return pl.pallas_call(
        _expert_kernel,
        grid_spec=grid_spec,
        out_shape=jax.ShapeDtypeStruct((n_rows, PACKED_COLS), jnp.uint32),
        compiler_params=pltpu.CompilerParams(dimension_semantics=("arbitrary",),
                                             vmem_limit_bytes=VMEM_LIMIT_BYTES),
        name="expert_ffn",
    )(block_e, next_e, n_used, block_cnt, block_base, sorted_tok, h2p, w_g, w_u, w_d)


def _tail_kernel(dest_ref, x1_ref, wts_ref, ys_hbm, p_ref, gple_ref, wpg_ref, bpg_ref, wpu_ref, gfin_ref,
                 out_ref, ybuf, sems):
    tm = x1_ref.shape[0]
    i = pl.program_id(0)
    last = pl.num_programs(0) - 1
    slot = lax.rem(i, 2)

    def start_token(step, slt, r):
        for k in range(TOP_K):
            _row_gather_copy(ys_hbm, dest_ref[TOP_K * (step * tm + r) + k], ybuf.at[slt], k * tm + r,
                             sems.at[slt]).start()

    def wait_tile(slt):
        pltpu.make_async_copy(ys_hbm.at[pl.ds(0, TOP_K * tm), :], ybuf.at[slt], sems.at[slt]).wait()

    @pl.when(i == 0)
    def _():
        def body(r, carry):
            start_token(0, 0, r)
            return carry
        lax.fori_loop(0, tm, body, 0, unroll=GATHER_UNROLL)

    wait_tile(slot)
    nxt = jnp.minimum(i + 1, last)
    other = 1 - slot
    wts = wts_ref[...]
    x2 = x1_ref[...]
    for k in range(TOP_K):
        y = jnp.concatenate(_unpack_bf16_halves(ybuf[slot, k * tm:(k + 1) * tm, :]), axis=1)
        x2 = x2 + wts[:, k:k + 1] * y
    hn = (_rms_scale(x2) * gple_ref[...]).astype(BF16)
    up = jnp.dot(p_ref[...].astype(BF16), wpu_ref[...], preferred_element_type=F32)
    n_slabs = D_MODEL // MXU_COLS
    per = tm // n_slabs
    x3 = []
    for n in range(n_slabs):
        cols = slice(n * MXU_COLS, (n + 1) * MXU_COLS)
        for r in range(n * per, (n + 1) * per):
            start_token(nxt, other, r)
        gate = jax.nn.sigmoid(jnp.dot(hn, wpg_ref[:, cols], preferred_element_type=F32) + bpg_ref[:, cols])
        x3.append(x2[:, cols] + gate * up[:, cols])
    x3 = jnp.concatenate(x3, axis=1)
    out_ref[...] = _rms_scale(x3) * gfin_ref[...]

    @pl.when(i == last)
    def _():
        wait_tile(other)


def _tail(dest, x1, wts, ys, p2d, g_ple, w_pg, b_pg, w_pu, g_final):
    n_tok = x1.shape[0]
    tm = TM_TAIL
    row = lambda i, *_: (i, 0)
    const = lambda i, *_: (0, 0)
    grid_spec = pltpu.PrefetchScalarGridSpec(
        num_scalar_prefetch=1,
        grid=(n_tok // tm,),
        in_specs=[
            pl.BlockSpec((tm, D_MODEL), row),
            pl.BlockSpec((tm, LANES), row),
            pl.BlockSpec(memory_space=pl.ANY),
            pl.BlockSpec((tm, PLE_DIM), row),
            pl.BlockSpec((1, D_MODEL), const),
            pl.BlockSpec((D_MODEL, D_MODEL), const, pipeline_mode=pl.Buffered(1)),
            pl.BlockSpec((1, D_MODEL), const),
            pl.BlockSpec((PLE_DIM, D_MODEL), const),
            pl.BlockSpec((1, D_MODEL), const),
        ],
        out_specs=pl.BlockSpec((tm, D_MODEL), row),
        scratch_shapes=[pltpu.VMEM((2, TOP_K * tm, PACKED_COLS), jnp.uint32), pltpu.SemaphoreType.DMA((2,))],
    )
    return pl.pallas_call(
        _tail_kernel,
        grid_spec=grid_spec,
        out_shape=jax.ShapeDtypeStruct((n_tok, D_MODEL), F32),
        compiler_params=pltpu.CompilerParams(dimension_semantics=("arbitrary",),
                                             vmem_limit_bytes=VMEM_LIMIT_BYTES),
        name="tail",
    )(dest, x1, wts, ys, p2d, g_ple, w_pg, b_pg, w_pu, g_final)


def _dispatch_plan(expert_id):
    n_tok = expert_id.shape[0]
    n_assign = n_tok * TOP_K
    n_blocks = -(-n_assign // MOE_BLOCK_ROWS) + N_EXPERTS
    n_rows = n_blocks * MOE_BLOCK_ROWS
    i32 = jnp.int32
    flat_e = expert_id.reshape(-1)
    experts = jnp.arange(N_EXPERTS, dtype=i32)
    assign = jnp.arange(n_assign, dtype=i32)
    se, order = lax.sort((flat_e, assign), num_keys=1)
    onehot_sorted = se[:, None] == experts[None, :]
    counts = jnp.sum(onehot_sorted.astype(i32), axis=0)
    padded = (counts + MOE_BLOCK_ROWS - 1) // MOE_BLOCK_ROWS * MOE_BLOCK_ROWS
    pad_end = jnp.cumsum(padded)
    pad_start = pad_end - padded
    start = jnp.cumsum(counts) - counts
    row_of_sorted = assign + jnp.sum(jnp.where(onehot_sorted, (pad_start - start)[None, :], 0), axis=1)
    _, dest = lax.sort((order, row_of_sorted), num_keys=1)
    n_used = pad_end[-1] // MOE_BLOCK_ROWS
    rows = jnp.arange(n_rows, dtype=i32)
    row_e = jnp.minimum(jnp.sum((pad_end[None, :] <= rows[:, None]).astype(i32), axis=1), N_EXPERTS - 1)
    onehot_row = row_e[:, None] == experts[None, :]
    pick = lambda table: jnp.sum(jnp.where(onehot_row, table[None, :], 0), axis=1)
    offset = rows - pick(pad_start)
    valid = offset < pick(counts)
    sorted_tok = jnp.concatenate([order // TOP_K, jnp.zeros((MOE_BLOCK_ROWS,), i32)])
    block_base = jnp.clip((pick(start) + offset).reshape(n_blocks, MOE_BLOCK_ROWS)[:, 0], 0, n_assign)
    block_e = row_e.reshape(n_blocks, MOE_BLOCK_ROWS)[:, 0]
    block_cnt = jnp.sum(valid.reshape(n_blocks, MOE_BLOCK_ROWS).astype(i32), axis=1)
    block_e = jnp.where(jnp.arange(n_blocks) < n_used, block_e, block_e[jnp.maximum(n_used - 1, 0)])
    later_used = jnp.logical_and(experts[None, :] > experts[:, None], (counts > 0)[None, :])
    next_used = jnp.min(jnp.where(later_used, experts[None, :], N_EXPERTS), axis=1)
    next_used = jnp.where(next_used < N_EXPERTS, next_used, -1)
    next_e = jnp.sum(jnp.where(block_e[:, None] == experts[None, :], next_used[None, :], 0), axis=1)
    return (block_e.astype(i32), next_e.astype(i32), n_used.astype(i32).reshape(1), block_cnt,
            block_base.astype(i32), sorted_tok.astype(i32), dest.astype(i32))


def kernel(x, p, g_mix, w_in, w_pool, pool_scale, w_branch_a, sgu_ln_g, sgu_ln_b, w_spatial, b_spatial, w_branch_b, w_merge_gate, b_merge_gate, w_out, g_ffn, w_router_group, b_router_group, w_router_expert, b_router_expert, w_exp_gate, w_exp_up, w_exp_down, g_ple, w_ple_gate, b_ple_gate, w_ple_up, g_final):
    bsz, seq, d = x.shape
    assert (seq, d) == (SEQ, D_MODEL) and g_mix.shape[0] == 1
    n_tok = bsz * seq
    x2d = x.reshape(n_tok, d)
    row2d = lambda v: v.reshape(1, -1)

    za, sb, w_merge_bf, w_a_bf, w_b_bf, w_out_bf, w_pg_bf = _mixer_front(
        x2d, row2d(g_mix[0]), w_in[0].astype(BF16), w_pool[0].astype(BF16), row2d(pool_scale[0]),
        row2d(sgu_ln_g[0]), row2d(sgu_ln_b[0]), w_spatial[0], b_spatial[0][:, :, None],
        [w_merge_gate[0], w_branch_a[0], w_branch_b[0], w_out[0], w_ple_gate[0]])

    w_router = jnp.concatenate(
        [w_router_group[0], jnp.transpose(w_router_expert[0], (1, 0, 2)).reshape(d, N_EXPERTS)], axis=1)
    b_router = jnp.concatenate([b_router_group[0], b_router_expert[0].reshape(N_EXPERTS)])
    pad = LANES - w_router.shape[1]
    w_router = jnp.pad(w_router, ((0, 0), (0, pad)))
    b_router = jnp.pad(b_router, (0, pad))
    w_router_hi = w_router.astype(BF16)
    w_router_lo = (w_router - w_router_hi.astype(F32)).astype(BF16)
    w_router_split = jnp.concatenate([w_router_hi, w_router_lo], axis=1)

    x1, h2p, eid, wts = _mixer_back(
        x2d, za, sb, row2d(g_mix[0]), w_merge_bf, row2d(b_merge_gate[0]), w_a_bf, w_b_bf, w_out_bf,
        row2d(g_ffn[0]), w_router_split, row2d(b_router))

    block_e, next_e, n_used, block_cnt, block_base, sorted_tok, dest = _dispatch_plan(eid[:, :TOP_K])
    ys = _expert_ffn(block_e, next_e, n_used, block_cnt, block_base, sorted_tok, h2p, w_exp_gate[0],
                     w_exp_up[0], w_exp_down[0])
    out = _tail(dest, x1, wts, ys, p[0].reshape(n_tok, PLE_DIM), row2d(g_ple[0]), w_pg_bf,
                row2d(b_ple_gate[0]), w_ple_up[0].astype(BF16), row2d(g_final))
    return out.reshape(bsz, seq, d)
```

```python
import jax
import jax.numpy as jnp
from jax import lax
from jax.experimental import pallas as pl
from jax.experimental.pallas import tpu as pltpu
from jax.experimental.pallas import tpu_sc as plsc

F32 = jnp.float32
BF16 = jnp.bfloat16

D_MODEL = 2048
SEQ = 4096
EPS = 1e-6
PLE_DIM = 256
POOL_WINDOWS = (2, 4, 8, 16)
POOL_WIDTH = D_MODEL // 2
POOL_GROUP_DIM = POOL_WIDTH // len(POOL_WINDOWS)
POOL_HISTORY = max(POOL_WINDOWS)
SGU_BLOCK = 128
SGU_CHUNK = 64
SGU_GROUPS = 8
SGU_WIDTH = D_MODEL // 2
SGU_GROUP_DIM = SGU_WIDTH // SGU_GROUPS
N_IN = POOL_WIDTH + 2 * SGU_WIDTH
N_EXPERT_GROUPS = 4
EXPERTS_PER_GROUP = 8
N_EXPERTS = N_EXPERT_GROUPS * EXPERTS_PER_GROUP
TOP_K = 2
D_EXPERT = D_MODEL // 4
MOE_BLOCK_ROWS = 256

LANES = 128
BF16_SUBLANES = 16
MXU_COLS = 256
ROUTER_EXPERT_LANE0 = N_EXPERT_GROUPS
VMEM_LIMIT_BYTES = 56 * 1024 * 1024

TM_FRONT = 512
TM_BACK = 256
TM_TAIL = 256
MERGE_CHUNK = 1024
WEIGHT_DMA_PRIORITY = 1
GATHER_GROUP = 32
GATHER_AHEAD = 2
GATHER_SLOTS = GATHER_AHEAD + 1
PACKED_COLS = D_MODEL // 2
SC_GATHER_WINDOW = 128
SC_TABLE_PARTS = 4
SC_PART_COLS = D_MODEL // 2 // SC_TABLE_PARTS
GATHER_UNROLL = 8


def _rms_scale(x):
    return x * lax.rsqrt(jnp.mean(x * x, axis=-1, keepdims=True) + EPS)


def _pack_bf16_halves(v):
    bits = lambda part: lax.bitcast_convert_type(part.astype(BF16).astype(F32), jnp.uint32)
    return (bits(v[:, :PACKED_COLS]) >> 16) | bits(v[:, PACKED_COLS:])


def _unpack_bf16_halves(words):
    return (lax.bitcast_convert_type(words << 16, F32),
            lax.bitcast_convert_type(words & jnp.uint32(0xFFFF0000), F32))


def _resident(shape):
    zeros = (0,) * len(shape)
    return pl.BlockSpec(shape, lambda *_: zeros, pipeline_mode=pl.Buffered(1))


def _mixer_front_kernel(x_ref, gmix_ref, win_ref, wpool_ref, pscale_ref, lng_ref, lnb_ref, ws_ref,
                        bsp_ref, *rest):
    n_later = len(rest) // 2 - 1
    later_f32, (za_ref, sb_ref), later_bf16, hist_ref = (
        rest[:n_later], rest[n_later:n_later + 2], rest[n_later + 2:-1], rest[-1])
    tm = x_ref.shape[0]
    tiles_per_seq = SEQ // tm
    seq_tile = lax.rem(pl.program_id(0), tiles_per_seq)

    @pl.when(seq_tile == 0)
    def _():
        hist_ref[...] = jnp.zeros_like(hist_ref)

    for src, dst in zip(later_f32, later_bf16):
        dst[...] = src[...].astype(BF16)

    h = (_rms_scale(x_ref[...]) * gmix_ref[...]).astype(BF16)
    project = lambda lo, width: jnp.dot(h, win_ref[:, lo:lo + width], preferred_element_type=F32)
    v = jax.nn.gelu(project(POOL_WIDTH + SGU_WIDTH, SGU_WIDTH))
    vc = v - jnp.mean(v, axis=-1, keepdims=True)
    var = jnp.mean(vc * vc, axis=-1, keepdims=True)
    vn = (vc * lax.rsqrt(var + EPS) * lng_ref[...] + lnb_ref[...]).astype(BF16)
    u = jax.nn.gelu(project(POOL_WIDTH, SGU_WIDTH))
    a = project(0, POOL_WIDTH)

    ext = jnp.concatenate([hist_ref[...], a], axis=0)
    hist_ref[...] = a[tm - POOL_HISTORY:, :]
    frames = (seq_tile * tm + 1 + lax.broadcasted_iota(jnp.int32, (tm, 1), 0)).astype(F32)
    for gi, w in enumerate(POOL_WINDOWS):
        cols = slice(gi * POOL_GROUP_DIM, (gi + 1) * POOL_GROUP_DIM)
        s = ext[:, cols]
        k = 1
        while k < w:
            s = s + pltpu.roll(s, k, 0)
            k *= 2
        wsum = s[POOL_HISTORY:, :]
        zg = wsum / jnp.minimum(frames, float(w)) - a[:, cols]
        yg = jnp.dot(zg.astype(BF16), wpool_ref[gi], preferred_element_type=F32)
        za_ref[:, cols] = (yg * pscale_ref[:, cols]).astype(BF16)

    t_chunk = lax.broadcasted_iota(jnp.int32, (SGU_BLOCK, SGU_BLOCK), 0) // SGU_CHUNK
    s_chunk = lax.broadcasted_iota(jnp.int32, (SGU_BLOCK, SGU_BLOCK), 1) // SGU_CHUNK
    causal = s_chunk <= t_chunk
    nblk = tm // SGU_BLOCK
    for g in range(SGU_GROUPS):
        cols = slice(g * SGU_GROUP_DIM, (g + 1) * SGU_GROUP_DIM)
        wsg = jnp.where(causal, ws_ref[g], 0.0).astype(BF16)
        vg = jnp.concatenate([vn[j * SGU_BLOCK:(j + 1) * SGU_BLOCK, cols] for j in range(nblk)], axis=1)
        vm = jnp.dot(wsg, vg, preferred_element_type=F32) + bsp_ref[g]
        for j in range(nblk):
            rows = slice(j * SGU_BLOCK, (j + 1) * SGU_BLOCK)
            sb_ref[rows, cols] = (u[rows, cols] * vm[:, j * SGU_GROUP_DIM:(j + 1) * SGU_GROUP_DIM]).astype(BF16)


def _mixer_front(x2d, g_mix, w_in, w_pool, pool_scale, ln_g, ln_b, w_spatial, b_spatial, later_weights):
    n_tok = x2d.shape[0]
    tm = TM_FRONT
    n_steps = n_tok // tm
    row = lambda i: (i, 0)
    later_specs = [pl.BlockSpec((w.shape[0] // n_steps, w.shape[1]), row) for w in later_weights]
    assert all(w.shape[0] % (n_steps * BF16_SUBLANES) == 0 for w in later_weights)
    return pl.pallas_call(
        _mixer_front_kernel,
        grid=(n_steps,),
        in_specs=[
            pl.BlockSpec((tm, D_MODEL), row),
            _resident((1, D_MODEL)),
            _resident((D_MODEL, N_IN)),
            _resident((len(POOL_WINDOWS), POOL_GROUP_DIM, POOL_GROUP_DIM)),
            _resident((1, POOL_WIDTH)),
            _resident((1, SGU_WIDTH)),
            _resident((1, SGU_WIDTH)),
            _resident((SGU_GROUPS, SGU_BLOCK, SGU_BLOCK)),
            _resident((SGU_GROUPS, SGU_BLOCK, 1)),
        ] + later_specs,
        out_specs=[pl.BlockSpec((tm, POOL_WIDTH), row), pl.BlockSpec((tm, SGU_WIDTH), row)] + later_specs,
        out_shape=[jax.ShapeDtypeStruct((n_tok, POOL_WIDTH), BF16),
                   jax.ShapeDtypeStruct((n_tok, SGU_WIDTH), BF16)]
                  + [jax.ShapeDtypeStruct(w.shape, BF16) for w in later_weights],
        scratch_shapes=[pltpu.VMEM((POOL_HISTORY, POOL_WIDTH), F32)],
        compiler_params=pltpu.CompilerParams(dimension_semantics=("arbitrary",),
                                             vmem_limit_bytes=VMEM_LIMIT_BYTES),
        name="mixer_front",
    )(x2d, g_mix, w_in, w_pool, pool_scale, ln_g, ln_b, w_spatial, b_spatial, *later_weights)


def _route(logits):
    lane = lax.broadcasted_iota(jnp.int32, logits.shape, 1).astype(F32)
    neg = -jnp.inf
    far = float(LANES)

    def first_argmax(vals):
        top = jnp.max(vals, axis=-1, keepdims=True)
        return top, jnp.min(jnp.where(vals == top, lane, far), axis=-1, keepdims=True)

    is_grp = lane < float(N_EXPERT_GROUPS)
    g_top, g_idx = first_argmax(jnp.where(is_grp, logits, neg))
    g_den = jnp.sum(jnp.where(is_grp, jnp.exp(logits - g_top), 0.0), axis=-1, keepdims=True)
    grp_p = 1.0 / g_den
    lo = float(ROUTER_EXPERT_LANE0) + g_idx * float(EXPERTS_PER_GROUP)
    e_log = jnp.where(lane >= lo, jnp.where(lane < lo + float(EXPERTS_PER_GROUP), logits, neg), neg)
    t1, i1 = first_argmax(e_log)
    t2, i2 = first_argmax(jnp.where(lane == i1, neg, e_log))
    r = jnp.exp(t2 - t1)
    w1 = grp_p / (1.0 + r)
    w2 = grp_p * r / (1.0 + r)
    e1 = i1 - float(ROUTER_EXPERT_LANE0)
    e2 = i2 - float(ROUTER_EXPERT_LANE0)
    eid = jnp.where(lane == 0.0, e1, jnp.where(lane == 1.0, e2, 0.0)).astype(jnp.int32)
    wts = jnp.where(lane == 0.0, w1, jnp.where(lane == 1.0, w2, 0.0))
    return eid, wts


def _mixer_back_kernel(x_ref, za_ref, sb_ref, gmix_ref, wm_ref, bm_ref, wa_ref, wb_ref, wo_ref,
                       gffn_ref, wr_ref, br_ref, x1_ref, h2p_ref, eid_ref, wts_ref, carry_ref):
    i = pl.program_id(0)

    @pl.when(i == 0)
    def _():
        carry_ref[1] = jnp.zeros(carry_ref.shape[1:], F32)

    x1_prev = carry_ref[lax.rem(i + 1, 2)]
    h2 = _rms_scale(x1_prev) * gffn_ref[...]
    h2_hi = h2.astype(BF16)
    h2_lo = (h2 - h2_hi.astype(F32)).astype(BF16)
    packed = _pack_bf16_halves(h2)
    for q in range(SC_TABLE_PARTS):
        h2p_ref[q] = packed[:, q * SC_PART_COLS:(q + 1) * SC_PART_COLS]
    hi_terms = jnp.dot(h2_hi, wr_ref[...], preferred_element_type=F32)
    lo_term = jnp.dot(h2_lo, wr_ref[:, :LANES], preferred_element_type=F32)
    logits = hi_terms[:, :LANES] + hi_terms[:, LANES:] + lo_term + br_ref[...]
    eid, wts = _route(logits)
    eid_ref[...] = eid
    wts_ref[...] = wts

    x = x_ref[...]
    h = (_rms_scale(x) * gmix_ref[...]).astype(BF16)
    za = za_ref[...]
    sb = sb_ref[...]
    acc = jnp.zeros(x.shape, F32)
    for c in range(D_MODEL // MERGE_CHUNK):
        ca = slice(c * MERGE_CHUNK, (c + 1) * MERGE_CHUNK)
        cb = slice(D_MODEL + c * MERGE_CHUNK, D_MODEL + (c + 1) * MERGE_CHUNK)
        ga = jax.nn.sigmoid(jnp.dot(h, wm_ref[:, ca], preferred_element_type=F32) + bm_ref[:, ca])
        gb = jax.nn.sigmoid(jnp.dot(h, wm_ref[:, cb], preferred_element_type=F32) + bm_ref[:, cb])
        ya = jnp.dot(za, wa_ref[:, ca], preferred_element_type=F32)
        yb = jnp.dot(sb, wb_ref[:, ca], preferred_element_type=F32)
        merged = (ga * ya + gb * yb).astype(BF16)
        acc = acc + jnp.dot(merged, wo_ref[ca, :], preferred_element_type=F32)
    x1 = x + acc
    x1_ref[...] = x1
    carry_ref[lax.rem(i, 2)] = x1


def _mixer_back(x2d, za, sb, g_mix, w_merge, b_merge, w_a, w_b, w_out, g_ffn, w_router, b_router):
    n_tok = x2d.shape[0]
    tm = TM_BACK
    n_tiles = n_tok // tm
    row = lambda i: (jnp.minimum(i, n_tiles - 1), 0)
    prev = lambda i: (jnp.maximum(i - 1, 0), 0)
    return pl.pallas_call(
        _mixer_back_kernel,
        grid=(n_tiles + 1,),
        in_specs=[
            pl.BlockSpec((tm, D_MODEL), row),
            pl.BlockSpec((tm, POOL_WIDTH), row),
            pl.BlockSpec((tm, SGU_WIDTH), row),
            _resident((1, D_MODEL)),
            _resident((D_MODEL, 2 * D_MODEL)),
            _resident((1, 2 * D_MODEL)),
            _resident((POOL_WIDTH, D_MODEL)),
            _resident((SGU_WIDTH, D_MODEL)),
            _resident((D_MODEL, D_MODEL)),
            _resident((1, D_MODEL)),
            _resident((D_MODEL, 2 * LANES)),
            _resident((1, LANES)),
        ],
        out_specs=[pl.BlockSpec((tm, D_MODEL), row),
                   pl.BlockSpec((SC_TABLE_PARTS, tm, SC_PART_COLS), lambda i: (0, jnp.maximum(i - 1, 0), 0)),
                   pl.BlockSpec((tm, LANES), prev), pl.BlockSpec((tm, LANES), prev)],
        out_shape=[jax.ShapeDtypeStruct((n_tok, D_MODEL), F32),
                   jax.ShapeDtypeStruct((SC_TABLE_PARTS, n_tok, SC_PART_COLS), jnp.uint32),
                   jax.ShapeDtypeStruct((n_tok, LANES), jnp.int32),
                   jax.ShapeDtypeStruct((n_tok, LANES), F32)],
        scratch_shapes=[pltpu.VMEM((2, tm, D_MODEL), F32)],
        compiler_params=pltpu.CompilerParams(dimension_semantics=("arbitrary",),
                                             vmem_limit_bytes=VMEM_LIMIT_BYTES),
        name="mixer_back",
    )(x2d, za, sb, g_mix, w_merge, b_merge, w_a, w_b, w_out, g_ffn, w_router, b_router)


def _row_gather_copy(src_hbm, src_row, dst_vmem, dst_row, sem):
    return pltpu.make_async_copy(src_hbm.at[pl.ds(src_row, 1), :], dst_vmem.at[pl.ds(dst_row, 1), :], sem)


def _expert_kernel(be_ref, nexte_ref, nused_ref, cnt_ref, base_ref, tok_ref, h2p_hbm, wg_hbm, wu_hbm, wd_hbm,
                   ys_ref, xbuf, sems, wg_st, wu_st, wd_st, wsems, wg_bf, wu_bf, wd_bf):
    rows = MOE_BLOCK_ROWS
    b = pl.program_id(0)
    n_blocks = pl.num_programs(0)
    n_used = nused_ref[0]
    slot = lax.rem(b, GATHER_SLOTS)
    groups = rows // GATHER_GROUP

    def start_rows(blk, cnt, slt):
        base = base_ref[blk]
        for g in range(groups):
            @pl.when(g * GATHER_GROUP < cnt)
            def _():
                for r in range(g * GATHER_GROUP, (g + 1) * GATHER_GROUP):
                    _row_gather_copy(h2p_hbm, tok_ref[base + r], xbuf.at[slt], r, sems.at[slt]).start()

    def wait_rows(cnt, slt):
        for g in range(groups):
            @pl.when(g * GATHER_GROUP < cnt)
            def _():
                part = pl.ds(g * GATHER_GROUP, GATHER_GROUP)
                pltpu.make_async_copy(h2p_hbm.at[part, :], xbuf.at[slt, part, :], sems.at[slt]).wait()

    def weight_copies(e):
        return [pltpu.make_async_copy(src.at[e], dst, wsems.at[j])
                for j, (src, dst) in enumerate(((wg_hbm, wg_st), (wu_hbm, wu_st), (wd_hbm, wd_st)))]

    @pl.when(b == 0)
    def _():
        xbuf[...] = jnp.zeros_like(xbuf)
        for ahead in range(GATHER_AHEAD):
            start_rows(ahead, cnt_ref[ahead], ahead)

    @pl.when(b < n_used)
    def _():
        @pl.when(b == 0)
        def _():
            for cp in weight_copies(be_ref[0]):
                cp.start()

        @pl.when(jnp.logical_or(b == 0, be_ref[b] != be_ref[jnp.maximum(b - 1, 0)]))
        def _():
            for cp in weight_copies(be_ref[b]):
                cp.wait()
            wg_bf[...] = wg_st[...].astype(BF16)
            wu_bf[...] = wu_st[...].astype(BF16)
            wd_bf[...] = wd_st[...].astype(BF16)
            nxt_e = nexte_ref[b]

            @pl.when(nxt_e >= 0)
            def _():
                for cp in weight_copies(nxt_e):
                    cp.start(priority=WEIGHT_DMA_PRIORITY)

        ahead_blk = jnp.minimum(b + GATHER_AHEAD, n_blocks - 1)
        ahead_cnt = jnp.where(b + GATHER_AHEAD < n_blocks, cnt_ref[ahead_blk], 0)
        start_rows(ahead_blk, ahead_cnt, lax.rem(b + GATHER_AHEAD, GATHER_SLOTS))
        wait_rows(cnt_ref[b], slot)

        h2 = jnp.concatenate([half.astype(BF16) for half in _unpack_bf16_halves(xbuf[slot])], axis=1)
        gate = jnp.dot(h2, wg_bf[...], preferred_element_type=F32)
        up = jnp.dot(h2, wu_bf[...], preferred_element_type=F32)
        hid = (jax.nn.silu(gate) * up).astype(BF16)
        y = jnp.dot(hid, wd_bf[...], preferred_element_type=F32)
        ys_ref[...] = _pack_bf16_halves(y)

    @pl.when(b >= n_used)
    def _():
        ys_ref[...] = jnp.zeros_like(ys_ref)


def _expert_ffn(block_e, next_e, n_used, block_cnt, block_base, sorted_tok, h2p, w_g, w_u, w_d):
    n_blocks = block_e.shape[0]
    n_rows = n_blocks * MOE_BLOCK_ROWS
    grid_spec = pltpu.PrefetchScalarGridSpec(
        num_scalar_prefetch=6,
        grid=(n_blocks,),
        in_specs=[
            pl.BlockSpec(memory_space=pl.ANY),
            pl.BlockSpec(memory_space=pl.ANY),
            pl.BlockSpec(memory_space=pl.ANY),
            pl.BlockSpec(memory_space=pl.ANY),
        ],
        out_specs=pl.BlockSpec((MOE_BLOCK_ROWS, PACKED_COLS), lambda b, *_: (b, 0)),
        scratch_shapes=[
            pltpu.VMEM((GATHER_SLOTS, MOE_BLOCK_ROWS, PACKED_COLS), jnp.uint32),
            pltpu.SemaphoreType.DMA((GATHER_SLOTS,)),
            pltpu.VMEM((D_MODEL, D_EXPERT), F32),
            pltpu.VMEM((D_MODEL, D_EXPERT), F32),
            pltpu.VMEM((D_EXPERT, D_MODEL), F32),
            pltpu.SemaphoreType.DMA((3,)),
            pltpu.VMEM((D_MODEL, D_EXPERT), BF16),
            pltpu.VMEM((D_MODEL, D_EXPERT), BF16),
            pltpu.VMEM((D_EXPERT, D_MODEL), BF16),
        ],
    )
    return pl.pallas_call(
        _expert_kernel,
        grid_spec=grid_spec,
        out_shape=jax.ShapeDtypeStruct((n_rows, PACKED_COLS), jnp.uint32),
        compiler_params=pltpu.CompilerParams(dimension_semantics=("arbitrary",),
                                             vmem_limit_bytes=VMEM_LIMIT_BYTES),
        name="expert_ffn",
    )(block_e, next_e, n_used, block_cnt, block_base, sorted_tok, h2p, w_g, w_u, w_d)


def _sc_gather_rows(table, row_idx):
    n = row_idx.shape[0]
    mesh = plsc.VectorSubcoreMesh(core_axis_name="core", subcore_axis_name="subcore")

    @pl.kernel(out_type=jax.ShapeDtypeStruct((n, table.shape[1]), table.dtype), mesh=mesh)
    def gather(x_hbm, i_hbm, o_hbm):
        def body(i_vmem, o_vmem):
            pltpu.sync_copy(x_hbm.at[i_vmem.at[0]], o_vmem)

        pltpu.emit_pipeline(
            body,
            grid=(n // SC_GATHER_WINDOW,),
            in_specs=[pl.BlockSpec((1, SC_GATHER_WINDOW), lambda i: (0, i))],
            out_specs=[pl.BlockSpec((SC_GATHER_WINDOW, table.shape[1]), lambda i: (i, 0))],
            core_axis_name=("core", "subcore"),
            dimension_semantics=(pltpu.PARALLEL,),
        )(i_hbm, o_hbm)

    return gather(table, row_idx.reshape(1, n))


def _expert_dense_kernel(be_ref, nexte_ref, nused_ref, *refs):
    xs_refs, (wg_hbm, wu_hbm, wd_hbm, ys_ref, wg_st, wu_st, wd_st, wsems, wg_bf, wu_bf, wd_bf) = (
        refs[:SC_TABLE_PARTS], refs[SC_TABLE_PARTS:])
    b = pl.program_id(0)
    n_used = nused_ref[0]

    def weight_copies(e):
        return [pltpu.make_async_copy(src.at[e], dst, wsems.at[j])
                for j, (src, dst) in enumerate(((wg_hbm, wg_st), (wu_hbm, wu_st), (wd_hbm, wd_st)))]

    @pl.when(b < n_used)
    def _():
        @pl.when(b == 0)
        def _():
            for cp in weight_copies(be_ref[0]):
                cp.start()

        @pl.when(jnp.logical_or(b == 0, be_ref[b] != be_ref[jnp.maximum(b - 1, 0)]))
        def _():
            for cp in weight_copies(be_ref[b]):
                cp.wait()
            wg_bf[...] = wg_st[...].astype(BF16)
            wu_bf[...] = wu_st[...].astype(BF16)
            wd_bf[...] = wd_st[...].astype(BF16)
            nxt_e = nexte_ref[b]

            @pl.when(nxt_e >= 0)
            def _():
                for cp in weight_copies(nxt_e):
                    cp.start(priority=WEIGHT_DMA_PRIORITY)

        words = jnp.concatenate([r[...] for r in xs_refs], axis=1)
        h2 = jnp.concatenate([half.astype(BF16) for half in _unpack_bf16_halves(words)], axis=1)
        gate = jnp.dot(h2, wg_bf[...], preferred_element_type=F32)
        up = jnp.dot(h2, wu_bf[...], preferred_element_type=F32)
        hid = (jax.nn.silu(gate) * up).astype(BF16)
        y = jnp.dot(hid, wd_bf[...], preferred_element_type=F32)
        ys_ref[...] = _pack_bf16_halves(y)

    @pl.when(b >= n_used)
    def _():
        ys_ref[...] = jnp.zeros_like(ys_ref)


def _expert_dense_ffn(block_e, next_e, n_used, xs_parts, w_g, w_u, w_d):
    n_rows = xs_parts[0].shape[0]
    n_blocks = n_rows // MOE_BLOCK_ROWS
    blk = lambda b, *_: (b, 0)
    grid_spec = pltpu.PrefetchScalarGridSpec(
        num_scalar_prefetch=3,
        grid=(n_blocks,),
        in_specs=[pl.BlockSpec((MOE_BLOCK_ROWS, SC_PART_COLS), blk) for _ in xs_parts] + [
            pl.BlockSpec(memory_space=pl.ANY),
            pl.BlockSpec(memory_space=pl.ANY),
            pl.BlockSpec(memory_space=pl.ANY),
        ],
        out_specs=pl.BlockSpec((MOE_BLOCK_ROWS, PACKED_COLS), blk),
        scratch_shapes=[
            pltpu.VMEM((D_MODEL, D_EXPERT), F32),
            pltpu.VMEM((D_MODEL, D_EXPERT), F32),
            pltpu.VMEM((D_EXPERT, D_MODEL), F32),
            pltpu.SemaphoreType.DMA((3,)),
            pltpu.VMEM((D_MODEL, D_EXPERT), BF16),
            pltpu.VMEM((D_MODEL, D_EXPERT), BF16),
            pltpu.VMEM((D_EXPERT, D_MODEL), BF16),
        ],
    )
    return pl.pallas_call(
        _expert_dense_kernel,
        grid_spec=grid_spec,
        out_shape=jax.ShapeDtypeStruct((n_rows, PACKED_COLS), jnp.uint32),
        compiler_params=pltpu.CompilerParams(dimension_semantics=("arbitrary",),
                                             vmem_limit_bytes=VMEM_LIMIT_BYTES),
        name="expert_dense_ffn",
    )(block_e, next_e, n_used, *xs_parts, w_g, w_u, w_d)


def _tail_kernel(dest_ref, x1_ref, wts_ref, ys_hbm, p_ref, gple_ref, wpg_ref, bpg_ref, wpu_ref, gfin_ref,
                 out_ref, ybuf, sems):
    tm = x1_ref.shape[0]
    i = pl.program_id(0)
    last = pl.num_programs(0) - 1
    slot = lax.rem(i, 2)

    def start_token(step, slt, r):
        for k in range(TOP_K):
            _row_gather_copy(ys_hbm, dest_ref[TOP_K * (step * tm + r) + k], ybuf.at[slt], k * tm + r,
                             sems.at[slt]).start()

    def wait_tile(slt):
        pltpu.make_async_copy(ys_hbm.at[pl.ds(0, TOP_K * tm), :], ybuf.at[slt], sems.at[slt]).wait()

    @pl.when(i == 0)
    def _():
        def body(r, carry):
            start_token(0, 0, r)
            return carry
        lax.fori_loop(0, tm, body, 0, unroll=GATHER_UNROLL)

    wait_tile(slot)
    nxt = jnp.minimum(i + 1, last)
    other = 1 - slot
    wts = wts_ref[...]
    x2 = x1_ref[...]
    for k in range(TOP_K):
        y = jnp.concatenate(_unpack_bf16_halves(ybuf[slot, k * tm:(k + 1) * tm, :]), axis=1)
        x2 = x2 + wts[:, k:k + 1] * y
    hn = (_rms_scale(x2) * gple_ref[...]).astype(BF16)
    up = jnp.dot(p_ref[...].astype(BF16), wpu_ref[...], preferred_element_type=F32)
    n_slabs = D_MODEL // MXU_COLS
    per = tm // n_slabs
    x3 = []
    for n in range(n_slabs):
        cols = slice(n * MXU_COLS, (n + 1) * MXU_COLS)
        for r in range(n * per, (n + 1) * per):
            start_token(nxt, other, r)
        gate = jax.nn.sigmoid(jnp.dot(hn, wpg_ref[:, cols], preferred_element_type=F32) + bpg_ref[:, cols])
        x3.append(x2[:, cols] + gate * up[:, cols])
    x3 = jnp.concatenate(x3, axis=1)
    out_ref[...] = _rms_scale(x3) * gfin_ref[...]

    @pl.when(i == last)
    def _():
        wait_tile(other)


def _tail(dest, x1, wts, ys, p2d, g_ple, w_pg, b_pg, w_pu, g_final):
    n_tok = x1.shape[0]
    tm = TM_TAIL
    row = lambda i, *_: (i, 0)
    const = lambda i, *_: (0, 0)
    grid_spec = pltpu.PrefetchScalarGridSpec(
        num_scalar_prefetch=1,
        grid=(n_tok // tm,),
        in_specs=[
            pl.BlockSpec((tm, D_MODEL), row),
            pl.BlockSpec((tm, LANES), row),
            pl.BlockSpec(memory_space=pl.ANY),
            pl.BlockSpec((tm, PLE_DIM), row),
            pl.BlockSpec((1, D_MODEL), const),
            pl.BlockSpec((D_MODEL, D_MODEL), const, pipeline_mode=pl.Buffered(1)),
            pl.BlockSpec((1, D_MODEL), const),
            pl.BlockSpec((PLE_DIM, D_MODEL), const),
            pl.BlockSpec((1, D_MODEL), const),
        ],
        out_specs=pl.BlockSpec((tm, D_MODEL), row),
        scratch_shapes=[pltpu.VMEM((2, TOP_K * tm, PACKED_COLS), jnp.uint32), pltpu.SemaphoreType.DMA((2,))],
    )
    return pl.pallas_call(
        _tail_kernel,
        grid_spec=grid_spec,
        out_shape=jax.ShapeDtypeStruct((n_tok, D_MODEL), F32),
        compiler_params=pltpu.CompilerParams(dimension_semantics=("arbitrary",),
                                             vmem_limit_bytes=VMEM_LIMIT_BYTES),
        name="tail",
    )(dest, x1, wts, ys, p2d, g_ple, w_pg, b_pg, w_pu, g_final)


def _dispatch_plan(expert_id):
    n_tok = expert_id.shape[0]
    n_assign = n_tok * TOP_K
    n_blocks = -(-n_assign // MOE_BLOCK_ROWS) + N_EXPERTS
    n_rows = n_blocks * MOE_BLOCK_ROWS
    i32 = jnp.int32
    flat_e = expert_id.reshape(-1)
    experts = jnp.arange(N_EXPERTS, dtype=i32)
    assign = jnp.arange(n_assign, dtype=i32)
    se, order = lax.sort((flat_e, assign), num_keys=1)
    onehot_sorted = se[:, None] == experts[None, :]
    counts = jnp.sum(onehot_sorted.astype(i32), axis=0)
    padded = (counts + MOE_BLOCK_ROWS - 1) // MOE_BLOCK_ROWS * MOE_BLOCK_ROWS
    pad_end = jnp.cumsum(padded)
    pad_start = pad_end - padded
    start = jnp.cumsum(counts) - counts
    row_of_sorted = assign + jnp.sum(jnp.where(onehot_sorted, (pad_start - start)[None, :], 0), axis=1)
    _, dest = lax.sort((order, row_of_sorted), num_keys=1)
    n_used = pad_end[-1] // MOE_BLOCK_ROWS
    rows = jnp.arange(n_rows, dtype=i32)
    row_e = jnp.minimum(jnp.sum((pad_end[None, :] <= rows[:, None]).astype(i32), axis=1), N_EXPERTS - 1)
    onehot_row = row_e[:, None] == experts[None, :]
    pick = lambda table: jnp.sum(jnp.where(onehot_row, table[None, :], 0), axis=1)
    offset = rows - pick(pad_start)
    valid = offset < pick(counts)
    sorted_tok = jnp.concatenate([order // TOP_K, jnp.zeros((MOE_BLOCK_ROWS,), i32)])
    block_base = jnp.clip((pick(start) + offset).reshape(n_blocks, MOE_BLOCK_ROWS)[:, 0], 0, n_assign)
    block_e = row_e.reshape(n_blocks, MOE_BLOCK_ROWS)[:, 0]
    block_cnt = jnp.sum(valid.reshape(n_blocks, MOE_BLOCK_ROWS).astype(i32), axis=1)
    block_e = jnp.where(jnp.arange(n_blocks) < n_used, block_e, block_e[jnp.maximum(n_used - 1, 0)])
    later_used = jnp.logical_and(experts[None, :] > experts[:, None], (counts > 0)[None, :])
    next_used = jnp.min(jnp.where(later_used, experts[None, :], N_EXPERTS), axis=1)
    next_used = jnp.where(next_used < N_EXPERTS, next_used, -1)
    next_e = jnp.sum(jnp.where(block_e[:, None] == experts[None, :], next_used[None, :], 0), axis=1)
    row_tok = jnp.where(valid, sorted_tok[jnp.clip(pick(start) + offset, 0, n_assign - 1)], rows % n_tok)
    return (block_e.astype(i32), next_e.astype(i32), n_used.astype(i32).reshape(1), block_cnt,
            block_base.astype(i32), sorted_tok.astype(i32), dest.astype(i32), row_tok.astype(i32))


def kernel(x, p, g_mix, w_in, w_pool, pool_scale, w_branch_a, sgu_ln_g, sgu_ln_b, w_spatial, b_spatial, w_branch_b, w_merge_gate, b_merge_gate, w_out, g_ffn, w_router_group, b_router_group, w_router_expert, b_router_expert, w_exp_gate, w_exp_up, w_exp_down, g_ple, w_ple_gate, b_ple_gate, w_ple_up, g_final):
    bsz, seq, d = x.shape
    assert (seq, d) == (SEQ, D_MODEL) and g_mix.shape[0] == 1
    n_tok = bsz * seq
    x2d = x.reshape(n_tok, d)
    row2d = lambda v: v.reshape(1, -1)

    za, sb, w_merge_bf, w_a_bf, w_b_bf, w_out_bf, w_pg_bf = _mixer_front(
        x2d, row2d(g_mix[0]), w_in[0].astype(BF16), w_pool[0].astype(BF16), row2d(pool_scale[0]),
        row2d(sgu_ln_g[0]), row2d(sgu_ln_b[0]), w_spatial[0], b_spatial[0][:, :, None],
        [w_merge_gate[0], w_branch_a[0], w_branch_b[0], w_out[0], w_ple_gate[0]])

    w_router = jnp.concatenate(
        [w_router_group[0], jnp.transpose(w_router_expert[0], (1, 0, 2)).reshape(d, N_EXPERTS)], axis=1)
    b_router = jnp.concatenate([b_router_group[0], b_router_expert[0].reshape(N_EXPERTS)])
    pad = LANES - w_router.shape[1]
    w_router = jnp.pad(w_router, ((0, 0), (0, pad)))
    b_router = jnp.pad(b_router, (0, pad))
    w_router_hi = w_router.astype(BF16)
    w_router_lo = (w_router - w_router_hi.astype(F32)).astype(BF16)
    w_router_split = jnp.concatenate([w_router_hi, w_router_lo], axis=1)

    x1, h2p, eid, wts = _mixer_back(
        x2d, za, sb, row2d(g_mix[0]), w_merge_bf, row2d(b_merge_gate[0]), w_a_bf, w_b_bf, w_out_bf,
        row2d(g_ffn[0]), w_router_split, row2d(b_router))

    block_e, next_e, n_used, block_cnt, block_base, sorted_tok, dest, row_tok = _dispatch_plan(eid[:, :TOP_K])
    xs_parts = [_sc_gather_rows(h2p[q], row_tok) for q in range(SC_TABLE_PARTS)]
    ys = _expert_dense_ffn(block_e, next_e, n_used, xs_parts, w_exp_gate[0], w_exp_up[0], w_exp_down[0])
    out = _tail(dest, x1, wts, ys, p[0].reshape(n_tok, PLE_DIM), row2d(g_ple[0]), w_pg_bf,
                row2d(b_ple_gate[0]), w_ple_up[0].astype(BF16), row2d(g_final))
    return out.reshape(bsz, seq, d)
```

```python
import jax
import jax.numpy as jnp
from jax import lax
from jax.experimental import pallas as pl
from jax.experimental.pallas import tpu as pltpu

F32 = jnp.float32
BF16 = jnp.bfloat16

D_MODEL = 2048
SEQ = 4096
EPS = 1e-6
PLE_DIM = 256
POOL_WINDOWS = (2, 4, 8, 16)
POOL_WIDTH = D_MODEL // 2
POOL_GROUP_DIM = POOL_WIDTH // len(POOL_WINDOWS)
POOL_HISTORY = max(POOL_WINDOWS)
SGU_BLOCK = 128
SGU_CHUNK = 64
SGU_GROUPS = 8
SGU_WIDTH = D_MODEL // 2
SGU_GROUP_DIM = SGU_WIDTH // SGU_GROUPS
N_IN = POOL_WIDTH + 2 * SGU_WIDTH
N_EXPERT_GROUPS = 4
EXPERTS_PER_GROUP = 8
N_EXPERTS = N_EXPERT_GROUPS * EXPERTS_PER_GROUP
TOP_K = 2
D_EXPERT = D_MODEL // 4
MOE_BLOCK_ROWS = 256

LANES = 128
BF16_SUBLANES = 16
MXU_COLS = 256
ROUTER_EXPERT_LANE0 = N_EXPERT_GROUPS
VMEM_LIMIT_BYTES = 56 * 1024 * 1024

TM_FRONT = 512
TM_BACK = 256
TM_TAIL = 512
MERGE_CHUNK = 1024
WEIGHT_DMA_PRIORITY = 1
GATHER_GROUP = 32
GATHER_AHEAD = 2
GATHER_SLOTS = GATHER_AHEAD + 1
PACKED_COLS = D_MODEL // 2
GATHER_UNROLL = 8


def _rms_scale(x):
    return x * lax.rsqrt(jnp.mean(x * x, axis=-1, keepdims=True) + EPS)


def _pack_bf16_halves(v):
    bits = lambda part: lax.bitcast_convert_type(part.astype(BF16).astype(F32), jnp.uint32)
    return (bits(v[:, :PACKED_COLS]) >> 16) | bits(v[:, PACKED_COLS:])


def _unpack_bf16_halves(words):
    return (lax.bitcast_convert_type(words << 16, F32),
            lax.bitcast_convert_type(words & jnp.uint32(0xFFFF0000), F32))


def _resident(shape):
    zeros = (0,) * len(shape)
    return pl.BlockSpec(shape, lambda *_: zeros, pipeline_mode=pl.Buffered(1))


def _mixer_front_kernel(x_ref, gmix_ref, win_ref, wpool_ref, pscale_ref, lng_ref, lnb_ref, ws_ref,
                        bsp_ref, *rest):
    n_later = len(rest) // 2 - 1
    later_f32, (za_ref, sb_ref), later_bf16, hist_ref = (
        rest[:n_later], rest[n_later:n_later + 2], rest[n_later + 2:-1], rest[-1])
    tm = x_ref.shape[0]
    tiles_per_seq = SEQ // tm
    seq_tile = lax.rem(pl.program_id(0), tiles_per_seq)

    @pl.when(seq_tile == 0)
    def _():
        hist_ref[...] = jnp.zeros_like(hist_ref)

    for src, dst in zip(later_f32, later_bf16):
        dst[...] = src[...].astype(BF16)

    h = (_rms_scale(x_ref[...]) * gmix_ref[...]).astype(BF16)
    project = lambda lo, width: jnp.dot(h, win_ref[:, lo:lo + width], preferred_element_type=F32)
    v = jax.nn.gelu(project(POOL_WIDTH + SGU_WIDTH, SGU_WIDTH))
    vc = v - jnp.mean(v, axis=-1, keepdims=True)
    var = jnp.mean(vc * vc, axis=-1, keepdims=True)
    vn = (vc * lax.rsqrt(var + EPS) * lng_ref[...] + lnb_ref[...]).astype(BF16)
    u = jax.nn.gelu(project(POOL_WIDTH, SGU_WIDTH))
    a = project(0, POOL_WIDTH)

    ext = jnp.concatenate([hist_ref[...], a], axis=0)
    hist_ref[...] = a[tm - POOL_HISTORY:, :]
    frames = (seq_tile * tm + 1 + lax.broadcasted_iota(jnp.int32, (tm, 1), 0)).astype(F32)
    for gi, w in enumerate(POOL_WINDOWS):
        cols = slice(gi * POOL_GROUP_DIM, (gi + 1) * POOL_GROUP_DIM)
        s = ext[:, cols]
        k = 1
        while k < w:
            s = s + pltpu.roll(s, k, 0)
            k *= 2
        wsum = s[POOL_HISTORY:, :]
        zg = wsum / jnp.minimum(frames, float(w)) - a[:, cols]
        yg = jnp.dot(zg.astype(BF16), wpool_ref[gi], preferred_element_type=F32)
        za_ref[:, cols] = (yg * pscale_ref[:, cols]).astype(BF16)

    t_chunk = lax.broadcasted_iota(jnp.int32, (SGU_BLOCK, SGU_BLOCK), 0) // SGU_CHUNK
    s_chunk = lax.broadcasted_iota(jnp.int32, (SGU_BLOCK, SGU_BLOCK), 1) // SGU_CHUNK
    causal = s_chunk <= t_chunk
    nblk = tm // SGU_BLOCK
    for g in range(SGU_GROUPS):
        cols = slice(g * SGU_GROUP_DIM, (g + 1) * SGU_GROUP_DIM)
        wsg = jnp.where(causal, ws_ref[g], 0.0).astype(BF16)
        vg = jnp.concatenate([vn[j * SGU_BLOCK:(j + 1) * SGU_BLOCK, cols] for j in range(nblk)], axis=1)
        vm = jnp.dot(wsg, vg, preferred_element_type=F32) + bsp_ref[g]
        for j in range(nblk):
            rows = slice(j * SGU_BLOCK, (j + 1) * SGU_BLOCK)
            sb_ref[rows, cols] = (u[rows, cols] * vm[:, j * SGU_GROUP_DIM:(j + 1) * SGU_GROUP_DIM]).astype(BF16)


def _mixer_front(x2d, g_mix, w_in, w_pool, pool_scale, ln_g, ln_b, w_spatial, b_spatial, later_weights):
    n_tok = x2d.shape[0]
    tm = TM_FRONT
    n_steps = n_tok // tm
    row = lambda i: (i, 0)
    later_specs = [pl.BlockSpec((w.shape[0] // n_steps, w.shape[1]), row) for w in later_weights]
    assert all(w.shape[0] % (n_steps * BF16_SUBLANES) == 0 for w in later_weights)
    return pl.pallas_call(
        _mixer_front_kernel,
        grid=(n_steps,),
        in_specs=[
            pl.BlockSpec((tm, D_MODEL), row),
            _resident((1, D_MODEL)),
            _resident((D_MODEL, N_IN)),
            _resident((len(POOL_WINDOWS), POOL_GROUP_DIM, POOL_GROUP_DIM)),
            _resident((1, POOL_WIDTH)),
            _resident((1, SGU_WIDTH)),
            _resident((1, SGU_WIDTH)),
            _resident((SGU_GROUPS, SGU_BLOCK, SGU_BLOCK)),
            _resident((SGU_GROUPS, SGU_BLOCK, 1)),
        ] + later_specs,
        out_specs=[pl.BlockSpec((tm, POOL_WIDTH), row), pl.BlockSpec((tm, SGU_WIDTH), row)] + later_specs,
        out_shape=[jax.ShapeDtypeStruct((n_tok, POOL_WIDTH), BF16),
                   jax.ShapeDtypeStruct((n_tok, SGU_WIDTH), BF16)]
                  + [jax.ShapeDtypeStruct(w.shape, BF16) for w in later_weights],
        scratch_shapes=[pltpu.VMEM((POOL_HISTORY, POOL_WIDTH), F32)],
        compiler_params=pltpu.CompilerParams(dimension_semantics=("arbitrary",),
                                             vmem_limit_bytes=VMEM_LIMIT_BYTES),
        name="mixer_front",
    )(x2d, g_mix, w_in, w_pool, pool_scale, ln_g, ln_b, w_spatial, b_spatial, *later_weights)


def _route(logits):
    lane = lax.broadcasted_iota(jnp.int32, logits.shape, 1).astype(F32)
    neg = -jnp.inf
    far = float(LANES)

    def first_argmax(vals):
        top = jnp.max(vals, axis=-1, keepdims=True)
        return top, jnp.min(jnp.where(vals == top, lane, far), axis=-1, keepdims=True)

    is_grp = lane < float(N_EXPERT_GROUPS)
    g_top, g_idx = first_argmax(jnp.where(is_grp, logits, neg))
    g_den = jnp.sum(jnp.where(is_grp, jnp.exp(logits - g_top), 0.0), axis=-1, keepdims=True)
    grp_p = 1.0 / g_den
    lo = float(ROUTER_EXPERT_LANE0) + g_idx * float(EXPERTS_PER_GROUP)
    e_log = jnp.where(lane >= lo, jnp.where(lane < lo + float(EXPERTS_PER_GROUP), logits, neg), neg)
    t1, i1 = first_argmax(e_log)
    t2, i2 = first_argmax(jnp.where(lane == i1, neg, e_log))
    r = jnp.exp(t2 - t1)
    w1 = grp_p / (1.0 + r)
    w2 = grp_p * r / (1.0 + r)
    e1 = i1 - float(ROUTER_EXPERT_LANE0)
    e2 = i2 - float(ROUTER_EXPERT_LANE0)
    eid = jnp.where(lane == 0.0, e1, jnp.where(lane == 1.0, e2, 0.0)).astype(jnp.int32)
    wts = jnp.where(lane == 0.0, w1, jnp.where(lane == 1.0, w2, 0.0))
    return eid, wts


def _mixer_back_kernel(x_ref, za_ref, sb_ref, gmix_ref, wm_ref, bm_ref, wa_ref, wb_ref, wo_ref,
                       gffn_ref, wr_ref, br_ref, x1_ref, h2p_ref, eid_ref, wts_ref, carry_ref):
    i = pl.program_id(0)

    @pl.when(i == 0)
    def _():
        carry_ref[1] = jnp.zeros(carry_ref.shape[1:], F32)

    x1_prev = carry_ref[lax.rem(i + 1, 2)]
    h2 = _rms_scale(x1_prev) * gffn_ref[...]
    h2_hi = h2.astype(BF16)
    h2_lo = (h2 - h2_hi.astype(F32)).astype(BF16)
    h2p_ref[...] = _pack_bf16_halves(h2)
    hi_terms = jnp.dot(h2_hi, wr_ref[...], preferred_element_type=F32)
    lo_term = jnp.dot(h2_lo, wr_ref[:, :LANES], preferred_element_type=F32)
    logits = hi_terms[:, :LANES] + hi_terms[:, LANES:] + lo_term + br_ref[...]
    eid, wts = _route(logits)
    eid_ref[...] = eid
    wts_ref[...] = wts

    x = x_ref[...]
    h = (_rms_scale(x) * gmix_ref[...]).astype(BF16)
    za = za_ref[...]
    sb = sb_ref[...]
    acc = jnp.zeros(x.shape, F32)
    for c in range(D_MODEL // MERGE_CHUNK):
        ca = slice(c * MERGE_CHUNK, (c + 1) * MERGE_CHUNK)
        cb = slice(D_MODEL + c * MERGE_CHUNK, D_MODEL + (c + 1) * MERGE_CHUNK)
        ga = jax.nn.sigmoid(jnp.dot(h, wm_ref[:, ca], preferred_element_type=F32) + bm_ref[:, ca])
        gb = jax.nn.sigmoid(jnp.dot(h, wm_ref[:, cb], preferred_element_type=F32) + bm_ref[:, cb])
        ya = jnp.dot(za, wa_ref[:, ca], preferred_element_type=F32)
        yb = jnp.dot(sb, wb_ref[:, ca], preferred_element_type=F32)
        merged = (ga * ya + gb * yb).astype(BF16)
        acc = acc + jnp.dot(merged, wo_ref[ca, :], preferred_element_type=F32)
    x1 = x + acc
    x1_ref[...] = x1
    carry_ref[lax.rem(i, 2)] = x1


def _mixer_back(x2d, za, sb, g_mix, w_merge, b_merge, w_a, w_b, w_out, g_ffn, w_router, b_router):
    n_tok = x2d.shape[0]
    tm = TM_BACK
    n_tiles = n_tok // tm
    row = lambda i: (jnp.minimum(i, n_tiles - 1), 0)
    prev = lambda i: (jnp.maximum(i - 1, 0), 0)
    return pl.pallas_call(
        _mixer_back_kernel,
        grid=(n_tiles + 1,),
        in_specs=[
            pl.BlockSpec((tm, D_MODEL), row),
            pl.BlockSpec((tm, POOL_WIDTH), row),
            pl.BlockSpec((tm, SGU_WIDTH), row),
            _resident((1, D_MODEL)),
            _resident((D_MODEL, 2 * D_MODEL)),
            _resident((1, 2 * D_MODEL)),
            _resident((POOL_WIDTH, D_MODEL)),
            _resident((SGU_WIDTH, D_MODEL)),
            _resident((D_MODEL, D_MODEL)),
            _resident((1, D_MODEL)),
            _resident((D_MODEL, 2 * LANES)),
            _resident((1, LANES)),
        ],
        out_specs=[pl.BlockSpec((tm, D_MODEL), row), pl.BlockSpec((tm, PACKED_COLS), prev),
                   pl.BlockSpec((tm, LANES), prev), pl.BlockSpec((tm, LANES), prev)],
        out_shape=[jax.ShapeDtypeStruct((n_tok, D_MODEL), F32),
                   jax.ShapeDtypeStruct((n_tok, PACKED_COLS), jnp.uint32),
                   jax.ShapeDtypeStruct((n_tok, LANES), jnp.int32),
                   jax.ShapeDtypeStruct((n_tok, LANES), F32)],
        scratch_shapes=[pltpu.VMEM((2, tm, D_MODEL), F32)],
        compiler_params=pltpu.CompilerParams(dimension_semantics=("arbitrary",),
                                             vmem_limit_bytes=VMEM_LIMIT_BYTES),
        name="mixer_back",
    )(x2d, za, sb, g_mix, w_merge, b_merge, w_a, w_b, w_out, g_ffn, w_router, b_router)


def _row_gather_copy(src_hbm, src_row, dst_vmem, dst_row, sem):
    return pltpu.make_async_copy(src_hbm.at[pl.ds(src_row, 1), :], dst_vmem.at[pl.ds(dst_row, 1), :], sem)


def _expert_kernel(be_ref, nexte_ref, nused_ref, cnt_ref, base_ref, tok_ref, h2p_hbm, wg_hbm, wu_hbm, wd_hbm,
                   ys_ref, xbuf, sems, wg_st, wu_st, wd_st, wsems, wg_bf, wu_bf, wd_bf):
    rows = MOE_BLOCK_ROWS
    b = pl.program_id(0)
    n_blocks = pl.num_programs(0)
    n_used = nused_ref[0]
    slot = lax.rem(b, GATHER_SLOTS)
    groups = rows // GATHER_GROUP

    def start_rows(blk, cnt, slt):
        base = base_ref[blk]
        for g in range(groups):
            @pl.when(g * GATHER_GROUP < cnt)
            def _():
                for r in range(g * GATHER_GROUP, (g + 1) * GATHER_GROUP):
                    _row_gather_copy(h2p_hbm, tok_ref[base + r], xbuf.at[slt], r, sems.at[slt]).start()

    def wait_rows(cnt, slt):
        for g in range(groups):
            @pl.when(g * GATHER_GROUP < cnt)
            def _():
                part = pl.ds(g * GATHER_GROUP, GATHER_GROUP)
                pltpu.make_async_copy(h2p_hbm.at[part, :], xbuf.at[slt, part, :], sems.at[slt]).wait()

    def weight_copies(e):
        return [pltpu.make_async_copy(src.at[e], dst, wsems.at[j])
                for j, (src, dst) in enumerate(((wg_hbm, wg_st), (wu_hbm, wu_st), (wd_hbm, wd_st)))]

    @pl.when(b == 0)
    def _():
        xbuf[...] = jnp.zeros_like(xbuf)
        for ahead in range(GATHER_AHEAD):
            start_rows(ahead, cnt_ref[ahead], ahead)

    @pl.when(b < n_used)
    def _():
        @pl.when(b == 0)
        def _():
            for cp in weight_copies(be_ref[0]):
                cp.start()

        @pl.when(jnp.logical_or(b == 0, be_ref[b] != be_ref[jnp.maximum(b - 1, 0)]))
        def _():
            for cp in weight_copies(be_ref[b]):
                cp.wait()
            wg_bf[...] = wg_st[...].astype(BF16)
            wu_bf[...] = wu_st[...].astype(BF16)
            wd_bf[...] = wd_st[...].astype(BF16)
            nxt_e = nexte_ref[b]

            @pl.when(nxt_e >= 0)
            def _():
                for cp in weight_copies(nxt_e):
                    cp.start(priority=WEIGHT_DMA_PRIORITY)

        ahead_blk = jnp.minimum(b + GATHER_AHEAD, n_blocks - 1)
        ahead_cnt = jnp.where(b + GATHER_AHEAD < n_blocks, cnt_ref[ahead_blk], 0)
        start_rows(ahead_blk, ahead_cnt, lax.rem(b + GATHER_AHEAD, GATHER_SLOTS))
        wait_rows(cnt_ref[b], slot)

        h2 = jnp.concatenate([half.astype(BF16) for half in _unpack_bf16_halves(xbuf[slot])], axis=1)
        gate = jnp.dot(h2, wg_bf[...], preferred_element_type=F32)
        up = jnp.dot(h2, wu_bf[...], preferred_element_type=F32)
        hid = (jax.nn.silu(gate) * up).astype(BF16)
        y = jnp.dot(hid, wd_bf[...], preferred_element_type=F32)
        ys_ref[...] = _pack_bf16_halves(y)

    @pl.when(b >= n_used)
    def _():
        ys_ref[...] = jnp.zeros_like(ys_ref)


def _expert_ffn(block_e, next_e, n_used, block_cnt, block_base, sorted_tok, h2p, w_g, w_u, w_d):
    n_blocks = block_e.shape[0]
    n_rows = n_blocks * MOE_BLOCK_ROWS
    grid_spec = pltpu.PrefetchScalarGridSpec(
        num_scalar_prefetch=6,
        grid=(n_blocks,),
        in_specs=[
            pl.BlockSpec(memory_space=pl.ANY),
            pl.BlockSpec(memory_space=pl.ANY),
            pl.BlockSpec(memory_space=pl.ANY),
            pl.BlockSpec(memory_space=pl.ANY),
        ],
        out_specs=pl.BlockSpec((MOE_BLOCK_ROWS, PACKED_COLS), lambda b, *_: (b, 0)),
        scratch_shapes=[
            pltpu.VMEM((GATHER_SLOTS, MOE_BLOCK_ROWS, PACKED_COLS), jnp.uint32),
            pltpu.SemaphoreType.DMA((GATHER_SLOTS,)),
            pltpu.VMEM((D_MODEL, D_EXPERT), F32),
            pltpu.VMEM((D_MODEL, D_EXPERT), F32),
            pltpu.VMEM((D_EXPERT, D_MODEL), F32),
            pltpu.SemaphoreType.DMA((3,)),
            pltpu.VMEM((D_MODEL, D_EXPERT), BF16),
            pltpu.VMEM((D_MODEL, D_EXPERT), BF16),
            pltpu.VMEM((D_EXPERT, D_MODEL), BF16),
        ],
    )
    return pl.pallas_call(
        _expert_kernel,
        grid_spec=grid_spec,
        out_shape=jax.ShapeDtypeStruct((n_rows, PACKED_COLS), jnp.uint32),
        compiler_params=pltpu.CompilerParams(dimension_semantics=("arbitrary",),
                                             vmem_limit_bytes=VMEM_LIMIT_BYTES),
        name="expert_ffn",
    )(block_e, next_e, n_used, block_cnt, block_base, sorted_tok, h2p, w_g, w_u, w_d)


def _tail_kernel(dest_ref, x1_ref, wts_ref, ys_hbm, p_ref, gple_ref, wpg_ref, bpg_ref, wpu_ref, gfin_ref,
                 out_ref, ybuf, sems):
    tm = x1_ref.shape[0]
    i = pl.program_id(0)
    last = pl.num_programs(0) - 1
    slot = lax.rem(i, 2)

    def start_token(step, slt, r):
        for k in range(TOP_K):
            _row_gather_copy(ys_hbm, dest_ref[TOP_K * (step * tm + r) + k], ybuf.at[slt], k * tm + r,
                             sems.at[slt]).start()

    def wait_tile(slt):
        pltpu.make_async_copy(ys_hbm.at[pl.ds(0, TOP_K * tm), :], ybuf.at[slt], sems.at[slt]).wait()

    @pl.when(i == 0)
    def _():
        def body(r, carry):
            start_token(0, 0, r)
            return carry
        lax.fori_loop(0, tm, body, 0, unroll=GATHER_UNROLL)

    wait_tile(slot)
    nxt = jnp.minimum(i + 1, last)
    other = 1 - slot
    wts = wts_ref[...]
    x2 = x1_ref[...]
    for k in range(TOP_K):
        y = jnp.concatenate(_unpack_bf16_halves(ybuf[slot, k * tm:(k + 1) * tm, :]), axis=1)
        x2 = x2 + wts[:, k:k + 1] * y
    hn = (_rms_scale(x2) * gple_ref[...]).astype(BF16)
    up = jnp.dot(p_ref[...].astype(BF16), wpu_ref[...], preferred_element_type=F32)
    n_slabs = D_MODEL // MXU_COLS
    per = tm // n_slabs
    x3 = []
    for n in range(n_slabs):
        cols = slice(n * MXU_COLS, (n + 1) * MXU_COLS)
        for r in range(n * per, (n + 1) * per):
            start_token(nxt, other, r)
        gate = jax.nn.sigmoid(jnp.dot(hn, wpg_ref[:, cols], preferred_element_type=F32) + bpg_ref[:, cols])
        x3.append(x2[:, cols] + gate * up[:, cols])
    x3 = jnp.concatenate(x3, axis=1)
    out_ref[...] = _rms_scale(x3) * gfin_ref[...]

    @pl.when(i == last)
    def _():
        wait_tile(other)


def _tail(dest, x1, wts, ys, p2d, g_ple, w_pg, b_pg, w_pu, g_final):
    n_tok = x1.shape[0]
    tm = TM_TAIL
    row = lambda i, *_: (i, 0)
    const = lambda i, *_: (0, 0)
    grid_spec = pltpu.PrefetchScalarGridSpec(
        num_scalar_prefetch=1,
        grid=(n_tok // tm,),
        in_specs=[
            pl.BlockSpec((tm, D_MODEL), row),
            pl.BlockSpec((tm, LANES), row),
            pl.BlockSpec(memory_space=pl.ANY),
            pl.BlockSpec((tm, PLE_DIM), row),
            pl.BlockSpec((1, D_MODEL), const),
            pl.BlockSpec((D_MODEL, D_MODEL), const, pipeline_mode=pl.Buffered(1)),
            pl.BlockSpec((1, D_MODEL), const),
            pl.BlockSpec((PLE_DIM, D_MODEL), const),
            pl.BlockSpec((1, D_MODEL), const),
        ],
        out_specs=pl.BlockSpec((tm, D_MODEL), row),
        scratch_shapes=[pltpu.VMEM((2, TOP_K * tm, PACKED_COLS), jnp.uint32), pltpu.SemaphoreType.DMA((2,))],
    )
    return pl.pallas_call(
        _tail_kernel,
        grid_spec=grid_spec,
        out_shape=jax.ShapeDtypeStruct((n_tok, D_MODEL), F32),
        compiler_params=pltpu.CompilerParams(dimension_semantics=("arbitrary",),
                                             vmem_limit_bytes=VMEM_LIMIT_BYTES),
        name="tail",
    )(dest, x1, wts, ys, p2d, g_ple, w_pg, b_pg, w_pu, g_final)


def _dispatch_plan(expert_id):
    n_tok = expert_id.shape[0]
    n_assign = n_tok * TOP_K
    n_blocks = -(-n_assign // MOE_BLOCK_ROWS) + N_EXPERTS
    n_rows = n_blocks * MOE_BLOCK_ROWS
    i32 = jnp.int32
    flat_e = expert_id.reshape(-1)
    experts = jnp.arange(N_EXPERTS, dtype=i32)
    assign = jnp.arange(n_assign, dtype=i32)
    se, order = lax.sort((flat_e, assign), num_keys=1)
    onehot_sorted = se[:, None] == experts[None, :]
    counts = jnp.sum(onehot_sorted.astype(i32), axis=0)
    padded = (counts + MOE_BLOCK_ROWS - 1) // MOE_BLOCK_ROWS * MOE_BLOCK_ROWS
    pad_end = jnp.cumsum(padded)
    pad_start = pad_end - padded
    start = jnp.cumsum(counts) - counts
    row_of_sorted = assign + jnp.sum(jnp.where(onehot_sorted, (pad_start - start)[None, :], 0), axis=1)
    _, dest = lax.sort((order, row_of_sorted), num_keys=1)
    n_used = pad_end[-1] // MOE_BLOCK_ROWS
    rows = jnp.arange(n_rows, dtype=i32)
    row_e = jnp.minimum(jnp.sum((pad_end[None, :] <= rows[:, None]).astype(i32), axis=1), N_EXPERTS - 1)
    onehot_row = row_e[:, None] == experts[None, :]
    pick = lambda table: jnp.sum(jnp.where(onehot_row, table[None, :], 0), axis=1)
    offset = rows - pick(pad_start)
    valid = offset < pick(counts)
    sorted_tok = jnp.concatenate([order // TOP_K, jnp.zeros((MOE_BLOCK_ROWS,), i32)])
    block_base = jnp.clip((pick(start) + offset).reshape(n_blocks, MOE_BLOCK_ROWS)[:, 0], 0, n_assign)
    block_e = row_e.reshape(n_blocks, MOE_BLOCK_ROWS)[:, 0]
    block_cnt = jnp.sum(valid.reshape(n_blocks, MOE_BLOCK_ROWS).astype(i32), axis=1)
    block_e = jnp.where(jnp.arange(n_blocks) < n_used, block_e, block_e[jnp.maximum(n_used - 1, 0)])
    later_used = jnp.logical_and(experts[None, :] > experts[:, None], (counts > 0)[None, :])
    next_used = jnp.min(jnp.where(later_used, experts[None, :], N_EXPERTS), axis=1)
    next_used = jnp.where(next_used < N_EXPERTS, next_used, -1)
    next_e = jnp.sum(jnp.where(block_e[:, None] == experts[None, :], next_used[None, :], 0), axis=1)
    return (block_e.astype(i32), next_e.astype(i32), n_used.astype(i32).reshape(1), block_cnt,
            block_base.astype(i32), sorted_tok.astype(i32), dest.astype(i32))


def kernel(x, p, g_mix, w_in, w_pool, pool_scale, w_branch_a, sgu_ln_g, sgu_ln_b, w_spatial, b_spatial, w_branch_b, w_merge_gate, b_merge_gate, w_out, g_ffn, w_router_group, b_router_group, w_router_expert, b_router_expert, w_exp_gate, w_exp_up, w_exp_down, g_ple, w_ple_gate, b_ple_gate, w_ple_up, g_final):
    bsz, seq, d = x.shape
    assert (seq, d) == (SEQ, D_MODEL) and g_mix.shape[0] == 1
    n_tok = bsz * seq
    x2d = x.reshape(n_tok, d)
    row2d = lambda v: v.reshape(1, -1)

    za, sb, w_merge_bf, w_a_bf, w_b_bf, w_out_bf, w_pg_bf = _mixer_front(
        x2d, row2d(g_mix[0]), w_in[0].astype(BF16), w_pool[0].astype(BF16), row2d(pool_scale[0]),
        row2d(sgu_ln_g[0]), row2d(sgu_ln_b[0]), w_spatial[0], b_spatial[0][:, :, None],
        [w_merge_gate[0], w_branch_a[0], w_branch_b[0], w_out[0], w_ple_gate[0]])

    w_router = jnp.concatenate(
        [w_router_group[0], jnp.transpose(w_router_expert[0], (1, 0, 2)).reshape(d, N_EXPERTS)], axis=1)
    b_router = jnp.concatenate([b_router_group[0], b_router_expert[0].reshape(N_EXPERTS)])
    pad = LANES - w_router.shape[1]
    w_router = jnp.pad(w_router, ((0, 0), (0, pad)))
    b_router = jnp.pad(b_router, (0, pad))
    w_router_hi = w_router.astype(BF16)
    w_router_lo = (w_router - w_router_hi.astype(F32)).astype(BF16)
    w_router_split = jnp.concatenate([w_router_hi, w_router_lo], axis=1)

    x1, h2p, eid, wts = _mixer_back(
        x2d, za, sb, row2d(g_mix[0]), w_merge_bf, row2d(b_merge_gate[0]), w_a_bf, w_b_bf, w_out_bf,
        row2d(g_ffn[0]), w_router_split, row2d(b_router))

    block_e, next_e, n_used, block_cnt, block_base, sorted_tok, dest = _dispatch_plan(eid[:, :TOP_K])
    ys = _expert_ffn(block_e, next_e, n_used, block_cnt, block_base, sorted_tok, h2p, w_exp_gate[0],
                     w_exp_up[0], w_exp_down[0])
    out = _tail(dest, x1, wts, ys, p[0].reshape(n_tok, PLE_DIM), row2d(g_ple[0]), w_pg_bf,
                row2d(b_ple_gate[0]), w_ple_up[0].astype(BF16), row2d(g_final))
    return out.reshape(bsz, seq, d)
```

```python
import jax
import jax.numpy as jnp
from jax import lax
from jax.experimental import pallas as pl
from jax.experimental.pallas import tpu as pltpu

F32 = jnp.float32
BF16 = jnp.bfloat16

D_MODEL = 2048
SEQ = 4096
EPS = 1e-6
PLE_DIM = 256
POOL_WINDOWS = (2, 4, 8, 16)
POOL_WIDTH = D_MODEL // 2
POOL_GROUP_DIM = POOL_WIDTH // len(POOL_WINDOWS)
POOL_HISTORY = max(POOL_WINDOWS)
SGU_BLOCK = 128
SGU_CHUNK = 64
SGU_GROUPS = 8
SGU_WIDTH = D_MODEL // 2
SGU_GROUP_DIM = SGU_WIDTH // SGU_GROUPS
N_IN = POOL_WIDTH + 2 * SGU_WIDTH
N_EXPERT_GROUPS = 4
EXPERTS_PER_GROUP = 8
N_EXPERTS = N_EXPERT_GROUPS * EXPERTS_PER_GROUP
TOP_K = 2
D_EXPERT = D_MODEL // 4
MOE_BLOCK_ROWS = 256

LANES = 128
BF16_SUBLANES = 16
MXU_COLS = 256
ROUTER_EXPERT_LANE0 = N_EXPERT_GROUPS
VMEM_LIMIT_BYTES = 56 * 1024 * 1024

TM_FRONT = 512
TM_BACK = 256
TM_TAIL = 512
MERGE_CHUNK = 1024
WEIGHT_DMA_PRIORITY = 1
GATHER_GROUP = 32
GATHER_AHEAD = 2
GATHER_SLOTS = GATHER_AHEAD + 1
PACKED_COLS = D_MODEL // 2
GATHER_UNROLL = 8


def _rms_scale(x):
    return x * lax.rsqrt(jnp.mean(x * x, axis=-1, keepdims=True) + EPS)


def _pack_bf16_halves(v):
    bits = lambda part: lax.bitcast_convert_type(part.astype(BF16).astype(F32), jnp.uint32)
    return (bits(v[:, :PACKED_COLS]) >> 16) | bits(v[:, PACKED_COLS:])


def _unpack_bf16_halves(words):
    return (lax.bitcast_convert_type(words << 16, F32),
            lax.bitcast_convert_type(words & jnp.uint32(0xFFFF0000), F32))


def _resident(shape):
    zeros = (0,) * len(shape)
    return pl.BlockSpec(shape, lambda *_: zeros, pipeline_mode=pl.Buffered(1))


def _mixer_front_kernel(x_ref, gmix_ref, win_ref, wpool_ref, pscale_ref, lng_ref, lnb_ref, ws_ref,
                        bsp_ref, *rest):
    n_later = len(rest) // 2 - 1
    later_f32, (za_ref, sb_ref), later_bf16, hist_ref = (
        rest[:n_later], rest[n_later:n_later + 2], rest[n_later + 2:-1], rest[-1])
    tm = x_ref.shape[0]
    tiles_per_seq = SEQ // tm
    seq_tile = lax.rem(pl.program_id(0), tiles_per_seq)

    @pl.when(seq_tile == 0)
    def _():
        hist_ref[...] = jnp.zeros_like(hist_ref)

    for src, dst in zip(later_f32, later_bf16):
        dst[...] = src[...].astype(BF16)

    h = (_rms_scale(x_ref[...]) * gmix_ref[...]).astype(BF16)
    project = lambda lo, width: jnp.dot(h, win_ref[:, lo:lo + width], preferred_element_type=F32)
    v = jax.nn.gelu(project(POOL_WIDTH + SGU_WIDTH, SGU_WIDTH))
    vc = v - jnp.mean(v, axis=-1, keepdims=True)
    var = jnp.mean(vc * vc, axis=-1, keepdims=True)
    vn = (vc * lax.rsqrt(var + EPS) * lng_ref[...] + lnb_ref[...]).astype(BF16)
    u = jax.nn.gelu(project(POOL_WIDTH, SGU_WIDTH))
    a = project(0, POOL_WIDTH)

    ext = jnp.concatenate([hist_ref[...], a], axis=0)
    hist_ref[...] = a[tm - POOL_HISTORY:, :]
    frames = (seq_tile * tm + 1 + lax.broadcasted_iota(jnp.int32, (tm, 1), 0)).astype(F32)
    for gi, w in enumerate(POOL_WINDOWS):
        cols = slice(gi * POOL_GROUP_DIM, (gi + 1) * POOL_GROUP_DIM)
        s = ext[:, cols]
        k = 1
        while k < w:
            s = s + pltpu.roll(s, k, 0)
            k *= 2
        wsum = s[POOL_HISTORY:, :]
        zg = wsum / jnp.minimum(frames, float(w)) - a[:, cols]
        yg = jnp.dot(zg.astype(BF16), wpool_ref[gi], preferred_element_type=F32)
        za_ref[:, cols] = (yg * pscale_ref[:, cols]).astype(BF16)

    t_chunk = lax.broadcasted_iota(jnp.int32, (SGU_BLOCK, SGU_BLOCK), 0) // SGU_CHUNK
    s_chunk = lax.broadcasted_iota(jnp.int32, (SGU_BLOCK, SGU_BLOCK), 1) // SGU_CHUNK
    causal = s_chunk <= t_chunk
    nblk = tm // SGU_BLOCK
    for g in range(SGU_GROUPS):
        cols = slice(g * SGU_GROUP_DIM, (g + 1) * SGU_GROUP_DIM)
        wsg = jnp.where(causal, ws_ref[g], 0.0).astype(BF16)
        vg = jnp.concatenate([vn[j * SGU_BLOCK:(j + 1) * SGU_BLOCK, cols] for j in range(nblk)], axis=1)
        vm = jnp.dot(wsg, vg, preferred_element_type=F32) + bsp_ref[g]
        for j in range(nblk):
            rows = slice(j * SGU_BLOCK, (j + 1) * SGU_BLOCK)
            sb_ref[rows, cols] = (u[rows, cols] * vm[:, j * SGU_GROUP_DIM:(j + 1) * SGU_GROUP_DIM]).astype(BF16)


def _mixer_front(x2d, g_mix, w_in, w_pool, pool_scale, ln_g, ln_b, w_spatial, b_spatial, later_weights):
    n_tok = x2d.shape[0]
    tm = TM_FRONT
    n_steps = n_tok // tm
    row = lambda i: (i, 0)
    later_specs = [pl.BlockSpec((w.shape[0] // n_steps, w.shape[1]), row) for w in later_weights]
    assert all(w.shape[0] % (n_steps * BF16_SUBLANES) == 0 for w in later_weights)
    return pl.pallas_call(
        _mixer_front_kernel,
        grid=(n_steps,),
        in_specs=[
            pl.BlockSpec((tm, D_MODEL), row),
            _resident((1, D_MODEL)),
            _resident((D_MODEL, N_IN)),
            _resident((len(POOL_WINDOWS), POOL_GROUP_DIM, POOL_GROUP_DIM)),
            _resident((1, POOL_WIDTH)),
            _resident((1, SGU_WIDTH)),
            _resident((1, SGU_WIDTH)),
            _resident((SGU_GROUPS, SGU_BLOCK, SGU_BLOCK)),
            _resident((SGU_GROUPS, SGU_BLOCK, 1)),
        ] + later_specs,
        out_specs=[pl.BlockSpec((tm, POOL_WIDTH), row), pl.BlockSpec((tm, SGU_WIDTH), row)] + later_specs,
        out_shape=[jax.ShapeDtypeStruct((n_tok, POOL_WIDTH), BF16),
                   jax.ShapeDtypeStruct((n_tok, SGU_WIDTH), BF16)]
                  + [jax.ShapeDtypeStruct(w.shape, BF16) for w in later_weights],
        scratch_shapes=[pltpu.VMEM((POOL_HISTORY, POOL_WIDTH), F32)],
        compiler_params=pltpu.CompilerParams(dimension_semantics=("arbitrary",),
                                             vmem_limit_bytes=VMEM_LIMIT_BYTES),
        name="mixer_front",
    )(x2d, g_mix, w_in, w_pool, pool_scale, ln_g, ln_b, w_spatial, b_spatial, *later_weights)


def _route(logits):
    lane = lax.broadcasted_iota(jnp.int32, logits.shape, 1).astype(F32)
    neg = -jnp.inf
    far = float(LANES)

    def first_argmax(vals):
        top = jnp.max(vals, axis=-1, keepdims=True)
        return top, jnp.min(jnp.where(vals == top, lane, far), axis=-1, keepdims=True)

    is_grp = lane < float(N_EXPERT_GROUPS)
    g_top, g_idx = first_argmax(jnp.where(is_grp, logits, neg))
    g_den = jnp.sum(jnp.where(is_grp, jnp.exp(logits - g_top), 0.0), axis=-1, keepdims=True)
    grp_p = 1.0 / g_den
    lo = float(ROUTER_EXPERT_LANE0) + g_idx * float(EXPERTS_PER_GROUP)
    e_log = jnp.where(lane >= lo, jnp.where(lane < lo + float(EXPERTS_PER_GROUP), logits, neg), neg)
    t1, i1 = first_argmax(e_log)
    t2, i2 = first_argmax(jnp.where(lane == i1, neg, e_log))
    r = jnp.exp(t2 - t1)
    w1 = grp_p / (1.0 + r)
    w2 = grp_p * r / (1.0 + r)
    e1 = i1 - float(ROUTER_EXPERT_LANE0)
    e2 = i2 - float(ROUTER_EXPERT_LANE0)
    eid = jnp.where(lane == 0.0, e1, jnp.where(lane == 1.0, e2, 0.0)).astype(jnp.int32)
    wts = jnp.where(lane == 0.0, w1, jnp.where(lane == 1.0, w2, 0.0))
    return eid, wts


def _mixer_back_kernel(x_ref, za_ref, sb_ref, gmix_ref, wm_ref, bm_ref, wa_ref, wb_ref, wo_ref,
                       gffn_ref, wr_ref, br_ref, x1_ref, h2p_ref, eid_ref, wts_ref, carry_ref):
    i = pl.program_id(0)

    @pl.when(i == 0)
    def _():
        carry_ref[1] = jnp.zeros(carry_ref.shape[1:], F32)

    x1_prev = carry_ref[lax.rem(i + 1, 2)]
    h2 = _rms_scale(x1_prev) * gffn_ref[...]
    h2_hi = h2.astype(BF16)
    h2_lo = (h2 - h2_hi.astype(F32)).astype(BF16)
    h2p_ref[...] = _pack_bf16_halves(h2)
    hi_terms = jnp.dot(h2_hi, wr_ref[...], preferred_element_type=F32)
    lo_term = jnp.dot(h2_lo, wr_ref[:, :LANES], preferred_element_type=F32)
    logits = hi_terms[:, :LANES] + hi_terms[:, LANES:] + lo_term + br_ref[...]
    eid, wts = _route(logits)
    eid_ref[...] = eid
    wts_ref[...] = wts

    x = x_ref[...]
    h = (_rms_scale(x) * gmix_ref[...]).astype(BF16)
    za = za_ref[...]
    sb = sb_ref[...]
    acc = jnp.zeros(x.shape, F32)
    for c in range(D_MODEL // MERGE_CHUNK):
        ca = slice(c * MERGE_CHUNK, (c + 1) * MERGE_CHUNK)
        cb = slice(D_MODEL + c * MERGE_CHUNK, D_MODEL + (c + 1) * MERGE_CHUNK)
        ga = jax.nn.sigmoid(jnp.dot(h, wm_ref[:, ca], preferred_element_type=F32) + bm_ref[:, ca])
        gb = jax.nn.sigmoid(jnp.dot(h, wm_ref[:, cb], preferred_element_type=F32) + bm_ref[:, cb])
        ya = jnp.dot(za, wa_ref[:, ca], preferred_element_type=F32)
        yb = jnp.dot(sb, wb_ref[:, ca], preferred_element_type=F32)
        merged = (ga * ya + gb * yb).astype(BF16)
        acc = acc + jnp.dot(merged, wo_ref[ca, :], preferred_element_type=F32)
    x1 = x + acc
    x1_ref[...] = x1
    carry_ref[lax.rem(i, 2)] = x1


def _mixer_back(x2d, za, sb, g_mix, w_merge, b_merge, w_a, w_b, w_out, g_ffn, w_router, b_router):
    n_tok = x2d.shape[0]
    tm = TM_BACK
    n_tiles = n_tok // tm
    row = lambda i: (jnp.minimum(i, n_tiles - 1), 0)
    prev = lambda i: (jnp.maximum(i - 1, 0), 0)
    return pl.pallas_call(
        _mixer_back_kernel,
        grid=(n_tiles + 1,),
        in_specs=[
            pl.BlockSpec((tm, D_MODEL), row),
            pl.BlockSpec((tm, POOL_WIDTH), row),
            pl.BlockSpec((tm, SGU_WIDTH), row),
            _resident((1, D_MODEL)),
            _resident((D_MODEL, 2 * D_MODEL)),
            _resident((1, 2 * D_MODEL)),
            _resident((POOL_WIDTH, D_MODEL)),
            _resident((SGU_WIDTH, D_MODEL)),
            _resident((D_MODEL, D_MODEL)),
            _resident((1, D_MODEL)),
            _resident((D_MODEL, 2 * LANES)),
            _resident((1, LANES)),
        ],
        out_specs=[pl.BlockSpec((tm, D_MODEL), row), pl.BlockSpec((tm, PACKED_COLS), prev),
                   pl.BlockSpec((tm, LANES), prev), pl.BlockSpec((tm, LANES), prev)],
        out_shape=[jax.ShapeDtypeStruct((n_tok, D_MODEL), F32),
                   jax.ShapeDtypeStruct((n_tok, PACKED_COLS), jnp.uint32),
                   jax.ShapeDtypeStruct((n_tok, LANES), jnp.int32),
                   jax.ShapeDtypeStruct((n_tok, LANES), F32)],
        scratch_shapes=[pltpu.VMEM((2, tm, D_MODEL), F32)],
        compiler_params=pltpu.CompilerParams(dimension_semantics=("arbitrary",),
                                             vmem_limit_bytes=VMEM_LIMIT_BYTES),
        name="mixer_back",
    )(x2d, za, sb, g_mix, w_merge, b_merge, w_a, w_b, w_out, g_ffn, w_router, b_router)


def _row_gather_copy(src_hbm, src_row, dst_vmem, dst_row, sem):
    return pltpu.make_async_copy(src_hbm.at[pl.ds(src_row, 1), :], dst_vmem.at[pl.ds(dst_row, 1), :], sem)


def _expert_kernel(be_ref, nexte_ref, nused_ref, cnt_ref, base_ref, tok_ref, h2p_hbm, wg_hbm, wu_hbm, wd_hbm,
                   ys_ref, xbuf, sems, wg_st, wu_st, wd_st, wsems, wg_bf, wu_bf, wd_bf):
    rows = MOE_BLOCK_ROWS
    b = pl.program_id(0)
    n_blocks = pl.num_programs(0)
    n_used = nused_ref[0]
    slot = lax.rem(b, GATHER_SLOTS)
    groups = rows // GATHER_GROUP

    def start_rows(blk, cnt, slt):
        base = base_ref[blk]
        for g in range(groups):
            @pl.when(g * GATHER_GROUP < cnt)
            def _():
                for r in range(g * GATHER_GROUP, (g + 1) * GATHER_GROUP):
                    _row_gather_copy(h2p_hbm, tok_ref[base + r], xbuf.at[slt], r, sems.at[slt]).start()

    def wait_rows(cnt, slt):
        for g in range(groups):
            @pl.when(g * GATHER_GROUP < cnt)
            def _():
                part = pl.ds(g * GATHER_GROUP, GATHER_GROUP)
                pltpu.make_async_copy(h2p_hbm.at[part, :], xbuf.at[slt, part, :], sems.at[slt]).wait()

    def weight_copies(e):
        return [pltpu.make_async_copy(src.at[e], dst, wsems.at[j])
                for j, (src, dst) in enumerate(((wg_hbm, wg_st), (wu_hbm, wu_st), (wd_hbm, wd_st)))]

    @pl.when(b == 0)
    def _():
        xbuf[...] = jnp.zeros_like(xbuf)
        for ahead in range(GATHER_AHEAD):
            start_rows(ahead, cnt_ref[ahead], ahead)

    @pl.when(b < n_used)
    def _():
        @pl.when(b == 0)
        def _():
            for cp in weight_copies(be_ref[0]):
                cp.start()

        @pl.when(jnp.logical_or(b == 0, be_ref[b] != be_ref[jnp.maximum(b - 1, 0)]))
        def _():
            for cp in weight_copies(be_ref[b]):
                cp.wait()
            wg_bf[...] = wg_st[...].astype(BF16)
            wu_bf[...] = wu_st[...].astype(BF16)
            wd_bf[...] = wd_st[...].astype(BF16)
            nxt_e = nexte_ref[b]

            @pl.when(nxt_e >= 0)
            def _():
                for cp in weight_copies(nxt_e):
                    cp.start(priority=WEIGHT_DMA_PRIORITY)

        ahead_blk = jnp.minimum(b + GATHER_AHEAD, n_blocks - 1)
        ahead_cnt = jnp.where(b + GATHER_AHEAD < n_blocks, cnt_ref[ahead_blk], 0)
        ahead_slot = lax.rem(b + GATHER_AHEAD, GATHER_SLOTS)
        ahead_full = ahead_cnt == rows
        wait_rows(cnt_ref[b], slot)

        def block_ffn(start_ahead_rows):
            h2 = jnp.concatenate([half.astype(BF16) for half in _unpack_bf16_halves(xbuf[slot])], axis=1)
            start_ahead_rows()
            gate = jnp.dot(h2, wg_bf[...], preferred_element_type=F32)
            up = jnp.dot(h2, wu_bf[...], preferred_element_type=F32)
            hid = (jax.nn.silu(gate) * up).astype(BF16)
            y = jnp.dot(hid, wd_bf[...], preferred_element_type=F32)
            ys_ref[...] = _pack_bf16_halves(y)

        @pl.when(ahead_full)
        def _():
            def start_all():
                base = base_ref[ahead_blk]
                for r in range(rows):
                    _row_gather_copy(h2p_hbm, tok_ref[base + r], xbuf.at[ahead_slot], r, sems.at[ahead_slot]).start()
            block_ffn(start_all)

        @pl.when(jnp.logical_not(ahead_full))
        def _():
            start_rows(ahead_blk, ahead_cnt, ahead_slot)
            block_ffn(lambda: None)

    @pl.when(b >= n_used)
    def _():
        ys_ref[...] = jnp.zeros_like(ys_ref)


def _expert_ffn(block_e, next_e, n_used, block_cnt, block_base, sorted_tok, h2p, w_g, w_u, w_d):
    n_blocks = block_e.shape[0]
    n_rows = n_blocks * MOE_BLOCK_ROWS
    grid_spec = pltpu.PrefetchScalarGridSpec(
        num_scalar_prefetch=6,
        grid=(n_blocks,),
        in_specs=[
            pl.BlockSpec(memory_space=pl.ANY),
            pl.BlockSpec(memory_space=pl.ANY),
            pl.BlockSpec(memory_space=pl.ANY),
            pl.BlockSpec(memory_space=pl.ANY),
        ],
        out_specs=pl.BlockSpec((MOE_BLOCK_ROWS, PACKED_COLS), lambda b, *_: (b, 0)),
        scratch_shapes=[
            pltpu.VMEM((GATHER_SLOTS, MOE_BLOCK_ROWS, PACKED_COLS), jnp.uint32),
            pltpu.SemaphoreType.DMA((GATHER_SLOTS,)),
            pltpu.VMEM((D_MODEL, D_EXPERT), F32),
            pltpu.VMEM((D_MODEL, D_EXPERT), F32),
            pltpu.VMEM((D_EXPERT, D_MODEL), F32),
            pltpu.SemaphoreType.DMA((3,)),
            pltpu.VMEM((D_MODEL, D_EXPERT), BF16),
            pltpu.VMEM((D_MODEL, D_EXPERT), BF16),
            pltpu.VMEM((D_EXPERT, D_MODEL), BF16),
        ],
    )
    return pl.pallas_call(
        _expert_kernel,
        grid_spec=grid_spec,
        out_shape=jax.ShapeDtypeStruct((n_rows, PACKED_COLS), jnp.uint32),
        compiler_params=pltpu.CompilerParams(dimension_semantics=("arbitrary",),
                                             vmem_limit_bytes=VMEM_LIMIT_BYTES),
        name="expert_ffn",
    )(block_e, next_e, n_used, block_cnt, block_base, sorted_tok, h2p, w_g, w_u, w_d)


def _tail_kernel(dest_ref, x1_ref, wts_ref, ys_hbm, p_ref, gple_ref, wpg_ref, bpg_ref, wpu_ref, gfin_ref,
                 out_ref, ybuf, sems):
    tm = x1_ref.shape[0]
    i = pl.program_id(0)
    last = pl.num_programs(0) - 1
    slot = lax.rem(i, 2)

    def start_token(step, slt, r):
        for k in range(TOP_K):
            _row_gather_copy(ys_hbm, dest_ref[TOP_K * (step * tm + r) + k], ybuf.at[slt], k * tm + r,
                             sems.at[slt]).start()

    def wait_tile(slt):
        pltpu.make_async_copy(ys_hbm.at[pl.ds(0, TOP_K * tm), :], ybuf.at[slt], sems.at[slt]).wait()

    @pl.when(i == 0)
    def _():
        def body(r, carry):
            start_token(0, 0, r)
            return carry
        lax.fori_loop(0, tm, body, 0, unroll=GATHER_UNROLL)

    wait_tile(slot)
    nxt = jnp.minimum(i + 1, last)
    other = 1 - slot
    wts = wts_ref[...]
    x2 = x1_ref[...]
    for k in range(TOP_K):
        y = jnp.concatenate(_unpack_bf16_halves(ybuf[slot, k * tm:(k + 1) * tm, :]), axis=1)
        x2 = x2 + wts[:, k:k + 1] * y
    hn = (_rms_scale(x2) * gple_ref[...]).astype(BF16)
    up = jnp.dot(p_ref[...].astype(BF16), wpu_ref[...], preferred_element_type=F32)
    n_slabs = D_MODEL // MXU_COLS
    per = tm // n_slabs
    x3 = []
    for n in range(n_slabs):
        cols = slice(n * MXU_COLS, (n + 1) * MXU_COLS)
        for r in range(n * per, (n + 1) * per):
            start_token(nxt, other, r)
        gate = jax.nn.sigmoid(jnp.dot(hn, wpg_ref[:, cols], preferred_element_type=F32) + bpg_ref[:, cols])
        x3.append(x2[:, cols] + gate * up[:, cols])
    x3 = jnp.concatenate(x3, axis=1)
    out_ref[...] = _rms_scale(x3) * gfin_ref[...]

    @pl.when(i == last)
    def _():
        wait_tile(other)


def _tail(dest, x1, wts, ys, p2d, g_ple, w_pg, b_pg, w_pu, g_final):
    n_tok = x1.shape[0]
    tm = TM_TAIL
    row = lambda i, *_: (i, 0)
    const = lambda i, *_: (0, 0)
    grid_spec = pltpu.PrefetchScalarGridSpec(
        num_scalar_prefetch=1,
        grid=(n_tok // tm,),
        in_specs=[
            pl.BlockSpec((tm, D_MODEL), row),
            pl.BlockSpec((tm, LANES), row),
            pl.BlockSpec(memory_space=pl.ANY),
            pl.BlockSpec((tm, PLE_DIM), row),
            pl.BlockSpec((1, D_MODEL), const),
            pl.BlockSpec((D_MODEL, D_MODEL), const, pipeline_mode=pl.Buffered(1)),
            pl.BlockSpec((1, D_MODEL), const),
            pl.BlockSpec((PLE_DIM, D_MODEL), const),
            pl.BlockSpec((1, D_MODEL), const),
        ],
        out_specs=pl.BlockSpec((tm, D_MODEL), row),
        scratch_shapes=[pltpu.VMEM((2, TOP_K * tm, PACKED_COLS), jnp.uint32), pltpu.SemaphoreType.DMA((2,))],
    )
    return pl.pallas_call(
        _tail_kernel,
        grid_spec=grid_spec,
        out_shape=jax.ShapeDtypeStruct((n_tok, D_MODEL), F32),
        compiler_params=pltpu.CompilerParams(dimension_semantics=("arbitrary",),
                                             vmem_limit_bytes=VMEM_LIMIT_BYTES),
        name="tail",
    )(dest, x1, wts, ys, p2d, g_ple, w_pg, b_pg, w_pu, g_final)


def _dispatch_plan(expert_id):
    n_tok = expert_id.shape[0]
    n_assign = n_tok * TOP_K
    n_blocks = -(-n_assign // MOE_BLOCK_ROWS) + N_EXPERTS
    n_rows = n_blocks * MOE_BLOCK_ROWS
    i32 = jnp.int32
    flat_e = expert_id.reshape(-1)
    experts = jnp.arange(N_EXPERTS, dtype=i32)
    assign = jnp.arange(n_assign, dtype=i32)
    se, order = lax.sort((flat_e, assign), num_keys=1)
    onehot_sorted = se[:, None] == experts[None, :]
    counts = jnp.sum(onehot_sorted.astype(i32), axis=0)
    padded = (counts + MOE_BLOCK_ROWS - 1) // MOE_BLOCK_ROWS * MOE_BLOCK_ROWS
    pad_end = jnp.cumsum(padded)
    pad_start = pad_end - padded
    start = jnp.cumsum(counts) - counts
    row_of_sorted = assign + jnp.sum(jnp.where(onehot_sorted, (pad_start - start)[None, :], 0), axis=1)
    _, dest = lax.sort((order, row_of_sorted), num_keys=1)
    n_used = pad_end[-1] // MOE_BLOCK_ROWS
    rows = jnp.arange(n_rows, dtype=i32)
    row_e = jnp.minimum(jnp.sum((pad_end[None, :] <= rows[:, None]).astype(i32), axis=1), N_EXPERTS - 1)
    onehot_row = row_e[:, None] == experts[None, :]
    pick = lambda table: jnp.sum(jnp.where(onehot_row, table[None, :], 0), axis=1)
    offset = rows - pick(pad_start)
    valid = offset < pick(counts)
    sorted_tok = jnp.concatenate([order // TOP_K, jnp.zeros((MOE_BLOCK_ROWS,), i32)])
    block_base = jnp.clip((pick(start) + offset).reshape(n_blocks, MOE_BLOCK_ROWS)[:, 0], 0, n_assign)
    block_e = row_e.reshape(n_blocks, MOE_BLOCK_ROWS)[:, 0]
    block_cnt = jnp.sum(valid.reshape(n_blocks, MOE_BLOCK_ROWS).astype(i32), axis=1)
    block_e = jnp.where(jnp.arange(n_blocks) < n_used, block_e, block_e[jnp.maximum(n_used - 1, 0)])
    later_used = jnp.logical_and(experts[None, :] > experts[:, None], (counts > 0)[None, :])
    next_used = jnp.min(jnp.where(later_used, experts[None, :], N_EXPERTS), axis=1)
    next_used = jnp.where(next_used < N_EXPERTS, next_used, -1)
    next_e = jnp.sum(jnp.where(block_e[:, None] == experts[None, :], next_used[None, :], 0), axis=1)
    return (block_e.astype(i32), next_e.astype(i32), n_used.astype(i32).reshape(1), block_cnt,
            block_base.astype(i32), sorted_tok.astype(i32), dest.astype(i32))


def kernel(x, p, g_mix, w_in, w_pool, pool_scale, w_branch_a, sgu_ln_g, sgu_ln_b, w_spatial, b_spatial, w_branch_b, w_merge_gate, b_merge_gate, w_out, g_ffn, w_router_group, b_router_group, w_router_expert, b_router_expert, w_exp_gate, w_exp_up, w_exp_down, g_ple, w_ple_gate, b_ple_gate, w_ple_up, g_final):
    bsz, seq, d = x.shape
    assert (seq, d) == (SEQ, D_MODEL) and g_mix.shape[0] == 1
    n_tok = bsz * seq
    x2d = x.reshape(n_tok, d)
    row2d = lambda v: v.reshape(1, -1)

    za, sb, w_merge_bf, w_a_bf, w_b_bf, w_out_bf, w_pg_bf = _mixer_front(
        x2d, row2d(g_mix[0]), w_in[0].astype(BF16), w_pool[0].astype(BF16), row2d(pool_scale[0]),
        row2d(sgu_ln_g[0]), row2d(sgu_ln_b[0]), w_spatial[0], b_spatial[0][:, :, None],
        [w_merge_gate[0], w_branch_a[0], w_branch_b[0], w_out[0], w_ple_gate[0]])

    w_router = jnp.concatenate(
        [w_router_group[0], jnp.transpose(w_router_expert[0], (1, 0, 2)).reshape(d, N_EXPERTS)], axis=1)
    b_router = jnp.concatenate([b_router_group[0], b_router_expert[0].reshape(N_EXPERTS)])
    pad = LANES - w_router.shape[1]
    w_router = jnp.pad(w_router, ((0, 0), (0, pad)))
    b_router = jnp.pad(b_router, (0, pad))
    w_router_hi = w_router.astype(BF16)
    w_router_lo = (w_router - w_router_hi.astype(F32)).astype(BF16)
    w_router_split = jnp.concatenate([w_router_hi, w_router_lo], axis=1)

    x1, h2p, eid, wts = _mixer_back(
        x2d, za, sb, row2d(g_mix[0]), w_merge_bf, row2d(b_merge_gate[0]), w_a_bf, w_b_bf, w_out_bf,
        row2d(g_ffn[0]), w_router_split, row2d(b_router))

    block_e, next_e, n_used, block_cnt, block_base, sorted_tok, dest = _dispatch_plan(eid[:, :TOP_K])
    ys = _expert_ffn(block_e, next_e, n_used, block_cnt, block_base, sorted_tok, h2p, w_exp_gate[0],
                     w_exp_up[0], w_exp_down[0])
    out = _tail(dest, x1, wts, ys, p[0].reshape(n_tok, PLE_DIM), row2d(g_ple[0]), w_pg_bf,
                row2d(b_ple_gate[0]), w_ple_up[0].astype(BF16), row2d(g_final))
    return out.reshape(bsz, seq, d)
```

```python
import jax
import jax.numpy as jnp
from jax import lax
from jax.experimental import pallas as pl
from jax.experimental.pallas import tpu as pltpu

F32 = jnp.float32
BF16 = jnp.bfloat16

D_MODEL = 2048
SEQ = 4096
EPS = 1e-6
PLE_DIM = 256
POOL_WINDOWS = (2, 4, 8, 16)
POOL_WIDTH = D_MODEL // 2
POOL_GROUP_DIM = POOL_WIDTH // len(POOL_WINDOWS)
POOL_HISTORY = max(POOL_WINDOWS)
SGU_BLOCK = 128
SGU_CHUNK = 64
SGU_GROUPS = 8
SGU_WIDTH = D_MODEL // 2
SGU_GROUP_DIM = SGU_WIDTH // SGU_GROUPS
N_IN = POOL_WIDTH + 2 * SGU_WIDTH
N_EXPERT_GROUPS = 4
EXPERTS_PER_GROUP = 8
N_EXPERTS = N_EXPERT_GROUPS * EXPERTS_PER_GROUP
TOP_K = 2
D_EXPERT = D_MODEL // 4
MOE_BLOCK_ROWS = 256

LANES = 128
BF16_SUBLANES = 16
MXU_COLS = 256
ROUTER_EXPERT_LANE0 = N_EXPERT_GROUPS
VMEM_LIMIT_BYTES = 56 * 1024 * 1024

TM_FRONT = 512
TM_BACK = 256
TM_TAIL = 512
MERGE_CHUNK = 1024
WEIGHT_DMA_PRIORITY = 1
GATHER_GROUP = 32
GATHER_AHEAD = 2
GATHER_SLOTS = GATHER_AHEAD + 1
PACKED_COLS = D_MODEL // 2
GATHER_UNROLL = 8


def _rms_scale(x):
    return x * lax.rsqrt(jnp.mean(x * x, axis=-1, keepdims=True) + EPS)


def _pack_bf16_halves(v):
    bits = lambda part: lax.bitcast_convert_type(part.astype(BF16).astype(F32), jnp.uint32)
    return (bits(v[:, :PACKED_COLS]) >> 16) | bits(v[:, PACKED_COLS:])


def _unpack_bf16_halves(words):
    return (lax.bitcast_convert_type(words << 16, F32),
            lax.bitcast_convert_type(words & jnp.uint32(0xFFFF0000), F32))


def _resident(shape):
    zeros = (0,) * len(shape)
    return pl.BlockSpec(shape, lambda *_: zeros, pipeline_mode=pl.Buffered(1))


def _mixer_front_kernel(x_ref, gmix_ref, win_ref, wpool_ref, pscale_ref, lng_ref, lnb_ref, ws_ref,
                        bsp_ref, *rest):
    n_later = len(rest) // 2 - 1
    later_f32, (za_ref, sb_ref), later_bf16, hist_ref = (
        rest[:n_later], rest[n_later:n_later + 2], rest[n_later + 2:-1], rest[-1])
    tm = x_ref.shape[0]
    tiles_per_seq = SEQ // tm
    seq_tile = lax.rem(pl.program_id(0), tiles_per_seq)

    @pl.when(seq_tile == 0)
    def _():
        hist_ref[...] = jnp.zeros_like(hist_ref)

    for src, dst in zip(later_f32, later_bf16):
        dst[...] = src[...].astype(BF16)

    h = (_rms_scale(x_ref[...]) * gmix_ref[...]).astype(BF16)
    project = lambda lo, width: jnp.dot(h, win_ref[:, lo:lo + width], preferred_element_type=F32)
    v = jax.nn.gelu(project(POOL_WIDTH + SGU_WIDTH, SGU_WIDTH))
    vc = v - jnp.mean(v, axis=-1, keepdims=True)
    var = jnp.mean(vc * vc, axis=-1, keepdims=True)
    vn = (vc * lax.rsqrt(var + EPS) * lng_ref[...] + lnb_ref[...]).astype(BF16)
    u = jax.nn.gelu(project(POOL_WIDTH, SGU_WIDTH))
    a = project(0, POOL_WIDTH)

    ext = jnp.concatenate([hist_ref[...], a], axis=0)
    hist_ref[...] = a[tm - POOL_HISTORY:, :]
    frames = (seq_tile * tm + 1 + lax.broadcasted_iota(jnp.int32, (tm, 1), 0)).astype(F32)
    for gi, w in enumerate(POOL_WINDOWS):
        cols = slice(gi * POOL_GROUP_DIM, (gi + 1) * POOL_GROUP_DIM)
        s = ext[:, cols]
        k = 1
        while k < w:
            s = s + pltpu.roll(s, k, 0)
            k *= 2
        wsum = s[POOL_HISTORY:, :]
        zg = wsum / jnp.minimum(frames, float(w)) - a[:, cols]
        yg = jnp.dot(zg.astype(BF16), wpool_ref[gi], preferred_element_type=F32)
        za_ref[:, cols] = (yg * pscale_ref[:, cols]).astype(BF16)

    t_chunk = lax.broadcasted_iota(jnp.int32, (SGU_BLOCK, SGU_BLOCK), 0) // SGU_CHUNK
    s_chunk = lax.broadcasted_iota(jnp.int32, (SGU_BLOCK, SGU_BLOCK), 1) // SGU_CHUNK
    causal = s_chunk <= t_chunk
    nblk = tm // SGU_BLOCK
    for g in range(SGU_GROUPS):
        cols = slice(g * SGU_GROUP_DIM, (g + 1) * SGU_GROUP_DIM)
        wsg = jnp.where(causal, ws_ref[g], 0.0).astype(BF16)
        vg = jnp.concatenate([vn[j * SGU_BLOCK:(j + 1) * SGU_BLOCK, cols] for j in range(nblk)], axis=1)
        vm = jnp.dot(wsg, vg, preferred_element_type=F32) + bsp_ref[g]
        for j in range(nblk):
            rows = slice(j * SGU_BLOCK, (j + 1) * SGU_BLOCK)
            sb_ref[rows, cols] = (u[rows, cols] * vm[:, j * SGU_GROUP_DIM:(j + 1) * SGU_GROUP_DIM]).astype(BF16)


def _mixer_front(x2d, g_mix, w_in, w_pool, pool_scale, ln_g, ln_b, w_spatial, b_spatial, later_weights):
    n_tok = x2d.shape[0]
    tm = TM_FRONT
    n_steps = n_tok // tm
    row = lambda i: (i, 0)
    later_specs = [pl.BlockSpec((w.shape[0] // n_steps, w.shape[1]), row) for w in later_weights]
    assert all(w.shape[0] % (n_steps * BF16_SUBLANES) == 0 for w in later_weights)
    return pl.pallas_call(
        _mixer_front_kernel,
        grid=(n_steps,),
        in_specs=[
            pl.BlockSpec((tm, D_MODEL), row),
            _resident((1, D_MODEL)),
            _resident((D_MODEL, N_IN)),
            _resident((len(POOL_WINDOWS), POOL_GROUP_DIM, POOL_GROUP_DIM)),
            _resident((1, POOL_WIDTH)),
            _resident((1, SGU_WIDTH)),
            _resident((1, SGU_WIDTH)),
            _resident((SGU_GROUPS, SGU_BLOCK, SGU_BLOCK)),
            _resident((SGU_GROUPS, SGU_BLOCK, 1)),
        ] + later_specs,
        out_specs=[pl.BlockSpec((tm, POOL_WIDTH), row), pl.BlockSpec((tm, SGU_WIDTH), row)] + later_specs,
        out_shape=[jax.ShapeDtypeStruct((n_tok, POOL_WIDTH), BF16),
                   jax.ShapeDtypeStruct((n_tok, SGU_WIDTH), BF16)]
                  + [jax.ShapeDtypeStruct(w.shape, BF16) for w in later_weights],
        scratch_shapes=[pltpu.VMEM((POOL_HISTORY, POOL_WIDTH), F32)],
        compiler_params=pltpu.CompilerParams(dimension_semantics=("arbitrary",),
                                             vmem_limit_bytes=VMEM_LIMIT_BYTES),
        name="mixer_front",
    )(x2d, g_mix, w_in, w_pool, pool_scale, ln_g, ln_b, w_spatial, b_spatial, *later_weights)


def _route(logits):
    lane = lax.broadcasted_iota(jnp.int32, logits.shape, 1).astype(F32)
    neg = -jnp.inf
    far = float(LANES)

    def first_argmax(vals):
        top = jnp.max(vals, axis=-1, keepdims=True)
        return top, jnp.min(jnp.where(vals == top, lane, far), axis=-1, keepdims=True)

    is_grp = lane < float(N_EXPERT_GROUPS)
    g_top, g_idx = first_argmax(jnp.where(is_grp, logits, neg))
    g_den = jnp.sum(jnp.where(is_grp, jnp.exp(logits - g_top), 0.0), axis=-1, keepdims=True)
    grp_p = 1.0 / g_den
    lo = float(ROUTER_EXPERT_LANE0) + g_idx * float(EXPERTS_PER_GROUP)
    e_log = jnp.where(lane >= lo, jnp.where(lane < lo + float(EXPERTS_PER_GROUP), logits, neg), neg)
    t1, i1 = first_argmax(e_log)
    t2, i2 = first_argmax(jnp.where(lane == i1, neg, e_log))
    r = jnp.exp(t2 - t1)
    w1 = grp_p / (1.0 + r)
    w2 = grp_p * r / (1.0 + r)
    e1 = i1 - float(ROUTER_EXPERT_LANE0)
    e2 = i2 - float(ROUTER_EXPERT_LANE0)
    eid = jnp.where(lane == 0.0, e1, jnp.where(lane == 1.0, e2, 0.0)).astype(jnp.int32)
    wts = jnp.where(lane == 0.0, w1, jnp.where(lane == 1.0, w2, 0.0))
    return eid, wts


def _mixer_back_kernel(x_ref, za_ref, sb_ref, gmix_ref, wm_ref, bm_ref, wa_ref, wb_ref, wo_ref,
                       gffn_ref, wr_ref, br_ref, x1_ref, h2p_ref, eid_ref, wts_ref, carry_ref):
    i = pl.program_id(0)

    @pl.when(i == 0)
    def _():
        carry_ref[1] = jnp.zeros(carry_ref.shape[1:], F32)

    x1_prev = carry_ref[lax.rem(i + 1, 2)]
    h2 = _rms_scale(x1_prev) * gffn_ref[...]
    h2_hi = h2.astype(BF16)
    h2_lo = (h2 - h2_hi.astype(F32)).astype(BF16)
    h2p_ref[...] = _pack_bf16_halves(h2)
    hi_terms = jnp.dot(h2_hi, wr_ref[...], preferred_element_type=F32)
    lo_term = jnp.dot(h2_lo, wr_ref[:, :LANES], preferred_element_type=F32)
    logits = hi_terms[:, :LANES] + hi_terms[:, LANES:] + lo_term + br_ref[...]
    eid, wts = _route(logits)
    eid_ref[...] = eid
    wts_ref[...] = wts

    x = x_ref[...]
    h = (_rms_scale(x) * gmix_ref[...]).astype(BF16)
    za = za_ref[...]
    sb = sb_ref[...]
    acc = jnp.zeros(x.shape, F32)
    for c in range(D_MODEL // MERGE_CHUNK):
        ca = slice(c * MERGE_CHUNK, (c + 1) * MERGE_CHUNK)
        cb = slice(D_MODEL + c * MERGE_CHUNK, D_MODEL + (c + 1) * MERGE_CHUNK)
        ga = jax.nn.sigmoid(jnp.dot(h, wm_ref[:, ca], preferred_element_type=F32) + bm_ref[:, ca])
        gb = jax.nn.sigmoid(jnp.dot(h, wm_ref[:, cb], preferred_element_type=F32) + bm_ref[:, cb])
        ya = jnp.dot(za, wa_ref[:, ca], preferred_element_type=F32)
        yb = jnp.dot(sb, wb_ref[:, ca], preferred_element_type=F32)
        merged = (ga * ya + gb * yb).astype(BF16)
        acc = acc + jnp.dot(merged, wo_ref[ca, :], preferred_element_type=F32)
    x1 = x + acc
    x1_ref[...] = x1
    carry_ref[lax.rem(i, 2)] = x1


def _mixer_back(x2d, za, sb, g_mix, w_merge, b_merge, w_a, w_b, w_out, g_ffn, w_router, b_router):
    n_tok = x2d.shape[0]
    tm = TM_BACK
    n_tiles = n_tok // tm
    row = lambda i: (jnp.minimum(i, n_tiles - 1), 0)
    prev = lambda i: (jnp.maximum(i - 1, 0), 0)
    return pl.pallas_call(
        _mixer_back_kernel,
        grid=(n_tiles + 1,),
        in_specs=[
            pl.BlockSpec((tm, D_MODEL), row),
            pl.BlockSpec((tm, POOL_WIDTH), row),
            pl.BlockSpec((tm, SGU_WIDTH), row),
            _resident((1, D_MODEL)),
            _resident((D_MODEL, 2 * D_MODEL)),
            _resident((1, 2 * D_MODEL)),
            _resident((POOL_WIDTH, D_MODEL)),
            _resident((SGU_WIDTH, D_MODEL)),
            _resident((D_MODEL, D_MODEL)),
            _resident((1, D_MODEL)),
            _resident((D_MODEL, 2 * LANES)),
            _resident((1, LANES)),
        ],
        out_specs=[pl.BlockSpec((tm, D_MODEL), row), pl.BlockSpec((tm, PACKED_COLS), prev),
                   pl.BlockSpec((tm, LANES), prev), pl.BlockSpec((tm, LANES), prev)],
        out_shape=[jax.ShapeDtypeStruct((n_tok, D_MODEL), F32),
                   jax.ShapeDtypeStruct((n_tok, PACKED_COLS), jnp.uint32),
                   jax.ShapeDtypeStruct((n_tok, LANES), jnp.int32),
                   jax.ShapeDtypeStruct((n_tok, LANES), F32)],
        scratch_shapes=[pltpu.VMEM((2, tm, D_MODEL), F32)],
        compiler_params=pltpu.CompilerParams(dimension_semantics=("arbitrary",),
                                             vmem_limit_bytes=VMEM_LIMIT_BYTES),
        name="mixer_back",
    )(x2d, za, sb, g_mix, w_merge, b_merge, w_a, w_b, w_out, g_ffn, w_router, b_router)


def _row_gather_copy(src_hbm, src_row, dst_vmem, dst_row, sem):
    return pltpu.make_async_copy(src_hbm.at[pl.ds(src_row, 1), :], dst_vmem.at[pl.ds(dst_row, 1), :], sem)


def _expert_kernel(be_ref, nexte_ref, nused_ref, cnt_ref, base_ref, tok_ref, h2p_hbm, wg_hbm, wu_hbm, wd_hbm,
                   ys_ref, xbuf, sems, wg_st, wu_st, wd_st, wsems, wg_bf, wu_bf, wd_bf):
    rows = MOE_BLOCK_ROWS
    b = pl.program_id(0)
    n_blocks = pl.num_programs(0)
    n_used = nused_ref[0]
    slot = lax.rem(b, GATHER_SLOTS)
    groups = rows // GATHER_GROUP

    def start_rows(blk, cnt, slt):
        base = base_ref[blk]
        for g in range(groups):
            @pl.when(g * GATHER_GROUP < cnt)
            def _():
                for r in range(g * GATHER_GROUP, (g + 1) * GATHER_GROUP):
                    _row_gather_copy(h2p_hbm, tok_ref[base + r], xbuf.at[slt], r, sems.at[slt]).start()

    def wait_rows(cnt, slt):
        for g in range(groups):
            @pl.when(g * GATHER_GROUP < cnt)
            def _():
                part = pl.ds(g * GATHER_GROUP, GATHER_GROUP)
                pltpu.make_async_copy(h2p_hbm.at[part, :], xbuf.at[slt, part, :], sems.at[slt]).wait()

    def weight_copies(e):
        return [pltpu.make_async_copy(src.at[e], dst, wsems.at[j])
                for j, (src, dst) in enumerate(((wg_hbm, wg_st), (wu_hbm, wu_st), (wd_hbm, wd_st)))]

    @pl.when(b == 0)
    def _():
        xbuf[...] = jnp.zeros_like(xbuf)
        for ahead in range(GATHER_AHEAD):
            start_rows(ahead, cnt_ref[ahead], ahead)

    @pl.when(b < n_used)
    def _():
        @pl.when(b == 0)
        def _():
            for cp in weight_copies(be_ref[0]):
                cp.start()

        @pl.when(jnp.logical_or(b == 0, be_ref[b] != be_ref[jnp.maximum(b - 1, 0)]))
        def _():
            for cp in weight_copies(be_ref[b]):
                cp.wait()
            wg_bf[...] = wg_st[...].astype(BF16)
            wu_bf[...] = wu_st[...].astype(BF16)
            wd_bf[...] = wd_st[...].astype(BF16)
            nxt_e = nexte_ref[b]

            @pl.when(nxt_e >= 0)
            def _():
                for cp in weight_copies(nxt_e):
                    cp.start(priority=WEIGHT_DMA_PRIORITY)

        ahead_blk = jnp.minimum(b + GATHER_AHEAD, n_blocks - 1)
        ahead_cnt = jnp.where(b + GATHER_AHEAD < n_blocks, cnt_ref[ahead_blk], 0)
        ahead_slot = lax.rem(b + GATHER_AHEAD, GATHER_SLOTS)
        ahead_full = ahead_cnt == rows
        wait_rows(cnt_ref[b], slot)

        def block_ffn(start_ahead_rows):
            h2 = jnp.concatenate([half.astype(BF16) for half in _unpack_bf16_halves(xbuf[slot])], axis=1)
            start_ahead_rows()
            gate = jnp.dot(h2, wg_bf[...], preferred_element_type=F32)
            up = jnp.dot(h2, wu_bf[...], preferred_element_type=F32)
            hid = (jax.nn.silu(gate) * up).astype(BF16)
            y = jnp.dot(hid, wd_bf[...], preferred_element_type=F32)
            ys_ref[...] = _pack_bf16_halves(y)

        @pl.when(ahead_full)
        def _():
            def start_all():
                base = base_ref[ahead_blk]
                for r in range(rows):
                    _row_gather_copy(h2p_hbm, tok_ref[base + r], xbuf.at[ahead_slot], r, sems.at[ahead_slot]).start()
            block_ffn(start_all)

        @pl.when(jnp.logical_not(ahead_full))
        def _():
            start_rows(ahead_blk, ahead_cnt, ahead_slot)
            block_ffn(lambda: None)

    @pl.when(b >= n_used)
    def _():
        ys_ref[...] = jnp.zeros_like(ys_ref)


def _expert_ffn(block_e, next_e, n_used, block_cnt, block_base, sorted_tok, h2p, w_g, w_u, w_d):
    n_blocks = block_e.shape[0]
    n_rows = n_blocks * MOE_BLOCK_ROWS
    grid_spec = pltpu.PrefetchScalarGridSpec(
        num_scalar_prefetch=6,
        grid=(n_blocks,),
        in_specs=[
            pl.BlockSpec(memory_space=pl.ANY),
            pl.BlockSpec(memory_space=pl.ANY),
            pl.BlockSpec(memory_space=pl.ANY),
            pl.BlockSpec(memory_space=pl.ANY),
        ],
        out_specs=pl.BlockSpec((MOE_BLOCK_ROWS, PACKED_COLS), lambda b, *_: (b, 0)),
        scratch_shapes=[
            pltpu.VMEM((GATHER_SLOTS, MOE_BLOCK_ROWS, PACKED_COLS), jnp.uint32),
            pltpu.SemaphoreType.DMA((GATHER_SLOTS,)),
            pltpu.VMEM((D_MODEL, D_EXPERT), F32),
            pltpu.VMEM((D_MODEL, D_EXPERT), F32),
            pltpu.VMEM((D_EXPERT, D_MODEL), F32),
            pltpu.SemaphoreType.DMA((3,)),
            pltpu.VMEM((D_MODEL, D_EXPERT), BF16),
            pltpu.VMEM((D_MODEL, D_EXPERT), BF16),
            pltpu.VMEM((D_EXPERT, D_MODEL), BF16),
        ],
    )
    return pl.pallas_call(
        _expert_kernel,
        grid_spec=grid_spec,
        out_shape=jax.ShapeDtypeStruct((n_rows, PACKED_COLS), jnp.uint32),
        compiler_params=pltpu.CompilerParams(dimension_semantics=("arbitrary",),
                                             vmem_limit_bytes=VMEM_LIMIT_BYTES),
        name="expert_ffn",
    )(block_e, next_e, n_used, block_cnt, block_base, sorted_tok, h2p, w_g, w_u, w_d)


def _tail_kernel(dest_ref, x1_ref, wts_ref, ys_hbm, p_ref, gple_ref, wpg_ref, bpg_ref, wpu_ref, gfin_ref,
                 out_ref, ybuf, sems):
    tm = x1_ref.shape[0]
    i = pl.program_id(0)
    last = pl.num_programs(0) - 1
    slot = lax.rem(i, 2)

    def start_token(step, slt, r):
        for k in range(TOP_K):
            _row_gather_copy(ys_hbm, dest_ref[TOP_K * (step * tm + r) + k], ybuf.at[slt], k * tm + r,
                             sems.at[slt]).start()

    def wait_tile(slt):
        pltpu.make_async_copy(ys_hbm.at[pl.ds(0, TOP_K * tm), :], ybuf.at[slt], sems.at[slt]).wait()

    @pl.when(i == 0)
    def _():
        def body(r, carry):
            start_token(0, 0, r)
            return carry
        lax.fori_loop(0, tm, body, 0, unroll=GATHER_UNROLL)

    wait_tile(slot)
    nxt = jnp.minimum(i + 1, last)
    other = 1 - slot
    wts = wts_ref[...]
    x2 = x1_ref[...]
    for k in range(TOP_K):
        y = jnp.concatenate(_unpack_bf16_halves(ybuf[slot, k * tm:(k + 1) * tm, :]), axis=1)
        x2 = x2 + wts[:, k:k + 1] * y
    hn = (_rms_scale(x2) * gple_ref[...]).astype(BF16)
    up = jnp.dot(p_ref[...].astype(BF16), wpu_ref[...], preferred_element_type=F32)
    n_slabs = D_MODEL // MXU_COLS
    per = tm // n_slabs
    x3 = []
    for n in range(n_slabs):
        cols = slice(n * MXU_COLS, (n + 1) * MXU_COLS)
        for r in range(n * per, (n + 1) * per):
            start_token(nxt, other, r)
        gate = jax.nn.sigmoid(jnp.dot(hn, wpg_ref[:, cols], preferred_element_type=F32) + bpg_ref[:, cols])
        x3.append(x2[:, cols] + gate * up[:, cols])
    x3 = jnp.concatenate(x3, axis=1)
    out_ref[...] = _rms_scale(x3) * gfin_ref[...]

    @pl.when(i == last)
    def _():
        wait_tile(other)


def _tail(dest, x1, wts, ys, p2d, g_ple, w_pg, b_pg, w_pu, g_final):
    n_tok = x1.shape[0]
    tm = TM_TAIL
    row = lambda i, *_: (i, 0)
    const = lambda i, *_: (0, 0)
    grid_spec = pltpu.PrefetchScalarGridSpec(
        num_scalar_prefetch=1,
        grid=(n_tok // tm,),
        in_specs=[
            pl.BlockSpec((tm, D_MODEL), row),
            pl.BlockSpec((tm, LANES), row),
            pl.BlockSpec(memory_space=pl.ANY),
            pl.BlockSpec((tm, PLE_DIM), row),
            pl.BlockSpec((1, D_MODEL), const),
            pl.BlockSpec((D_MODEL, D_MODEL), const, pipeline_mode=pl.Buffered(1)),
            pl.BlockSpec((1, D_MODEL), const),
            pl.BlockSpec((PLE_DIM, D_MODEL), const),
            pl.BlockSpec((1, D_MODEL), const),
        ],
        out_specs=pl.BlockSpec((tm, D_MODEL), row),
        scratch_shapes=[pltpu.VMEM((2, TOP_K * tm, PACKED_COLS), jnp.uint32), pltpu.SemaphoreType.DMA((2,))],
    )
    return pl.pallas_call(
        _tail_kernel,
        grid_spec=grid_spec,
        out_shape=jax.ShapeDtypeStruct((n_tok, D_MODEL), F32),
        compiler_params=pltpu.CompilerParams(dimension_semantics=("arbitrary",),
                                             vmem_limit_bytes=VMEM_LIMIT_BYTES),
        name="tail",
    )(dest, x1, wts, ys, p2d, g_ple, w_pg, b_pg, w_pu, g_final)


def _dispatch_plan(expert_id):
    n_tok = expert_id.shape[0]
    n_assign = n_tok * TOP_K
    n_blocks = -(-n_assign // MOE_BLOCK_ROWS) + N_EXPERTS
    n_rows = n_blocks * MOE_BLOCK_ROWS
    i32 = jnp.int32
    flat_e = expert_id.reshape(-1)
    experts = jnp.arange(N_EXPERTS, dtype=i32)
    assign = jnp.arange(n_assign, dtype=i32)
    assert N_EXPERTS * n_assign < 2**31 and n_assign * n_rows < 2**31
    packed = jnp.sort(flat_e * n_assign + assign)
    se, order = packed // n_assign, packed % n_assign
    onehot_sorted = se[:, None] == experts[None, :]
    counts = jnp.sum(onehot_sorted.astype(i32), axis=0)
    padded = (counts + MOE_BLOCK_ROWS - 1) // MOE_BLOCK_ROWS * MOE_BLOCK_ROWS
    pad_end = jnp.cumsum(padded)
    pad_start = pad_end - padded
    start = jnp.cumsum(counts) - counts
    row_of_sorted = assign + jnp.sum(jnp.where(onehot_sorted, (pad_start - start)[None, :], 0), axis=1)
    dest = jnp.sort(order * n_rows + row_of_sorted) % n_rows
    n_used = pad_end[-1] // MOE_BLOCK_ROWS
    rows = jnp.arange(n_rows, dtype=i32)
    row_e = jnp.minimum(jnp.sum((pad_end[None, :] <= rows[:, None]).astype(i32), axis=1), N_EXPERTS - 1)
    onehot_row = row_e[:, None] == experts[None, :]
    pick = lambda table: jnp.sum(jnp.where(onehot_row, table[None, :], 0), axis=1)
    offset = rows - pick(pad_start)
    valid = offset < pick(counts)
    sorted_tok = jnp.concatenate([order // TOP_K, jnp.zeros((MOE_BLOCK_ROWS,), i32)])
    block_base = jnp.clip((pick(start) + offset).reshape(n_blocks, MOE_BLOCK_ROWS)[:, 0], 0, n_assign)
    block_e = row_e.reshape(n_blocks, MOE_BLOCK_ROWS)[:, 0]
    block_cnt = jnp.sum(valid.reshape(n_blocks, MOE_BLOCK_ROWS).astype(i32), axis=1)
    block_e = jnp.where(jnp.arange(n_blocks) < n_used, block_e, block_e[jnp.maximum(n_used - 1, 0)])
    later_used = jnp.logical_and(experts[None, :] > experts[:, None], (counts > 0)[None, :])
    next_used = jnp.min(jnp.where(later_used, experts[None, :], N_EXPERTS), axis=1)
    next_used = jnp.where(next_used < N_EXPERTS, next_used, -1)
    next_e = jnp.sum(jnp.where(block_e[:, None] == experts[None, :], next_used[None, :], 0), axis=1)
    return (block_e.astype(i32), next_e.astype(i32), n_used.astype(i32).reshape(1), block_cnt,
            block_base.astype(i32), sorted_tok.astype(i32), dest.astype(i32))


def kernel(x, p, g_mix, w_in, w_pool, pool_scale, w_branch_a, sgu_ln_g, sgu_ln_b, w_spatial, b_spatial, w_branch_b, w_merge_gate, b_merge_gate, w_out, g_ffn, w_router_group, b_router_group, w_router_expert, b_router_expert, w_exp_gate, w_exp_up, w_exp_down, g_ple, w_ple_gate, b_ple_gate, w_ple_up, g_final):
    bsz, seq, d = x.shape
    assert (seq, d) == (SEQ, D_MODEL) and g_mix.shape[0] == 1
    n_tok = bsz * seq
    x2d = x.reshape(n_tok, d)
    row2d = lambda v: v.reshape(1, -1)

    za, sb, w_merge_bf, w_a_bf, w_b_bf, w_out_bf, w_pg_bf = _mixer_front(
        x2d, row2d(g_mix[0]), w_in[0].astype(BF16), w_pool[0].astype(BF16), row2d(pool_scale[0]),
        row2d(sgu_ln_g[0]), row2d(sgu_ln_b[0]), w_spatial[0], b_spatial[0][:, :, None],
        [w_merge_gate[0], w_branch_a[0], w_branch_b[0], w_out[0], w_ple_gate[0]])

    w_router = jnp.concatenate(
        [w_router_group[0], jnp.transpose(w_router_expert[0], (1, 0, 2)).reshape(d, N_EXPERTS)], axis=1)
    b_router = jnp.concatenate([b_router_group[0], b_router_expert[0].reshape(N_EXPERTS)])
    pad = LANES - w_router.shape[1]
    w_router = jnp.pad(w_router, ((0, 0), (0, pad)))
    b_router = jnp.pad(b_router, (0, pad))
    w_router_hi = w_router.astype(BF16)
    w_router_lo = (w_router - w_router_hi.astype(F32)).astype(BF16)
    w_router_split = jnp.concatenate([w_router_hi, w_router_lo], axis=1)

    x1, h2p, eid, wts = _mixer_back(
        x2d, za, sb, row2d(g_mix[0]), w_merge_bf, row2d(b_merge_gate[0]), w_a_bf, w_b_bf, w_out_bf,
        row2d(g_ffn[0]), w_router_split, row2d(b_router))

    block_e, next_e, n_used, block_cnt, block_base, sorted_tok, dest = _dispatch_plan(eid[:, :TOP_K])
    ys = _expert_ffn(block_e, next_e, n_used, block_cnt, block_base, sorted_tok, h2p, w_exp_gate[0],
                     w_exp_up[0], w_exp_down[0])
    out = _tail(dest, x1, wts, ys, p[0].reshape(n_tok, PLE_DIM), row2d(g_ple[0]), w_pg_bf,
                row2d(b_ple_gate[0]), w_ple_up[0].astype(BF16), row2d(g_final))
    return out.reshape(bsz, seq, d)
```
